```python
import math
import jax
import jax.numpy as jnp
from jax import lax
import numpy as np

D_MODEL = 1024
BATCH = 16
SEQ = 2048
DEPTH = 4

CTX_LEN = 256
GRID_W = 64
ROPE_THETA = 10000.0
Q_BLOCK = 128
NEG_INF = -1e30
F32 = jnp.float32
MIX_WIDTH = D_MODEL

A_HEADS = 4
A_KV_HEADS = 2
A_HEAD_DIM = 128
S5_CH = D_MODEL // 2
S5_GROUP = 16
S5_GROUPS = S5_CH // S5_GROUP
S5_STATE = 64
C_HEADS = 4
C_Q_LORA = 256
C_KV_LORA = 256
C_NOPE = 128
C_ROPE = 64
C_V = 128
D_HEADS = 4
D_KV_HEADS = 2
D_HEAD_DIM = 128
D_WINDOW = 128
N_EXPERTS = 32
TOP_K = 4
D_FF = D_MODEL
SWIGLU_LIMIT = 7.0
SWIGLU_ALPHA = 1.702
MOE_BLOCK = 256

N_EVEN = (DEPTH + 1) // 2
N_ODD = DEPTH // 2
DEEPNORM_ALPHA = (2.0 * DEPTH) ** 0.25
DEEPNORM_BETA = (8.0 * DEPTH) ** -0.25

_A_Q = A_HEADS * A_HEAD_DIM
_A_KV = A_KV_HEADS * A_HEAD_DIM
AB_CUTS = (_A_Q, _A_Q + _A_KV, _A_Q + 2 * _A_KV)
AB_IN = _A_Q + 2 * _A_KV + S5_CH
_C_IN = C_Q_LORA + C_KV_LORA + C_ROPE
CD_CUTS = (C_Q_LORA, C_Q_LORA + C_KV_LORA, _C_IN, _C_IN + D_HEADS * D_HEAD_DIM,
           _C_IN + (D_HEADS + D_KV_HEADS) * D_HEAD_DIM)
CD_IN = _C_IN + (D_HEADS + 2 * D_KV_HEADS) * D_HEAD_DIM

kernel_name = "hybrid_dit_gqa_s5_mla_swa_moe"


def rms_norm(x, g, eps=1e-6):
    xf = x.astype(F32)
    y = xf * lax.rsqrt(jnp.mean(jnp.square(xf), axis=-1, keepdims=True) + eps)
    return (y * g.astype(F32)).astype(x.dtype)


def layer_norm(x, g, b, eps=1e-5):
    xf = x.astype(F32)
    mu = jnp.mean(xf, axis=-1, keepdims=True)
    var = jnp.mean(jnp.square(xf - mu), axis=-1, keepdims=True)
    y = (xf - mu) * lax.rsqrt(var + eps)
    return (y * g.astype(F32) + b.astype(F32)).astype(x.dtype)


def axial_rope_tables(n_rows, rot_dim):
    row, col = jnp.meshgrid(jnp.arange(n_rows, dtype=F32), jnp.arange(GRID_W, dtype=F32), indexing="ij")
    quarter = rot_dim // 4
    inv_freq = ROPE_THETA ** (-jnp.arange(quarter, dtype=F32) / quarter)
    ang_r = row.reshape(-1, 1) * inv_freq
    ang_c = col.reshape(-1, 1) * inv_freq
    return (jnp.cos(ang_r), jnp.sin(ang_r), jnp.cos(ang_c), jnp.sin(ang_c))


def _rot_half(x, cos, sin):
    x1, x2 = jnp.split(x, 2, axis=-1)
    return jnp.concatenate([x1 * cos - x2 * sin, x2 * cos + x1 * sin], axis=-1)


def apply_axial_rope(x, tabs):
    lead = x.ndim - 3
    cr, sr, cc, sc = (t.reshape(t.shape[0], *((1,) * lead), t.shape[1]).astype(x.dtype) for t in tabs)
    half = x.shape[-1] // 2
    return jnp.concatenate([_rot_half(x[..., :half], cr, sr), _rot_half(x[..., half:], cc, sc)], axis=-1)


def attend(q, k, v, scale, bias=None, sink=None):
    s = jnp.einsum("bqgrd,bkgd->bgrqk", q, k).astype(F32) * scale
    if bias is not None:
        s = s + bias
    if sink is not None:
        sk = jnp.broadcast_to(sink.astype(F32)[None, :, :, None, None], s.shape[:-1] + (1,))
        p = jax.nn.softmax(jnp.concatenate([s, sk], axis=-1), axis=-1)[..., :-1]
    else:
        p = jax.nn.softmax(s, axis=-1)
    return jnp.einsum("bgrqk,bkgd->bqgrd", p.astype(v.dtype), v)


def dense_blocked_attend(q, k, v, scale):
    b, t = q.shape[:2]
    nb = t // Q_BLOCK
    qb = jnp.moveaxis(q.reshape(b, nb, Q_BLOCK, *q.shape[2:]), 1, 0)
    ob = lax.map(lambda qi: attend(qi, k, v, scale), qb)
    return jnp.moveaxis(ob, 0, 1).reshape(b, t, *ob.shape[3:])


def windowed_blocked_attend(q, k_lat, v_lat, k_ctx, v_ctx, sink, scale):
    b, t = q.shape[:2]
    nb = t // Q_BLOCK
    band = Q_BLOCK + 2 * D_WINDOW
    padw = ((0, 0), (D_WINDOW, D_WINDOW), (0, 0), (0, 0))
    kp = jnp.pad(k_lat, padw)
    vp = jnp.pad(v_lat, padw)
    rel = jnp.arange(band)[None, :] - D_WINDOW - jnp.arange(Q_BLOCK)[:, None]
    in_win = jnp.abs(rel) <= D_WINDOW
    ctx_bias = jnp.zeros((Q_BLOCK, k_ctx.shape[1]), F32)

    def block(i):
        start = i * Q_BLOCK
        qi = lax.dynamic_slice_in_dim(q, start, Q_BLOCK, axis=1)
        ki = lax.dynamic_slice_in_dim(kp, start, band, axis=1)
        vi = lax.dynamic_slice_in_dim(vp, start, band, axis=1)
        kpos = start - D_WINDOW + jnp.arange(band)
        valid = in_win & ((kpos >= 0) & (kpos < t))[None, :]
        bias = jnp.concatenate([ctx_bias, jnp.where(valid, 0.0, NEG_INF).astype(F32)], axis=-1)
        return attend(qi, jnp.concatenate([k_ctx, ki], axis=1), jnp.concatenate([v_ctx, vi], axis=1),
                      scale, bias, sink)

    ob = lax.map(block, jnp.arange(nb))
    return jnp.moveaxis(ob, 0, 1).reshape(b, t, *ob.shape[3:])


def _lin_combine(e1, e2):
    a1, b1 = e1
    a2, b2 = e2
    return a1 * a2, a2 * b1 + b2


def s5_scan(lam_bar, b_bar, u, h0, reverse):
    bu = jnp.einsum("gpc,btgc->btgp", b_bar, u.astype(jnp.complex64))
    if h0 is not None:
        edge = -1 if reverse else 0
        bu = bu.at[:, edge].add(lam_bar * h0)
    a = jnp.broadcast_to(lam_bar, bu.shape)
    _, h = lax.associative_scan(_lin_combine, (a, bu), reverse=reverse, axis=1)
    return h


def s5_mixer(u_c, u_l, lam_re, lam_im, log_dt, b_re, b_im, c_re, c_im, d_skip):
    to_groups = lambda u: u.astype(F32).reshape(u.shape[0], u.shape[1], S5_GROUPS, S5_GROUP)
    uc, ul = to_groups(u_c), to_groups(u_l)
    outs_c, outs_l = [], []
    for direction in range(2):
        reverse = direction == 1
        lam = lax.complex(lam_re[direction].astype(F32), lam_im[direction].astype(F32))
        dt = jnp.exp(log_dt[direction].astype(F32))[:, None]
        lam_bar = jnp.exp(lam * dt)
        b_bar = ((lam_bar - 1.0) / lam)[..., None] * lax.complex(b_re[direction].astype(F32),
                                                                 b_im[direction].astype(F32))
        c_mat = lax.complex(c_re[direction].astype(F32), c_im[direction].astype(F32))
        h_c = s5_scan(lam_bar, b_bar, uc, None, reverse)
        h0 = h_c[:, 0] if reverse else h_c[:, -1]
        h_l = s5_scan(lam_bar, b_bar, ul, h0, reverse)
        outs_c.append(jnp.einsum("gcp,btgp->btgc", c_mat, h_c).real)
        outs_l.append(jnp.einsum("gcp,btgp->btgc", c_mat, h_l).real)
    d = d_skip.astype(F32).reshape(S5_GROUPS, S5_GROUP)
    y_c = (outs_c[0] + outs_c[1] + d * uc).reshape(u_c.shape).astype(u_c.dtype)
    y_l = (outs_l[0] + outs_l[1] + d * ul).reshape(u_l.shape).astype(u_l.dtype)
    return y_c, y_l


def half_glu(y, w, b):
    z = jax.nn.gelu(y)
    return z * jax.nn.sigmoid(z @ w + b)


def mixer_ab(hc, hl, tabs_head, w_in, q_norm, k_norm, lam_re, lam_im, log_dt, b_re, b_im, c_re, c_im,
             d_skip, w_glu, b_glu):
    rep = A_HEADS // A_KV_HEADS

    def project(h):
        b, t = h.shape[:2]
        q, k, v, u = jnp.split(h @ w_in, AB_CUTS, axis=-1)
        q = rms_norm(q.reshape(b, t, A_HEADS, A_HEAD_DIM), q_norm)
        k = rms_norm(k.reshape(b, t, A_KV_HEADS, A_HEAD_DIM), k_norm)
        return q, k, v.reshape(b, t, A_KV_HEADS, A_HEAD_DIM), u

    qc, kc, vc, uc = project(hc)
    ql, kl, vl, ul = project(hl)
    ql = apply_axial_rope(ql, tabs_head)
    kl = apply_axial_rope(kl, tabs_head)
    group = lambda q: q.reshape(q.shape[0], q.shape[1], A_KV_HEADS, rep, A_HEAD_DIM)
    scale = A_HEAD_DIM ** -0.5
    ac = attend(group(qc), kc, vc, scale)
    al = dense_blocked_attend(group(ql), jnp.concatenate([kc, kl], axis=1),
                              jnp.concatenate([vc, vl], axis=1), scale)
    sc, sl = s5_mixer(uc, ul, lam_re, lam_im, log_dt, b_re, b_im, c_re, c_im, d_skip)
    sc = half_glu(sc, w_glu, b_glu)
    sl = half_glu(sl, w_glu, b_glu)
    flat = lambda a: a.reshape(a.shape[0], a.shape[1], -1)
    return (jnp.concatenate([flat(ac), sc], axis=-1), jnp.concatenate([flat(al), sl], axis=-1))


def mla_qk(q_nope, q_rope, k_nope, k_rope):
    q = jnp.concatenate([q_nope, q_rope], axis=-1)[:, :, :, None, :]
    k_r = jnp.broadcast_to(k_rope[:, :, None, :], k_nope.shape[:3] + (C_ROPE,))
    return q, jnp.concatenate([k_nope, k_r], axis=-1)


def mixer_cd(hc, hl, tabs_head, tabs_mla, w_in, q_lora_norm, kv_lora_norm, w_uq, w_ukv, sink):
    rep = D_HEADS // D_KV_HEADS

    def project(h):
        b, t = h.shape[:2]
        cq, ckv, k_rope, qd, kd, vd = jnp.split(h @ w_in, CD_CUTS, axis=-1)
        q = (rms_norm(cq, q_lora_norm) @ w_uq).reshape(b, t, C_HEADS, C_NOPE + C_ROPE)
        kv = (rms_norm(ckv, kv_lora_norm) @ w_ukv).reshape(b, t, C_HEADS, C_NOPE + C_V)
        q_nope, q_rope = jnp.split(q, [C_NOPE], axis=-1)
        k_nope, v = jnp.split(kv, [C_NOPE], axis=-1)
        return (q_nope, q_rope, k_nope, k_rope, v,
                qd.reshape(b, t, D_HEADS, D_HEAD_DIM),
                kd.reshape(b, t, D_KV_HEADS, D_HEAD_DIM),
                vd.reshape(b, t, D_KV_HEADS, D_HEAD_DIM))

    qn_c, qr_c, kn_c, kr_c, vm_c, qd_c, kd_c, vd_c = project(hc)
    qn_l, qr_l, kn_l, kr_l, vm_l, qd_l, kd_l, vd_l = project(hl)
    qr_l = apply_axial_rope(qr_l, tabs_mla)
    kr_l = apply_axial_rope(kr_l, tabs_mla)
    qd_l = apply_axial_rope(qd_l, tabs_head)
    kd_l = apply_axial_rope(kd_l, tabs_head)
    qm_c, km_c = mla_qk(qn_c, qr_c, kn_c, kr_c)
    qm_l, km_l = mla_qk(qn_l, qr_l, kn_l, kr_l)
    scale_c = (C_NOPE + C_ROPE) ** -0.5
    mc = attend(qm_c, km_c, vm_c, scale_c)
    ml = dense_blocked_attend(qm_l, jnp.concatenate([km_c, km_l], axis=1),
                              jnp.concatenate([vm_c, vm_l], axis=1), scale_c)
    sink_g = sink.reshape(D_KV_HEADS, rep)
    group = lambda q: q.reshape(q.shape[0], q.shape[1], D_KV_HEADS, rep, D_HEAD_DIM)
    scale_d = D_HEAD_DIM ** -0.5
    wc = attend(group(qd_c), kd_c, vd_c, scale_d, sink=sink_g)
    wl = windowed_blocked_attend(group(qd_l), kd_l, vd_l, kd_c, vd_c, sink_g, scale_d)
    flat = lambda a: a.reshape(a.shape[0], a.shape[1], -1)
    return (jnp.concatenate([flat(mc), flat(wc)], axis=-1), jnp.concatenate([flat(ml), flat(wl)], axis=-1))


def clamped_swiglu(z):
    glu, lin = jnp.split(z, 2, axis=-1)
    glu = jnp.minimum(glu, SWIGLU_LIMIT)
    lin = jnp.clip(lin, -SWIGLU_LIMIT, SWIGLU_LIMIT)
    return glu * jax.nn.sigmoid(SWIGLU_ALPHA * glu) * (lin + 1.0)


def moe(h, w_r, b_r, w_in, b_in, w_out, b_out):
    n_tok, d = h.shape
    logits = (h @ w_r + b_r).astype(F32)
    top_logit, top_idx = lax.top_k(logits, TOP_K)
    gates = jax.nn.softmax(top_logit, axis=-1)
    flat_e = top_idx.reshape(-1)
    n_slots = n_tok * TOP_K
    order = jnp.argsort(flat_e)
    sorted_e = flat_e[order]
    counts = jnp.bincount(flat_e, length=N_EXPERTS)
    padded = (counts + MOE_BLOCK - 1) // MOE_BLOCK * MOE_BLOCK
    pad_end = jnp.cumsum(padded)
    pad_start = pad_end - padded
    start = jnp.cumsum(counts) - counts
    dest = pad_start[sorted_e] + jnp.arange(n_slots) - start[sorted_e]
    n_blocks = -(-n_slots // MOE_BLOCK) + N_EXPERTS
    n_pad = n_blocks * MOE_BLOCK
    slot_tok = jnp.full((n_pad,), n_tok, jnp.int32).at[dest].set((order // TOP_K).astype(jnp.int32))
    slot_gate = jnp.zeros((n_pad,), F32).at[dest].set(gates.reshape(-1)[order])
    block_e = jnp.minimum(jnp.searchsorted(pad_end, jnp.arange(n_blocks) * MOE_BLOCK, side="right"),
                          N_EXPERTS - 1)
    h_ext = jnp.concatenate([h, jnp.zeros((1, d), h.dtype)], axis=0)
    xs = h_ext[slot_tok].reshape(n_blocks, MOE_BLOCK, d)

    def expert_block(args):
        xb, e = args
        z = clamped_swiglu(xb @ w_in[e] + b_in[e])
        return z @ w_out[e] + b_out[e]

    ys = lax.map(expert_block, (xs, block_e)).reshape(n_pad, d)
    ys = ys * slot_gate[:, None].astype(ys.dtype)
    return jnp.zeros((n_tok + 1, d), ys.dtype).at[slot_tok].add(ys)[:n_tok]


def setup_inputs(seed: int = 0) -> dict:
    key = jax.random.key(seed)
    keys = list(jax.random.split(key, 40))
    nrm = lambda shape, scale: jax.random.normal(keys.pop(), shape, F32) * scale
    d = D_MODEL
    inp = {}
    inp["x"] = nrm((BATCH, SEQ, d), 1.0)
    inp["c"] = nrm((BATCH, d), 1.0)
    inp["ctx"] = nrm((BATCH, CTX_LEN, d), 1.0)
    inp["c_ctx"] = nrm((d,), 1.0)
    inp["w_mod"] = nrm((DEPTH, d, 6 * d), 0.5 * d ** -0.5)
    inp["b_mod"] = nrm((DEPTH, 6 * d), 0.02)
    inp["ln_g"] = 1.0 + nrm((DEPTH, 2, d), 0.02)
    inp["ln_b"] = nrm((DEPTH, 2, d), 0.02)
    inp["w_mix_out"] = nrm((DEPTH, MIX_WIDTH, d), DEEPNORM_BETA * MIX_WIDTH ** -0.5)
    inp["w_router"] = nrm((DEPTH, d, N_EXPERTS), d ** -0.5)
    inp["b_router"] = nrm((DEPTH, N_EXPERTS), 0.01)
    inp["w_exp_in"] = nrm((DEPTH, N_EXPERTS, d, 2 * D_FF), d ** -0.5)
    inp["b_exp_in"] = nrm((DEPTH, N_EXPERTS, 2 * D_FF), 0.02)
    inp["w_exp_out"] = nrm((DEPTH, N_EXPERTS, D_FF, d), DEEPNORM_BETA * D_FF ** -0.5)
    inp["b_exp_out"] = nrm((DEPTH, N_EXPERTS, d), 0.02)
    inp["w_in_ab"] = nrm((N_EVEN, d, AB_IN), d ** -0.5)
    inp["a_q_norm"] = 1.0 + nrm((N_EVEN, A_HEAD_DIM), 0.02)
    inp["a_k_norm"] = 1.0 + nrm((N_EVEN, A_HEAD_DIM), 0.02)
    ssm_shape = (N_EVEN, 2, S5_GROUPS, S5_STATE)
    inp["s5_lam_re"] = -0.5 * jnp.exp(nrm(ssm_shape, 0.02))
    inp["s5_lam_im"] = jnp.pi * jnp.arange(S5_STATE, dtype=F32) + nrm(ssm_shape, 0.01)
    inp["s5_log_dt"] = jax.random.uniform(keys.pop(), (N_EVEN, 2, S5_GROUPS), F32,
                                          minval=math.log(1e-3), maxval=math.log(1e-1))
    inp["s5_b_re"] = nrm((N_EVEN, 2, S5_GROUPS, S5_STATE, S5_GROUP), (2 * S5_GROUP) ** -0.5)
    inp["s5_b_im"] = nrm((N_EVEN, 2, S5_GROUPS, S5_STATE, S5_GROUP), (2 * S5_GROUP) ** -0.5)
    inp["s5_c_re"] = nrm((N_EVEN, 2, S5_GROUPS, S5_GROUP, S5_STATE), S5_STATE ** -0.5)
    inp["s5_c_im"] = nrm((N_EVEN, 2, S5_GROUPS, S5_GROUP, S5_STATE), S5_STATE ** -0.5)
    inp["s5_d"] = nrm((N_EVEN, S5_CH), 0.5)
    inp["w_glu"] = nrm((N_EVEN, S5_CH, S5_CH), S5_CH ** -0.5)
    inp["b_glu"] = nrm((N_EVEN, S5_CH), 0.02)
    inp["w_in_cd"] = nrm((N_ODD, d, CD_IN), d ** -0.5)
    inp["c_q_norm"] = 1.0 + nrm((N_ODD, C_Q_LORA), 0.02)
    inp["c_kv_norm"] = 1.0 + nrm((N_ODD, C_KV_LORA), 0.02)
    inp["w_uq"] = nrm((N_ODD, C_Q_LORA, C_HEADS * (C_NOPE + C_ROPE)), C_Q_LORA ** -0.5)
    inp["w_ukv"] = nrm((N_ODD, C_KV_LORA, C_HEADS * (C_NOPE + C_V)), C_KV_LORA ** -0.5)
    inp["d_sink"] = nrm((N_ODD, D_HEADS), 1.0)
    return inp


def reference(x, c, ctx, c_ctx, w_mod, b_mod, ln_g, ln_b, w_mix_out, w_router, b_router,
              w_exp_in, b_exp_in, w_exp_out, b_exp_out, w_in_ab, a_q_norm, a_k_norm,
              s5_lam_re, s5_lam_im, s5_log_dt, s5_b_re, s5_b_im, s5_c_re, s5_c_im, s5_d,
              w_glu, b_glu, w_in_cd, c_q_norm, c_kv_norm, w_uq, w_ukv, d_sink):
    n_lat = x.shape[1]
    n_rows = n_lat // GRID_W
    tabs_head = axial_rope_tables(n_rows, A_HEAD_DIM)
    tabs_mla = axial_rope_tables(n_rows, C_ROPE)
    xl, xc = x, ctx
    silu_c = jax.nn.silu(c)
    silu_ctx = jax.nn.silu(c_ctx)
    for layer in range(DEPTH):
        last = layer == DEPTH - 1
        i = layer // 2
        m_l = jnp.split((silu_c @ w_mod[layer] + b_mod[layer])[:, None, :], 6, axis=-1)
        m_c = jnp.split(silu_ctx @ w_mod[layer] + b_mod[layer], 6, axis=-1)
        hl = xl * (1.0 + m_l[1]) + m_l[0]
        hc = xc * (1.0 + m_c[1]) + m_c[0]
        if layer % 2 == 0:
            oc, ol = mixer_ab(hc, hl, tabs_head, w_in_ab[i], a_q_norm[i], a_k_norm[i],
                              s5_lam_re[i], s5_lam_im[i], s5_log_dt[i], s5_b_re[i], s5_b_im[i],
                              s5_c_re[i], s5_c_im[i], s5_d[i], w_glu[i], b_glu[i])
        else:
            oc, ol = mixer_cd(hc, hl, tabs_head, tabs_mla, w_in_cd[i], c_q_norm[i], c_kv_norm[i],
                              w_uq[i], w_ukv[i], d_sink[i])
        xl = layer_norm(DEEPNORM_ALPHA * xl + m_l[2] * (ol @ w_mix_out[layer]), ln_g[layer, 0], ln_b[layer, 0])
        hl = xl * (1.0 + m_l[4]) + m_l[3]
        moe_args = (w_router[layer], b_router[layer], w_exp_in[layer], b_exp_in[layer],
                    w_exp_out[layer], b_exp_out[layer])
        if last:
            yl = moe(hl.reshape(-1, D_MODEL), *moe_args).reshape(xl.shape)
        else:
            xc = layer_norm(DEEPNORM_ALPHA * xc + m_c[2] * (oc @ w_mix_out[layer]), ln_g[layer, 0], ln_b[layer, 0])
            hc = xc * (1.0 + m_c[4]) + m_c[3]
            n_ctx_tok = hc.shape[0] * hc.shape[1]
            y = moe(jnp.concatenate([hc.reshape(-1, D_MODEL), hl.reshape(-1, D_MODEL)], axis=0), *moe_args)
            yc = y[:n_ctx_tok].reshape(xc.shape)
            yl = y[n_ctx_tok:].reshape(xl.shape)
            xc = layer_norm(DEEPNORM_ALPHA * xc + m_c[5] * yc, ln_g[layer, 1], ln_b[layer, 1])
        xl = layer_norm(DEEPNORM_ALPHA * xl + m_l[5] * yl, ln_g[layer, 1], ln_b[layer, 1])
    return xl
```

```python
import functools
import math

import jax
import jax.numpy as jnp
from jax import lax
from jax.experimental import pallas as pl
from jax.experimental.pallas import tpu as pltpu

F32 = jnp.float32
BF16 = jnp.bfloat16
I32 = jnp.int32

GRID_W = 64
ROPE_THETA = 10000.0
HEAD_DIM = 128
N_HEADS = 4
N_KV_HEADS = 2
C_ROPE = 64
MLA_QK = 256
D_WINDOW = 128
S5_GROUP = 16
S5_STATE = 64
S5_CHUNK = 16
TOP_K = 4
SWIGLU_LIMIT = 7.0
SWIGLU_ALPHA = 1.702
NEG_INF = -1e30
LN_EPS = 1e-5
RMS_EPS = 1e-6

LANES = 128
SUBLANES = 8
VMEM_LIMIT_BYTES = 56 * 1024 * 1024

ROW_TILE = 256
MOE_BLOCK = 256


def _cparams(sem):
    return pltpu.CompilerParams(dimension_semantics=sem, vmem_limit_bytes=VMEM_LIMIT_BYTES)


def _split_bf16(a):
    hi = a.astype(BF16)
    lo = (a - hi.astype(F32)).astype(BF16)
    return hi, lo


def _dot(a, b):
    return jnp.dot(a, b, preferred_element_type=F32)


def _dot_nt(a, b):
    return lax.dot_general(a, b, (((1,), (1,)), ((), ())), preferred_element_type=F32)


def _layer_norm(x, g, b):
    mu = jnp.mean(x, axis=-1, keepdims=True)
    xc = x - mu
    var = jnp.mean(xc * xc, axis=-1, keepdims=True)
    return xc * lax.rsqrt(var + LN_EPS) * g + b


def _rms(x, g):
    return x * lax.rsqrt(jnp.mean(x * x, axis=-1, keepdims=True) + RMS_EPS) * g


def _mod_kernel(c_ref, w_ref, b_ref, o_ref):
    c = c_ref[...]
    s = c * jax.nn.sigmoid(c)
    s_hi, s_lo = _split_bf16(s)
    w_hi, w_lo = _split_bf16(w_ref[...])
    o_ref[...] = _dot(s_hi, w_hi) + _dot(s_lo, w_hi) + _dot(s_hi, w_lo) + b_ref[...]


def modulation(cvec, w_mod, b_mod):
    n_layers, d, d6 = w_mod.shape
    r = cvec.shape[0]
    tn = 1536
    return pl.pallas_call(
        _mod_kernel,
        grid=(n_layers, d6 // tn),
        in_specs=[
            pl.BlockSpec((r, d), lambda l, j: (0, 0)),
            pl.BlockSpec((None, d, tn), lambda l, j: (l, 0, j)),
            pl.BlockSpec((None, 1, tn), lambda l, j: (l, 0, j)),
        ],
        out_specs=pl.BlockSpec((None, r, tn), lambda l, j: (l, 0, j)),
        out_shape=jax.ShapeDtypeStruct((n_layers, r, d6), F32),
        compiler_params=_cparams(("arbitrary", "arbitrary")),
        name="modulation",
    )(cvec, w_mod, b_mod.reshape(n_layers, 1, d6))


def _rope_lanes(t, cos, sin, quarter):
    lane = lax.broadcasted_iota(I32, t.shape, 1)
    first = (lane % (2 * quarter)) < quarter
    partner = jnp.where(first, pltpu.roll(t, LANES - quarter, 1), pltpu.roll(t, quarter, 1))
    return t * cos + partner * sin


def _rope_tables(n_ctx, n_lat, rot_dim):
    quarter = rot_dim // 4
    pos = jnp.arange(n_lat, dtype=F32)
    row = jnp.floor(pos / GRID_W)
    col = pos - row * GRID_W
    inv_freq = ROPE_THETA ** (-jnp.arange(quarter, dtype=F32) / quarter)
    ang_r = row[:, None] * inv_freq
    ang_c = col[:, None] * inv_freq
    cos = jnp.concatenate([jnp.cos(ang_r)] * 2 + [jnp.cos(ang_c)] * 2, axis=-1)
    sin = jnp.concatenate([-jnp.sin(ang_r), jnp.sin(ang_r), -jnp.sin(ang_c), jnp.sin(ang_c)], axis=-1)
    pad = LANES - rot_dim
    cos = jnp.pad(cos, ((n_ctx, 0), (0, pad)), constant_values=1.0)
    sin = jnp.pad(sin, ((n_ctx, 0), (0, pad)))
    return cos, sin


def _inproj_ab_kernel(x_ref, mod_ref, w_ref, qn_ref, kn_ref, cos_ref, sin_ref,
                      q_ref, k_ref, v_ref, u_ref, *, scale):
    x = x_ref[...]
    h = (x * (1.0 + mod_ref[1:2, :]) + mod_ref[0:1, :]).astype(BF16)
    y = _dot(h, w_ref[...])
    cos = cos_ref[...]
    sin = sin_ref[...]
    rep = N_HEADS // N_KV_HEADS
    for hh in range(N_HEADS):
        qh = _rms(y[:, hh * HEAD_DIM:(hh + 1) * HEAD_DIM], qn_ref[...])
        qh = _rope_lanes(qh, cos, sin, HEAD_DIM // 4) * scale
        q_ref[hh // rep, hh % rep] = qh.astype(BF16)
    k0 = N_HEADS * HEAD_DIM
    v0 = k0 + N_KV_HEADS * HEAD_DIM
    for g in range(N_KV_HEADS):
        kh = _rms(y[:, k0 + g * HEAD_DIM:k0 + (g + 1) * HEAD_DIM], kn_ref[...])
        k_ref[g] = _rope_lanes(kh, cos, sin, HEAD_DIM // 4).astype(BF16)
        v_ref[g] = y[:, v0 + g * HEAD_DIM:v0 + (g + 1) * HEAD_DIM].astype(BF16)
    u_ref[...] = y[:, v0 + N_KV_HEADS * HEAD_DIM:]


def inproj_ab(x, modtab, w_in, q_norm, k_norm, cos, sin, n_ctx):
    b, t, d = x.shape
    n_in = w_in.shape[1]
    s5_ch = n_in - (N_HEADS + 2 * N_KV_HEADS) * HEAD_DIM
    rep = N_HEADS // N_KV_HEADS
    tq = ROW_TILE
    nct = n_ctx // tq
    seg = lambda i: jnp.minimum(i // nct, 1) if nct > 0 else 1
    kern = functools.partial(_inproj_ab_kernel, scale=HEAD_DIM ** -0.5)
    return pl.pallas_call(
        kern,
        grid=(b, t // tq),
        in_specs=[
            pl.BlockSpec((None, tq, d), lambda bi, i: (bi, i, 0)),
            pl.BlockSpec((None, None, 6, d), lambda bi, i: (bi, seg(i), 0, 0)),
            pl.BlockSpec((d, n_in), lambda bi, i: (0, 0)),
            pl.BlockSpec((1, HEAD_DIM), lambda bi, i: (0, 0)),
            pl.BlockSpec((1, HEAD_DIM), lambda bi, i: (0, 0)),
            pl.BlockSpec((tq, LANES), lambda bi, i: (i, 0)),
            pl.BlockSpec((tq, LANES), lambda bi, i: (i, 0)),
        ],
        out_specs=[
            pl.BlockSpec((None, N_KV_HEADS, rep, tq, HEAD_DIM), lambda bi, i: (bi, 0, 0, i, 0)),
            pl.BlockSpec((None, N_KV_HEADS, tq, HEAD_DIM), lambda bi, i: (bi, 0, i, 0)),
            pl.BlockSpec((None, N_KV_HEADS, tq, HEAD_DIM), lambda bi, i: (bi, 0, i, 0)),
            pl.BlockSpec((None, tq, s5_ch), lambda bi, i: (bi, i, 0)),
        ],
        out_shape=[
            jax.ShapeDtypeStruct((b, N_KV_HEADS, rep, t, HEAD_DIM), BF16),
            jax.ShapeDtypeStruct((b, N_KV_HEADS, t, HEAD_DIM), BF16),
            jax.ShapeDtypeStruct((b, N_KV_HEADS, t, HEAD_DIM), BF16),
            jax.ShapeDtypeStruct((b, t, s5_ch), F32),
        ],
        compiler_params=_cparams(("parallel", "parallel")),
        name="inproj_ab",
    )(x, modtab, w_in, q_norm.reshape(1, -1), k_norm.reshape(1, -1), cos, sin)


def _inproj_cd_kernel(x_ref, mod_ref, w_ref, qln_ref, kvln_ref, wuq_ref, wukv_ref,
                      cos_ref, sin_ref, cosm_ref, sinm_ref,
                      qm_ref, km_ref, vm_ref, qd_ref, kd_ref, vd_ref, *, scale_c, scale_d, q_lora, kv_lora):
    x = x_ref[...]
    h = (x * (1.0 + mod_ref[1:2, :]) + mod_ref[0:1, :]).astype(BF16)
    y = _dot(h, w_ref[...])
    cos = cos_ref[...]
    sin = sin_ref[...]
    cosm = cosm_ref[...]
    sinm = sinm_ref[...]
    rep = N_HEADS // N_KV_HEADS
    cq = _rms(y[:, :q_lora], qln_ref[...]).astype(BF16)
    ckv = _rms(y[:, q_lora:q_lora + kv_lora], kvln_ref[...]).astype(BF16)
    q = _dot(cq, wuq_ref[...])
    kv = _dot(ckv, wukv_ref[...])
    o = q_lora + kv_lora
    qd0, kd0 = o, o + N_HEADS * HEAD_DIM
    vd0 = kd0 + N_KV_HEADS * HEAD_DIM
    kr0 = vd0 + N_KV_HEADS * HEAD_DIM
    k_rope = _rope_lanes(y[:, kr0:kr0 + LANES], cosm, sinm, C_ROPE // 4)
    for hh in range(N_HEADS):
        qn = q[:, hh * MLA_QK:hh * MLA_QK + HEAD_DIM]
        qr = _rope_lanes(q[:, hh * MLA_QK + HEAD_DIM:(hh + 1) * MLA_QK], cosm, sinm, C_ROPE // 4)
        qm_ref[hh, 0] = (jnp.concatenate([qn, qr], axis=-1) * scale_c).astype(BF16)
        kn = kv[:, hh * HEAD_DIM:(hh + 1) * HEAD_DIM]
        km_ref[hh] = jnp.concatenate([kn, k_rope], axis=-1).astype(BF16)
        vm_ref[hh] = kv[:, (N_HEADS + hh) * HEAD_DIM:(N_HEADS + hh + 1) * HEAD_DIM].astype(BF16)
        qdh = _rope_lanes(y[:, qd0 + hh * HEAD_DIM:qd0 + (hh + 1) * HEAD_DIM], cos, sin, HEAD_DIM // 4)
        qd_ref[hh // rep, hh % rep] = (qdh * scale_d).astype(BF16)
    for g in range(N_KV_HEADS):
        kdh = _rope_lanes(y[:, kd0 + g * HEAD_DIM:kd0 + (g + 1) * HEAD_DIM], cos, sin, HEAD_DIM // 4)
        kd_ref[g] = kdh.astype(BF16)
        vd_ref[g] = y[:, vd0 + g * HEAD_DIM:vd0 + (g + 1) * HEAD_DIM].astype(BF16)


def inproj_cd(x, modtab, w_in, q_ln, kv_ln, w_uq, w_ukv, cos, sin, cosm, sinm, n_ctx):
    b, t, d = x.shape
    n_in = w_in.shape[1]
    q_lora, kv_lora = q_ln.shape[0], kv_ln.shape[0]
    rep = N_HEADS // N_KV_HEADS
    tq = ROW_TILE
    nct = n_ctx // tq
    seg = lambda i: jnp.minimum(i // nct, 1) if nct > 0 else 1
    kern = functools.partial(_inproj_cd_kernel, scale_c=(HEAD_DIM + C_ROPE) ** -0.5,
                             scale_d=HEAD_DIM ** -0.5, q_lora=q_lora, kv_lora=kv_lora)
    full = lambda shape: pl.BlockSpec(shape, lambda bi, i: (0,) * len(shape))
    tab = pl.BlockSpec((tq, LANES), lambda bi, i: (i, 0))
    return pl.pallas_call(
        kern,
        grid=(b, t // tq),
        in_specs=[
            pl.BlockSpec((None, tq, d), lambda bi, i: (bi, i, 0)),
            pl.BlockSpec((None, None, 6, d), lambda bi, i: (bi, seg(i), 0, 0)),
            full((d, n_in)), full((1, q_lora)), full((1, kv_lora)),
            full(w_uq.shape), full(w_ukv.shape), tab, tab, tab, tab,
        ],
        out_specs=[
            pl.BlockSpec((None, N_HEADS, 1, tq, MLA_QK), lambda bi, i: (bi, 0, 0, i, 0)),
            pl.BlockSpec((None, N_HEADS, tq, MLA_QK), lambda bi, i: (bi, 0, i, 0)),
            pl.BlockSpec((None, N_HEADS, tq, HEAD_DIM), lambda bi, i: (bi, 0, i, 0)),
            pl.BlockSpec((None, N_KV_HEADS, rep, tq, HEAD_DIM), lambda bi, i: (bi, 0, 0, i, 0)),
            pl.BlockSpec((None, N_KV_HEADS, tq, HEAD_DIM), lambda bi, i: (bi, 0, i, 0)),
            pl.BlockSpec((None, N_KV_HEADS, tq, HEAD_DIM), lambda bi, i: (bi, 0, i, 0)),
        ],
        out_shape=[
            jax.ShapeDtypeStruct((b, N_HEADS, 1, t, MLA_QK), BF16),
            jax.ShapeDtypeStruct((b, N_HEADS, t, MLA_QK), BF16),
            jax.ShapeDtypeStruct((b, N_HEADS, t, HEAD_DIM), BF16),
            jax.ShapeDtypeStruct((b, N_KV_HEADS, rep, t, HEAD_DIM), BF16),
            jax.ShapeDtypeStruct((b, N_KV_HEADS, t, HEAD_DIM), BF16),
            jax.ShapeDtypeStruct((b, N_KV_HEADS, t, HEAD_DIM), BF16),
        ],
        compiler_params=_cparams(("parallel", "parallel")),
        name="inproj_cd",
    )(x, modtab, w_in, q_ln.reshape(1, -1), kv_ln.reshape(1, -1), w_uq, w_ukv, cos, sin, cosm, sinm)


def _prep_cd_weights(w_in, w_uq, w_ukv):
    d = w_in.shape[0]
    q_lora = w_uq.shape[0]
    kv_lora = w_ukv.shape[0]
    o = q_lora + kv_lora
    k_rope = w_in[:, o:o + C_ROPE]
    rest = w_in[:, o + C_ROPE:]
    w_in_r = jnp.concatenate([w_in[:, :o], rest, k_rope, jnp.zeros((d, LANES - C_ROPE), w_in.dtype)], axis=1)
    uq = w_uq.reshape(q_lora, N_HEADS, HEAD_DIM + C_ROPE)
    uq = jnp.pad(uq, ((0, 0), (0, 0), (0, MLA_QK - HEAD_DIM - C_ROPE))).reshape(q_lora, N_HEADS * MLA_QK)
    ukv = w_ukv.reshape(kv_lora, N_HEADS, 2, HEAD_DIM).transpose(0, 2, 1, 3).reshape(kv_lora, 2 * N_HEADS * HEAD_DIM)
    return w_in_r.astype(BF16), uq.astype(BF16), ukv.astype(BF16)


def _softmax_pv(blocks, sink_col):
    mx = None
    for s, _ in blocks:
        bm = jnp.max(s, axis=-1, keepdims=True)
        mx = bm if mx is None else jnp.maximum(mx, bm)
    if sink_col is not None:
        mx = jnp.maximum(mx, sink_col)
    den = None
    acc = None
    for s, v in blocks:
        p = jnp.exp(s - mx)
        ps = jnp.sum(p, axis=-1, keepdims=True)
        den = ps if den is None else den + ps
        pv = _dot(p.astype(BF16), v)
        acc = pv if acc is None else acc + pv
    if sink_col is not None:
        den = den + jnp.exp(sink_col - mx)
    return acc / den


def _attn_kernel(sink_ref, q_ref, k_ref, v_ref, o_ref, *, n_ctx, tq, rep, windowed, use_sink):
    g = pl.program_id(1)
    qi = pl.program_id(2)
    t_all = k_ref.shape[0]
    dk = q_ref.shape[-1]
    dv = v_ref.shape[-1]
    q = q_ref[...].reshape(rep * tq, dk)
    nct = n_ctx // tq

    if use_sink:
        row = lax.broadcasted_iota(I32, (rep * tq, 1), 0)
        sink_col = jnp.full((rep * tq, 1), sink_ref[g * rep], F32)
        for r in range(1, rep):
            sink_col = jnp.where(row >= r * tq, sink_ref[g * rep + r], sink_col)
    else:
        sink_col = None

    def emit(o):
        for r in range(rep):
            o_ref[:, r * dv:(r + 1) * dv] = o[r * tq:(r + 1) * tq].astype(o_ref.dtype)

    @pl.when(qi < nct)
    def _():
        kc = k_ref[0:n_ctx, :]
        vc = v_ref[0:n_ctx, :]
        emit(_softmax_pv([(_dot_nt(q, kc), vc)], sink_col))

    @pl.when(qi >= nct)
    def _():
        if not windowed:
            emit(_softmax_pv([(_dot_nt(q, k_ref[...]), v_ref[...])], sink_col))
        else:
            band = tq + 2 * D_WINDOW
            s0 = (qi - nct) * tq
            kstart = jnp.clip(n_ctx + s0 - D_WINDOW, n_ctx, t_all - band)
            kstart = pl.multiple_of(kstart, LANES)
            kb = k_ref[pl.ds(kstart, band), :]
            vb = v_ref[pl.ds(kstart, band), :]
            sb = _dot_nt(q, kb)
            rowq = lax.broadcasted_iota(I32, (rep * tq, band), 0) % tq + s0
            colk = lax.broadcasted_iota(I32, (rep * tq, band), 1) + (kstart - n_ctx)
            sb = jnp.where(jnp.abs(colk - rowq) <= D_WINDOW, sb, NEG_INF)
            kc = k_ref[0:n_ctx, :]
            vc = v_ref[0:n_ctx, :]
            emit(_softmax_pv([(_dot_nt(q, kc), vc), (sb, vb)], sink_col))


def attention(q, k, v, n_ctx, *, windowed=False, sink=None):
    b, g, rep, t, dk = q.shape
    dv = v.shape[-1]
    tq = ROW_TILE
    use_sink = sink is not None
    if sink is None:
        sink = jnp.zeros((g * rep,), F32)
    kern = functools.partial(_attn_kernel, n_ctx=n_ctx, tq=tq, rep=rep, windowed=windowed, use_sink=use_sink)
    return pl.pallas_call(
        kern,
        grid=(b, g, t // tq),
        in_specs=[
            pl.BlockSpec(memory_space=pltpu.SMEM),
            pl.BlockSpec((None, None, rep, tq, dk), lambda bi, gi, i: (bi, gi, 0, i, 0)),
            pl.BlockSpec((None, None, t, dk), lambda bi, gi, i: (bi, gi, 0, 0)),
            pl.BlockSpec((None, None, t, dv), lambda bi, gi, i: (bi, gi, 0, 0)),
        ],
        out_specs=pl.BlockSpec((None, tq, rep * dv), lambda bi, gi, i: (bi, i, gi)),
        out_shape=jax.ShapeDtypeStruct((b, t, g * rep * dv), BF16),
        compiler_params=_cparams(("parallel", "parallel", "arbitrary")),
        name="attention_win" if windowed else "attention",
    )(sink.astype(F32), q, k, v)


def _s5_tables(lam_re, lam_im, log_dt, b_re, b_im, c_re, c_im, d_skip):
    n_groups, n_state = lam_re.shape[1:]
    L, gs = S5_CHUNK, S5_GROUP
    lam = lax.complex(lam_re.astype(F32), lam_im.astype(F32))
    dt = jnp.exp(log_dt.astype(F32))[..., None]
    lam_dt = lam * dt
    lam_bar = jnp.exp(lam_dt)
    b_bar = ((lam_bar - 1.0) / lam)[..., None] * lax.complex(b_re.astype(F32), b_im.astype(F32))
    c_mat = lax.complex(c_re.astype(F32), c_im.astype(F32))
    pw = jnp.exp(lam_dt[None] * jnp.arange(L + 1, dtype=F32)[:, None, None, None])
    kker = jnp.einsum("dgop,tdgp,dgpi->dtgoi", c_mat, pw[:L], b_bar).real
    s_in = jnp.arange(L)[:, None]
    s_out = jnp.arange(L)[None, :]
    tau_f = s_out - s_in
    tau_r = s_in - s_out
    kf = jnp.where((tau_f >= 0)[:, :, None, None, None], kker[0][jnp.clip(tau_f, 0, L - 1)], 0.0)
    kr = jnp.where((tau_r >= 0)[:, :, None, None, None], kker[1][jnp.clip(tau_r, 0, L - 1)], 0.0)
    kt = kf + kr
    tmat = kt.transpose(2, 0, 4, 1, 3).reshape(n_groups, L * gs, L * gs)
    dmat = jnp.eye(L * gs, dtype=F32)[None] * jnp.tile(d_skip.astype(F32).reshape(n_groups, 1, gs), (1, 1, L))
    tmat = tmat + dmat
    pad = LANES - n_state
    wf = pw[:L][::-1, 0][:, :, :, None] * b_bar[0][None]
    wr = pw[:L][:, 1][:, :, :, None] * b_bar[1][None]
    def w_block(w):
        w = w.transpose(1, 0, 3, 2).reshape(n_groups, L * gs, n_state)
        return [jnp.pad(w.real, ((0, 0), (0, 0), (0, pad))), jnp.pad(w.imag, ((0, 0), (0, 0), (0, pad)))]
    wmat = jnp.concatenate(w_block(wf) + w_block(wr), axis=-1)
    vf = c_mat[0][None] * pw[1:L + 1, 0][:, :, None, :]
    vr = c_mat[1][None] * pw[1:L + 1][::-1, 1][:, :, None, :]
    def v_block(vv):
        vv = vv.transpose(1, 3, 0, 2).reshape(n_groups, n_state, L * gs)
        return [jnp.pad(vv.real, ((0, 0), (0, pad), (0, 0))), jnp.pad(-vv.imag, ((0, 0), (0, pad), (0, 0)))]
    vmat = jnp.concatenate(v_block(vf) + v_block(vr), axis=1)
    lam_l = pw[L]
    lam_row = jnp.stack([jnp.pad(lam_l[0].real, ((0, 0), (0, pad))), jnp.pad(lam_l[0].imag, ((0, 0), (0, pad))),
                         jnp.pad(lam_l[1].real, ((0, 0), (0, pad))), jnp.pad(lam_l[1].imag, ((0, 0), (0, pad)))],
                        axis=1)
    return tmat.astype(BF16), wmat.astype(BF16), vmat.astype(BF16), lam_row


def _s5_kernel(u_ref, t_ref, w_ref, v_ref, lam_ref, y_ref, s_scr, h_scr, *, n_ctx_chunks):
    nch, b, width = u_ref.shape
    u = u_ref[...].reshape(nch * b, width).astype(BF16)
    y_ref[...] = _dot(u, t_ref[...]).reshape(nch, b, width)
    s_scr[...] = _dot(u, w_ref[...]).reshape(nch, b, 4 * LANES)
    lam = lam_ref[...]
    lfr, lfi, lrr, lri = (jnp.broadcast_to(lam[i:i + 1, :], (b, LANES)) for i in range(4))
    zero = jnp.zeros((b, LANES), F32)

    def step(i, carry):
        fr, fi, rr, ri = carry
        j = jnp.where(i < n_ctx_chunks, n_ctx_chunks - 1 - i, nch - 1 - (i - n_ctx_chunks))
        h_scr[i, :, 0 * LANES:1 * LANES] = fr
        h_scr[i, :, 1 * LANES:2 * LANES] = fi
        h_scr[j, :, 2 * LANES:3 * LANES] = rr
        h_scr[j, :, 3 * LANES:4 * LANES] = ri
        sf_r = s_scr[i, :, 0 * LANES:1 * LANES]
        sf_i = s_scr[i, :, 1 * LANES:2 * LANES]
        sr_r = s_scr[j, :, 2 * LANES:3 * LANES]
        sr_i = s_scr[j, :, 3 * LANES:4 * LANES]
        nfr = lfr * fr - lfi * fi + sf_r
        nfi = lfr * fi + lfi * fr + sf_i
        nrr = lrr * rr - lri * ri + sr_r
        nri = lrr * ri + lri * rr + sr_i
        return nfr, nfi, nrr, nri

    lax.fori_loop(0, nch, step, (zero, zero, zero, zero))
    h = h_scr[...].reshape(nch * b, 4 * LANES).astype(BF16)
    y_ref[...] += _dot(h, v_ref[...]).reshape(nch, b, width)


def s5_apply(u, tables, n_ctx):
    tmat, wmat, vmat, lam_row = tables
    b, t, ch = u.shape
    n_groups = ch // S5_GROUP
    L = S5_CHUNK
    nch = t // L
    width = L * S5_GROUP
    ug = u.reshape(b, nch, L, n_groups, S5_GROUP).transpose(3, 1, 0, 2, 4).reshape(n_groups, nch, b, width)
    kern = functools.partial(_s5_kernel, n_ctx_chunks=n_ctx // L)
    yg = pl.pallas_call(
        kern,
        grid=(n_groups,),
        in_specs=[
            pl.BlockSpec((None, nch, b, width), lambda g: (g, 0, 0, 0)),
            pl.BlockSpec((None, width, width), lambda g: (g, 0, 0)),
            pl.BlockSpec((None, width, 4 * LANES), lambda g: (g, 0, 0)),
            pl.BlockSpec((None, 4 * LANES, width), lambda g: (g, 0, 0)),
            pl.BlockSpec((None, 4, LANES), lambda g: (g, 0, 0)),
        ],
        out_specs=pl.BlockSpec((None, nch, b, width), lambda g: (g, 0, 0, 0)),
        out_shape=jax.ShapeDtypeStruct((n_groups, nch, b, width), F32),
        scratch_shapes=[pltpu.VMEM((nch, b, 4 * LANES), F32), pltpu.VMEM((nch, b, 4 * LANES), F32)],
        compiler_params=_cparams(("parallel",)),
        name="s5_scan",
    )(ug, tmat, wmat, vmat, lam_row)
    return yg.reshape(n_groups, nch, b, L, S5_GROUP).transpose(2, 1, 3, 0, 4).reshape(b, t, ch)


def _gelu_tanh(x):
    return 0.5 * x * (1.0 + jnp.tanh(math.sqrt(2.0 / math.pi) * (x + 0.044715 * (x * x * x))))


def _post_kernel(*refs, alpha, glu, n_exp):
    if glu:
        (x_ref, a_ref, s_ref, wg_ref, bg_ref, wm_ref, mod_ref, g_ref, b_ref, wrh_ref, wrl_ref, br_ref,
         x1_ref, h2_ref, idx_ref, gate_ref, rank_ref, cnt_ref, cnt_scr) = refs
    else:
        (x_ref, a_ref, s_ref, wm_ref, mod_ref, g_ref, b_ref, wrh_ref, wrl_ref, br_ref,
         x1_ref, h2_ref, idx_ref, gate_ref, rank_ref, cnt_ref, cnt_scr) = refs
    first = (pl.program_id(0) == 0) & (pl.program_id(1) == 0)

    @pl.when(first)
    def _():
        cnt_scr[...] = jnp.zeros_like(cnt_scr)

    tq, d = x_ref.shape
    half = a_ref.shape[1]
    if glu:
        z = _gelu_tanh(s_ref[...])
        gate = jax.nn.sigmoid(_dot(z.astype(BF16), wg_ref[...]) + bg_ref[...])
        second = (z * gate).astype(BF16)
    else:
        second = s_ref[...]
    mix = _dot(a_ref[...], wm_ref[0:half, :]) + _dot(second, wm_ref[half:, :])
    x1 = _layer_norm(alpha * x_ref[...] + mod_ref[2:3, :] * mix, g_ref[...], b_ref[...])
    x1_ref[...] = x1
    h2 = x1 * (1.0 + mod_ref[4:5, :]) + mod_ref[3:4, :]
    for s in range(d // LANES):
        h2_ref[:, s, :] = h2[:, s * LANES:(s + 1) * LANES]

    h_hi, h_lo = _split_bf16(h2)
    logits = _dot_nt(wrh_ref[...], h_hi) + _dot_nt(wrh_ref[...], h_lo) + _dot_nt(wrl_ref[...], h_hi) + br_ref[...]
    eidx = lax.broadcasted_iota(I32, (n_exp, tq), 0)
    work = logits
    tops, sels = [], []
    for k in range(TOP_K):
        m = jnp.max(work, axis=0, keepdims=True)
        ik = jnp.min(jnp.where(work == m, eidx, n_exp), axis=0, keepdims=True)
        sel = eidx == ik
        work = jnp.where(sel, -jnp.inf, work)
        tops.append(m)
        sels.append(sel)
        idx_ref[k:k + 1, :] = ik
    exps = [jnp.exp(tk - tops[0]) for tk in tops]
    den = exps[0] + exps[1] + exps[2] + exps[3]
    for k in range(TOP_K):
        gate_ref[k:k + 1, :] = exps[k] / den
    onehot = jnp.zeros((n_exp, tq), F32)
    for sel in sels:
        onehot = onehot + sel.astype(F32)
    tri = (lax.broadcasted_iota(I32, (tq, tq), 0) < lax.broadcasted_iota(I32, (tq, tq), 1)).astype(BF16)
    before = _dot(onehot.astype(BF16), tri) + cnt_scr[...]
    for k in range(TOP_K):
        rk = jnp.sum(jnp.where(sels[k], before, 0.0), axis=0, keepdims=True)
        rank_ref[k:k + 1, :] = rk.astype(I32)
    cnt_scr[...] += jnp.sum(onehot, axis=1, keepdims=True)
    cnt_ref[...] = jnp.broadcast_to(cnt_scr[...], cnt_ref.shape).astype(I32)


def post_mixer(x, a, s, modtab, w_mix, ln_g, ln_b, w_router, b_router, n_ctx, alpha, glu_w=None, glu_b=None):
    b, t, d = x.shape
    half = a.shape[-1]
    n_exp = w_router.shape[1]
    tq = ROW_TILE
    nct = n_ctx // tq
    seg = lambda i: jnp.minimum(i // nct, 1) if nct > 0 else 1
    glu = glu_w is not None
    wr_hi, wr_lo = _split_bf16(w_router.T.astype(F32))
    full = lambda shape: pl.BlockSpec(shape, lambda bi, i: (0,) * len(shape))
    tok = lambda width: pl.BlockSpec((None, tq, width), lambda bi, i: (bi, i, 0))
    in_specs = [tok(d), tok(half), tok(half)]
    args = [x, a, s]
    if glu:
        in_specs += [full((half, half)), full((1, half))]
        args += [glu_w.astype(BF16), glu_b.reshape(1, half)]
    in_specs += [full((d, d)), pl.BlockSpec((None, None, 6, d), lambda bi, i: (bi, seg(i), 0, 0)),
                 full((1, d)), full((1, d)), full((n_exp, d)), full((n_exp, d)), full((n_exp, 1))]
    args += [w_mix.astype(BF16), modtab, ln_g.reshape(1, d), ln_b.reshape(1, d), wr_hi, wr_lo,
             b_router.reshape(n_exp, 1)]
    nt = t // tq
    lane_out = lambda dt: (pl.BlockSpec((TOP_K, tq), lambda bi, i: (0, bi * nt + i)),
                           jax.ShapeDtypeStruct((TOP_K, b * t), dt))
    outs = [
        (tok(d), jax.ShapeDtypeStruct((b, t, d), F32)),
        (pl.BlockSpec((None, tq, d // LANES, LANES), lambda bi, i: (bi, i, 0, 0)),
         jax.ShapeDtypeStruct((b, t, d // LANES, LANES), F32)),
        lane_out(I32), lane_out(F32), lane_out(I32),
        (pl.BlockSpec((n_exp, LANES), lambda bi, i: (0, 0)), jax.ShapeDtypeStruct((n_exp, LANES), I32)),
    ]
    kern = functools.partial(_post_kernel, alpha=alpha, glu=glu, n_exp=n_exp)
    return pl.pallas_call(
        kern,
        grid=(b, nt),
        in_specs=in_specs,
        out_specs=[o[0] for o in outs],
        out_shape=[o[1] for o in outs],
        scratch_shapes=[pltpu.VMEM((n_exp, 1), F32)],
        compiler_params=_cparams(("arbitrary", "arbitrary")),
        name="post_mixer_glu" if glu else "post_mixer",
    )(*args)


def _row_gather_start(src_hbm, idx_smem, dst, sem, n_rows):
    for r in range(n_rows):
        pltpu.make_async_copy(src_hbm.at[idx_smem[0, r]], dst.at[r], sem).start()


def _moe_kernel(be_ref, nb_ref, tok_hbm, h_hbm, wi_ref, bi_ref, wo_ref, bo_ref, y_ref,
                idx_smem, xbuf, idx_sem, row_sem, *, tm, n_blocks):
    j = pl.program_id(0)
    n_used = nb_ref[0]
    slot = j % 2
    d = wi_ref.shape[0]

    def idx_copy(blk, sl):
        return pltpu.make_async_copy(tok_hbm.at[blk], idx_smem.at[sl], idx_sem.at[sl])

    def rows_wait(sl):
        pltpu.make_async_copy(h_hbm.at[pl.ds(0, tm)], xbuf.at[sl], row_sem.at[sl]).wait()

    @pl.when(j == 0)
    def _():
        idx_copy(0, 0).start()
        idx_copy(0, 0).wait()
        _row_gather_start(h_hbm, idx_smem.at[0], xbuf.at[0], row_sem.at[0], tm)

        @pl.when(n_used > 1)
        def _():
            idx_copy(1, 1).start()

    @pl.when(j < n_used)
    def _():
        @pl.when(j + 1 < n_used)
        def _():
            idx_copy(j + 1, 1 - slot).wait()
            _row_gather_start(h_hbm, idx_smem.at[1 - slot], xbuf.at[1 - slot], row_sem.at[1 - slot], tm)

        rows_wait(slot)

        @pl.when(j + 2 < n_used)
        def _():
            idx_copy(j + 2, slot).start()

        xs = xbuf.at[slot]
        parts = [xs[:, s, :] for s in range(d // LANES)]
        x = jnp.concatenate(parts, axis=-1).astype(BF16)
        z = _dot(x, wi_ref[...]) + bi_ref[...]
        f = z.shape[1] // 2
        glu = jnp.minimum(z[:, :f], SWIGLU_LIMIT)
        lin = jnp.clip(z[:, f:], -SWIGLU_LIMIT, SWIGLU_LIMIT)
        act = glu * jax.nn.sigmoid(SWIGLU_ALPHA * glu) * (lin + 1.0)
        y = _dot(act.astype(BF16), wo_ref[...]) + bo_ref[...]
        for s in range(d // LANES):
            y_ref[:, s, :] = y[:, s * LANES:(s + 1) * LANES]

    @pl.when(j >= n_used)
    def _():
        y_ref[...] = jnp.zeros_like(y_ref)


def moe_experts(h_rows, slot_tok, block_expert, n_used, w_in, b_in, w_out, b_out, layer):
    n_blocks, _, tm = slot_tok.shape
    _, n_exp, d, f2 = w_in.shape
    grid_spec = pltpu.PrefetchScalarGridSpec(
        num_scalar_prefetch=2,
        grid=(n_blocks,),
        in_specs=[
            pl.BlockSpec(memory_space=pl.ANY),
            pl.BlockSpec(memory_space=pl.ANY),
            pl.BlockSpec((None, None, d, f2), lambda j, be, nb: (layer, be[j], 0, 0)),
            pl.BlockSpec((None, None, 1, f2), lambda j, be, nb: (layer, be[j], 0, 0)),
            pl.BlockSpec((None, None, f2 // 2, d), lambda j, be, nb: (layer, be[j], 0, 0)),
            pl.BlockSpec((None, None, 1, d), lambda j, be, nb: (layer, be[j], 0, 0)),
        ],
        out_specs=pl.BlockSpec((tm, d // LANES, LANES), lambda j, be, nb: (j, 0, 0)),
        scratch_shapes=[
            pltpu.SMEM((2, 1, tm), I32),
            pltpu.VMEM((2, tm, d // LANES, LANES), F32),
            pltpu.SemaphoreType.DMA((2,)),
            pltpu.SemaphoreType.DMA((2,)),
        ],
    )
    kern = functools.partial(_moe_kernel, tm=tm, n_blocks=n_blocks)
    return pl.pallas_call(
        kern,
        grid_spec=grid_spec,
        out_shape=jax.ShapeDtypeStruct((n_blocks * tm, d // LANES, LANES), F32),
        compiler_params=_cparams(("arbitrary",)),
        name="moe_experts",
    )(block_expert, n_used, slot_tok, h_rows, w_in, b_in.reshape(b_in.shape[0], n_exp, 1, f2),
      w_out, b_out.reshape(b_out.shape[0], n_exp, 1, d))


def _combine_kernel(pos_hbm, x_ref, g_ref, mod_ref, lg_ref, lb_ref, ys_hbm, o_ref,
                    idx_smem, ybuf, idx_sem, row_sem, *, alpha, tq, n_tiles):
    i = pl.program_id(0)
    slot = i % 2
    d = x_ref.shape[1]

    def idx_copy(tile, sl):
        return pltpu.make_async_copy(pos_hbm.at[tile], idx_smem.at[sl], idx_sem.at[sl])

    def gather_start(sl):
        _row_gather_start(ys_hbm, idx_smem.at[sl], ybuf.at[sl], row_sem.at[sl], TOP_K * tq)

    def rows_wait(sl):
        pltpu.make_async_copy(ys_hbm.at[pl.ds(0, TOP_K * tq)], ybuf.at[sl], row_sem.at[sl]).wait()

    @pl.when(i == 0)
    def _():
        idx_copy(0, 0).start()
        idx_copy(0, 0).wait()
        gather_start(0)

        @pl.when(n_tiles > 1)
        def _():
            idx_copy(1, 1).start()

    @pl.when(i + 1 < n_tiles)
    def _():
        idx_copy(i + 1, 1 - slot).wait()
        gather_start(1 - slot)

    rows_wait(slot)

    @pl.when(i + 2 < n_tiles)
    def _():
        idx_copy(i + 2, slot).start()

    yb = ybuf.at[slot]
    gates = g_ref[...]
    parts = []
    for s in range(d // LANES):
        acc = None
        for k in range(TOP_K):
            term = gates[:, k:k + 1] * yb[k * tq:(k + 1) * tq, s, :]
            acc = term if acc is None else acc + term
        parts.append(acc)
    y = jnp.concatenate(parts, axis=-1)
    o_ref[...] = _layer_norm(alpha * x_ref[...] + mod_ref[5:6, :] * y, lg_ref[...], lb_ref[...])


def moe_combine(x1, pos_tiles, gates_tok, modtab, ln_g, ln_b, ys, n_ctx, alpha):
    b, t, d = x1.shape
    tq = ROW_TILE
    nt = t // tq
    n_tiles = b * nt
    nct = n_ctx // tq
    seg = lambda i: jnp.minimum((i % nt) // nct, 1) if nct > 0 else 1
    kern = functools.partial(_combine_kernel, alpha=alpha, tq=tq, n_tiles=n_tiles)
    out = pl.pallas_call(
        kern,
        grid=(n_tiles,),
        in_specs=[
            pl.BlockSpec(memory_space=pl.ANY),
            pl.BlockSpec((tq, d), lambda i: (i, 0)),
            pl.BlockSpec((tq, TOP_K), lambda i: (i, 0)),
            pl.BlockSpec((None, None, 6, d), lambda i: (i // nt, seg(i), 0, 0)),
            pl.BlockSpec((1, d), lambda i: (0, 0)),
            pl.BlockSpec((1, d), lambda i: (0, 0)),
            pl.BlockSpec(memory_space=pl.ANY),
        ],
        out_specs=pl.BlockSpec((tq, d), lambda i: (i, 0)),
        out_shape=jax.ShapeDtypeStruct((b * t, d), F32),
        scratch_shapes=[
            pltpu.SMEM((2, 1, TOP_K * tq), I32),
            pltpu.VMEM((2, TOP_K * tq, d // LANES, LANES), F32),
            pltpu.SemaphoreType.DMA((2,)),
            pltpu.SemaphoreType.DMA((2,)),
        ],
        compiler_params=_cparams(("arbitrary",)),
        name="moe_combine",
    )(pos_tiles, x1.reshape(b * t, d), gates_tok, modtab, ln_g.reshape(1, d), ln_b.reshape(1, d), ys)
    return out.reshape(b, t, d)


def _routing_tables(idx, rank, counts, tm, tq):
    n_exp = counts.shape[0]
    n_tok = idx.shape[1]
    n_slots = TOP_K * n_tok
    n_blocks = -(-n_slots // tm) + n_exp
    padded = (counts + tm - 1) // tm * tm
    pad_end = jnp.cumsum(padded)
    pad_start = pad_end - padded
    pos = pad_start[idx] + rank
    tok_ids = jnp.broadcast_to(jnp.arange(n_tok, dtype=I32)[None, :], pos.shape)
    slot_tok = jnp.zeros((n_blocks * tm,), I32).at[pos.reshape(-1)].set(tok_ids.reshape(-1))
    block_e = jnp.minimum(jnp.searchsorted(pad_end, jnp.arange(n_blocks, dtype=I32) * tm, side="right"),
                          n_exp - 1).astype(I32)
    n_used = (pad_end[-1] // tm).astype(I32).reshape(1)
    pos_tiles = pos.reshape(TOP_K, n_tok // tq, tq).transpose(1, 0, 2).reshape(n_tok // tq, 1, TOP_K * tq)
    return slot_tok.reshape(n_blocks, 1, tm), block_e, n_used, pos_tiles


def kernel(x, c, ctx, c_ctx, w_mod, b_mod, ln_g, ln_b, w_mix_out, w_router, b_router, w_exp_in, b_exp_in, w_exp_out, b_exp_out, w_in_ab, a_q_norm, a_k_norm, s5_lam_re, s5_lam_im, s5_log_dt, s5_b_re, s5_b_im, s5_c_re, s5_c_im, s5_d, w_glu, b_glu, w_in_cd, c_q_norm, c_kv_norm, w_uq, w_ukv, d_sink):
    batch, n_lat, d = x.shape
    n_ctx = ctx.shape[1]
    depth = w_mod.shape[0]
    alpha = (2.0 * depth) ** 0.25
    t = n_ctx + n_lat
    n_tok = batch * t

    xs = jnp.concatenate([ctx, x], axis=1)
    n_rows = -(-(batch + 1) // SUBLANES) * SUBLANES
    cvec = jnp.zeros((n_rows, d), F32).at[:batch].set(c).at[batch].set(c_ctx)
    mods = modulation(cvec, w_mod, b_mod).reshape(depth, n_rows, 6, d)
    modtab = jnp.stack([jnp.broadcast_to(mods[:, batch:batch + 1], (depth, batch, 6, d)), mods[:, :batch]],
                       axis=2)

    cos_h, sin_h = _rope_tables(n_ctx, n_lat, HEAD_DIM)
    cos_m, sin_m = _rope_tables(n_ctx, n_lat, C_ROPE)
    w_exp_in_b = w_exp_in.astype(BF16)
    w_exp_out_b = w_exp_out.astype(BF16)

    for layer in range(depth):
        i = layer // 2
        mt = modtab[layer]
        if layer % 2 == 0:
            q, k, v, u = inproj_ab(xs, mt, w_in_ab[i].astype(BF16), a_q_norm[i], a_k_norm[i], cos_h, sin_h, n_ctx)
            att = attention(q, k, v, n_ctx)
            tables = _s5_tables(s5_lam_re[i], s5_lam_im[i], s5_log_dt[i], s5_b_re[i], s5_b_im[i],
                                s5_c_re[i], s5_c_im[i], s5_d[i])
            second = s5_apply(u, tables, n_ctx)
            glu_w, glu_b = w_glu[i], b_glu[i]
        else:
            w_in_r, uq_r, ukv_r = _prep_cd_weights(w_in_cd[i], w_uq[i], w_ukv[i])
            qm, km, vm, qd, kd, vd = inproj_cd(xs, mt, w_in_r, c_q_norm[i], c_kv_norm[i], uq_r, ukv_r,
                                               cos_h, sin_h, cos_m, sin_m, n_ctx)
            att = attention(qm, km, vm, n_ctx)
            second = attention(qd, kd, vd, n_ctx, windowed=True, sink=d_sink[i])
            glu_w = glu_b = None
        x1, h2, idx, gates, rank, counts = post_mixer(
            xs, att, second, mt, w_mix_out[layer], ln_g[layer, 0], ln_b[layer, 0],
            w_router[layer], b_router[layer], n_ctx, alpha, glu_w, glu_b)
        slot_tok, block_e, n_used, pos_tiles = _routing_tables(idx, rank, counts[:, 0], MOE_BLOCK, ROW_TILE)
        ys = moe_experts(h2.reshape(n_tok, d // LANES, LANES), slot_tok, block_e, n_used,
                         w_exp_in_b, b_exp_in, w_exp_out_b, b_exp_out, layer)
        xs = moe_combine(x1, pos_tiles, gates.T, mt, ln_g[layer, 1], ln_b[layer, 1], ys, n_ctx, alpha)
    return xs[:, n_ctx:, :]
```

```python
import functools
import math

import jax
import jax.numpy as jnp
from jax import lax
from jax.experimental import pallas as pl
from jax.experimental.pallas import tpu as pltpu

F32 = jnp.float32
BF16 = jnp.bfloat16
I32 = jnp.int32

GRID_W = 64
ROPE_THETA = 10000.0
HEAD_DIM = 128
N_HEADS = 4
N_KV_HEADS = 2
C_ROPE = 64
MLA_QK = 256
D_WINDOW = 128
S5_GROUP = 16
S5_STATE = 64
S5_CHUNK = 8
S5_PACK = 8
TOP_K = 4
SWIGLU_LIMIT = 7.0
SWIGLU_ALPHA = 1.702
NEG_INF = -1e30
LN_EPS = 1e-5
RMS_EPS = 1e-6

LANES = 128
SUBLANES = 8
VMEM_LIMIT_BYTES = 56 * 1024 * 1024

ROW_TILE = 256
MOE_BLOCK = 256


def _cparams(sem):
    return pltpu.CompilerParams(dimension_semantics=sem, vmem_limit_bytes=VMEM_LIMIT_BYTES)


def _split_bf16(a):
    hi = a.astype(BF16)
    lo = (a - hi.astype(F32)).astype(BF16)
    return hi, lo


def _dot(a, b):
    return jnp.dot(a, b, preferred_element_type=F32)


def _dot_nt(a, b):
    return lax.dot_general(a, b, (((1,), (1,)), ((), ())), preferred_element_type=F32)


def _layer_norm(x, g, b):
    mu = jnp.mean(x, axis=-1, keepdims=True)
    xc = x - mu
    var = jnp.mean(xc * xc, axis=-1, keepdims=True)
    return xc * lax.rsqrt(var + LN_EPS) * g + b


def _rms(x, g):
    return x * lax.rsqrt(jnp.mean(x * x, axis=-1, keepdims=True) + RMS_EPS) * g


def _mod_kernel(c_ref, w_ref, b_ref, o_ref):
    c = c_ref[...]
    s = c * jax.nn.sigmoid(c)
    s_hi, s_lo = _split_bf16(s)
    w_hi, w_lo = _split_bf16(w_ref[...])
    o_ref[...] = _dot(s_hi, w_hi) + _dot(s_lo, w_hi) + _dot(s_hi, w_lo) + b_ref[...]


def modulation(cvec, w_mod, b_mod):
    n_layers, d, d6 = w_mod.shape
    r = cvec.shape[0]
    tn = 1536
    return pl.pallas_call(
        _mod_kernel,
        grid=(n_layers, d6 // tn),
        in_specs=[
            pl.BlockSpec((r, d), lambda l, j: (0, 0)),
            pl.BlockSpec((None, d, tn), lambda l, j: (l, 0, j)),
            pl.BlockSpec((None, 1, tn), lambda l, j: (l, 0, j)),
        ],
        out_specs=pl.BlockSpec((None, r, tn), lambda l, j: (l, 0, j)),
        out_shape=jax.ShapeDtypeStruct((n_layers, r, d6), F32),
        compiler_params=_cparams(("arbitrary", "arbitrary")),
        name="modulation",
    )(cvec, w_mod, b_mod.reshape(n_layers, 1, d6))


def _rope_lanes(t, cos, sin, quarter):
    lane = lax.broadcasted_iota(I32, t.shape, 1)
    first = (lane % (2 * quarter)) < quarter
    partner = jnp.where(first, pltpu.roll(t, LANES - quarter, 1), pltpu.roll(t, quarter, 1))
    return t * cos + partner * sin


def _rope_tables(n_ctx, n_lat, rot_dim):
    quarter = rot_dim // 4
    pos = jnp.arange(n_lat, dtype=F32)
    row = jnp.floor(pos / GRID_W)
    col = pos - row * GRID_W
    inv_freq = ROPE_THETA ** (-jnp.arange(quarter, dtype=F32) / quarter)
    ang_r = row[:, None] * inv_freq
    ang_c = col[:, None] * inv_freq
    cos = jnp.concatenate([jnp.cos(ang_r)] * 2 + [jnp.cos(ang_c)] * 2, axis=-1)
    sin = jnp.concatenate([-jnp.sin(ang_r), jnp.sin(ang_r), -jnp.sin(ang_c), jnp.sin(ang_c)], axis=-1)
    pad = LANES - rot_dim
    cos = jnp.pad(cos, ((n_ctx, 0), (0, pad)), constant_values=1.0)
    sin = jnp.pad(sin, ((n_ctx, 0), (0, pad)))
    return cos, sin


def _inproj_ab_kernel(x_ref, mod_ref, w_ref, qn_ref, kn_ref, cos_ref, sin_ref,
                      q_ref, k_ref, v_ref, u_ref, *, scale):
    x = x_ref[...]
    h = (x * (1.0 + mod_ref[1:2, :]) + mod_ref[0:1, :]).astype(BF16)
    y = _dot(h, w_ref[...])
    cos = cos_ref[...]
    sin = sin_ref[...]
    rep = N_HEADS // N_KV_HEADS
    for hh in range(N_HEADS):
        qh = _rms(y[:, hh * HEAD_DIM:(hh + 1) * HEAD_DIM], qn_ref[...])
        qh = _rope_lanes(qh, cos, sin, HEAD_DIM // 4) * scale
        q_ref[hh // rep, hh % rep] = qh.astype(BF16)
    k0 = N_HEADS * HEAD_DIM
    v0 = k0 + N_KV_HEADS * HEAD_DIM
    for g in range(N_KV_HEADS):
        kh = _rms(y[:, k0 + g * HEAD_DIM:k0 + (g + 1) * HEAD_DIM], kn_ref[...])
        k_ref[g] = _rope_lanes(kh, cos, sin, HEAD_DIM // 4).astype(BF16)
        v_ref[g] = y[:, v0 + g * HEAD_DIM:v0 + (g + 1) * HEAD_DIM].astype(BF16)
    u_ref[...] = y[:, v0 + N_KV_HEADS * HEAD_DIM:].astype(BF16)


def inproj_ab(x, modtab, w_in, q_norm, k_norm, cos, sin, n_ctx):
    b, t, d = x.shape
    n_in = w_in.shape[1]
    s5_ch = n_in - (N_HEADS + 2 * N_KV_HEADS) * HEAD_DIM
    rep = N_HEADS // N_KV_HEADS
    tq = ROW_TILE
    nct = n_ctx // tq
    seg = lambda i: jnp.minimum(i // nct, 1) if nct > 0 else 1
    kern = functools.partial(_inproj_ab_kernel, scale=HEAD_DIM ** -0.5)
    return pl.pallas_call(
        kern,
        grid=(b, t // tq),
        in_specs=[
            pl.BlockSpec((None, tq, d), lambda bi, i: (bi, i, 0)),
            pl.BlockSpec((None, None, 6, d), lambda bi, i: (bi, seg(i), 0, 0)),
            pl.BlockSpec((d, n_in), lambda bi, i: (0, 0)),
            pl.BlockSpec((1, HEAD_DIM), lambda bi, i: (0, 0)),
            pl.BlockSpec((1, HEAD_DIM), lambda bi, i: (0, 0)),
            pl.BlockSpec((tq, LANES), lambda bi, i: (i, 0)),
            pl.BlockSpec((tq, LANES), lambda bi, i: (i, 0)),
        ],
        out_specs=[
            pl.BlockSpec((None, N_KV_HEADS, rep, tq, HEAD_DIM), lambda bi, i: (bi, 0, 0, i, 0)),
            pl.BlockSpec((None, N_KV_HEADS, tq, HEAD_DIM), lambda bi, i: (bi, 0, i, 0)),
            pl.BlockSpec((None, N_KV_HEADS, tq, HEAD_DIM), lambda bi, i: (bi, 0, i, 0)),
            pl.BlockSpec((None, tq, s5_ch), lambda bi, i: (bi, i, 0)),
        ],
        out_shape=[
            jax.ShapeDtypeStruct((b, N_KV_HEADS, rep, t, HEAD_DIM), BF16),
            jax.ShapeDtypeStruct((b, N_KV_HEADS, t, HEAD_DIM), BF16),
            jax.ShapeDtypeStruct((b, N_KV_HEADS, t, HEAD_DIM), BF16),
            jax.ShapeDtypeStruct((b, t, s5_ch), BF16),
        ],
        compiler_params=_cparams(("parallel", "parallel")),
        name="inproj_ab",
    )(x, modtab, w_in, q_norm.reshape(1, -1), k_norm.reshape(1, -1), cos, sin)


def _inproj_cd_kernel(x_ref, mod_ref, w_ref, qln_ref, kvln_ref, wuq_ref, wukv_ref,
                      cos_ref, sin_ref, cosm_ref, sinm_ref,
                      qm_ref, km_ref, vm_ref, qd_ref, kd_ref, vd_ref, *, scale_c, scale_d, q_lora, kv_lora):
    x = x_ref[...]
    h = (x * (1.0 + mod_ref[1:2, :]) + mod_ref[0:1, :]).astype(BF16)
    y = _dot(h, w_ref[...])
    cos = cos_ref[...]
    sin = sin_ref[...]
    cosm = cosm_ref[...]
    sinm = sinm_ref[...]
    rep = N_HEADS // N_KV_HEADS
    cq = _rms(y[:, :q_lora], qln_ref[...]).astype(BF16)
    ckv = _rms(y[:, q_lora:q_lora + kv_lora], kvln_ref[...]).astype(BF16)
    q = _dot(cq, wuq_ref[...])
    kv = _dot(ckv, wukv_ref[...])
    o = q_lora + kv_lora
    qd0, kd0 = o, o + N_HEADS * HEAD_DIM
    vd0 = kd0 + N_KV_HEADS * HEAD_DIM
    kr0 = vd0 + N_KV_HEADS * HEAD_DIM
    k_rope = _rope_lanes(y[:, kr0:kr0 + LANES], cosm, sinm, C_ROPE // 4)
    for hh in range(N_HEADS):
        qn = q[:, hh * MLA_QK:hh * MLA_QK + HEAD_DIM]
        qr = _rope_lanes(q[:, hh * MLA_QK + HEAD_DIM:(hh + 1) * MLA_QK], cosm, sinm, C_ROPE // 4)
        qm_ref[hh, 0] = (jnp.concatenate([qn, qr], axis=-1) * scale_c).astype(BF16)
        kn = kv[:, hh * HEAD_DIM:(hh + 1) * HEAD_DIM]
        km_ref[hh] = jnp.concatenate([kn, k_rope], axis=-1).astype(BF16)
        vm_ref[hh] = kv[:, (N_HEADS + hh) * HEAD_DIM:(N_HEADS + hh + 1) * HEAD_DIM].astype(BF16)
        qdh = _rope_lanes(y[:, qd0 + hh * HEAD_DIM:qd0 + (hh + 1) * HEAD_DIM], cos, sin, HEAD_DIM // 4)
        qd_ref[hh // rep, hh % rep] = (qdh * scale_d).astype(BF16)
    for g in range(N_KV_HEADS):
        kdh = _rope_lanes(y[:, kd0 + g * HEAD_DIM:kd0 + (g + 1) * HEAD_DIM], cos, sin, HEAD_DIM // 4)
        kd_ref[g] = kdh.astype(BF16)
        vd_ref[g] = y[:, vd0 + g * HEAD_DIM:vd0 + (g + 1) * HEAD_DIM].astype(BF16)


def inproj_cd(x, modtab, w_in, q_ln, kv_ln, w_uq, w_ukv, cos, sin, cosm, sinm, n_ctx):
    b, t, d = x.shape
    n_in = w_in.shape[1]
    q_lora, kv_lora = q_ln.shape[0], kv_ln.shape[0]
    rep = N_HEADS // N_KV_HEADS
    tq = ROW_TILE
    nct = n_ctx // tq
    seg = lambda i: jnp.minimum(i // nct, 1) if nct > 0 else 1
    kern = functools.partial(_inproj_cd_kernel, scale_c=(HEAD_DIM + C_ROPE) ** -0.5,
                             scale_d=HEAD_DIM ** -0.5, q_lora=q_lora, kv_lora=kv_lora)
    full = lambda shape: pl.BlockSpec(shape, lambda bi, i: (0,) * len(shape))
    tab = pl.BlockSpec((tq, LANES), lambda bi, i: (i, 0))
    return pl.pallas_call(
        kern,
        grid=(b, t // tq),
        in_specs=[
            pl.BlockSpec((None, tq, d), lambda bi, i: (bi, i, 0)),
            pl.BlockSpec((None, None, 6, d), lambda bi, i: (bi, seg(i), 0, 0)),
            full((d, n_in)), full((1, q_lora)), full((1, kv_lora)),
            full(w_uq.shape), full(w_ukv.shape), tab, tab, tab, tab,
        ],
        out_specs=[
            pl.BlockSpec((None, N_HEADS, 1, tq, MLA_QK), lambda bi, i: (bi, 0, 0, i, 0)),
            pl.BlockSpec((None, N_HEADS, tq, MLA_QK), lambda bi, i: (bi, 0, i, 0)),
            pl.BlockSpec((None, N_HEADS, tq, HEAD_DIM), lambda bi, i: (bi, 0, i, 0)),
            pl.BlockSpec((None, N_KV_HEADS, rep, tq, HEAD_DIM), lambda bi, i: (bi, 0, 0, i, 0)),
            pl.BlockSpec((None, N_KV_HEADS, tq, HEAD_DIM), lambda bi, i: (bi, 0, i, 0)),
            pl.BlockSpec((None, N_KV_HEADS, tq, HEAD_DIM), lambda bi, i: (bi, 0, i, 0)),
        ],
        out_shape=[
            jax.ShapeDtypeStruct((b, N_HEADS, 1, t, MLA_QK), BF16),
            jax.ShapeDtypeStruct((b, N_HEADS, t, MLA_QK), BF16),
            jax.ShapeDtypeStruct((b, N_HEADS, t, HEAD_DIM), BF16),
            jax.ShapeDtypeStruct((b, N_KV_HEADS, rep, t, HEAD_DIM), BF16),
            jax.ShapeDtypeStruct((b, N_KV_HEADS, t, HEAD_DIM), BF16),
            jax.ShapeDtypeStruct((b, N_KV_HEADS, t, HEAD_DIM), BF16),
        ],
        compiler_params=_cparams(("parallel", "parallel")),
        name="inproj_cd",
    )(x, modtab, w_in, q_ln.reshape(1, -1), kv_ln.reshape(1, -1), w_uq, w_ukv, cos, sin, cosm, sinm)


def _prep_cd_weights(w_in, w_uq, w_ukv):
    d = w_in.shape[0]
    q_lora = w_uq.shape[0]
    kv_lora = w_ukv.shape[0]
    o = q_lora + kv_lora
    k_rope = w_in[:, o:o + C_ROPE]
    rest = w_in[:, o + C_ROPE:]
    w_in_r = jnp.concatenate([w_in[:, :o], rest, k_rope, jnp.zeros((d, LANES - C_ROPE), w_in.dtype)], axis=1)
    uq = w_uq.reshape(q_lora, N_HEADS, HEAD_DIM + C_ROPE)
    uq = jnp.pad(uq, ((0, 0), (0, 0), (0, MLA_QK - HEAD_DIM - C_ROPE))).reshape(q_lora, N_HEADS * MLA_QK)
    ukv = w_ukv.reshape(kv_lora, N_HEADS, 2, HEAD_DIM).transpose(0, 2, 1, 3).reshape(kv_lora, 2 * N_HEADS * HEAD_DIM)
    return w_in_r.astype(BF16), uq.astype(BF16), ukv.astype(BF16)


def _softmax_pv(blocks, sink_col):
    mx = None
    for s, _ in blocks:
        bm = jnp.max(s, axis=-1, keepdims=True)
        mx = bm if mx is None else jnp.maximum(mx, bm)
    if sink_col is not None:
        mx = jnp.maximum(mx, sink_col)
    den = None
    acc = None
    for s, v in blocks:
        p = jnp.exp(s - mx)
        ps = jnp.sum(p, axis=-1, keepdims=True)
        den = ps if den is None else den + ps
        pv = _dot(p.astype(BF16), v)
        acc = pv if acc is None else acc + pv
    if sink_col is not None:
        den = den + jnp.exp(sink_col - mx)
    return acc / den


def _attn_kernel(sink_ref, q_ref, k_ref, v_ref, o_ref, *, n_ctx, tq, rep, windowed, use_sink):
    g = pl.program_id(1)
    qi = pl.program_id(2)
    t_all = k_ref.shape[0]
    dk = q_ref.shape[-1]
    dv = v_ref.shape[-1]
    q = q_ref[...].reshape(rep * tq, dk)
    nct = n_ctx // tq

    if use_sink:
        row = lax.broadcasted_iota(I32, (rep * tq, 1), 0)
        sink_col = jnp.full((rep * tq, 1), sink_ref[g * rep], F32)
        for r in range(1, rep):
            sink_col = jnp.where(row >= r * tq, sink_ref[g * rep + r], sink_col)
    else:
        sink_col = None

    def emit(o):
        for r in range(rep):
            o_ref[:, r * dv:(r + 1) * dv] = o[r * tq:(r + 1) * tq].astype(o_ref.dtype)

    @pl.when(qi < nct)
    def _():
        kc = k_ref[0:n_ctx, :]
        vc = v_ref[0:n_ctx, :]
        emit(_softmax_pv([(_dot_nt(q, kc), vc)], sink_col))

    @pl.when(qi >= nct)
    def _():
        if not windowed:
            emit(_softmax_pv([(_dot_nt(q, k_ref[...]), v_ref[...])], sink_col))
        else:
            band = tq + 2 * D_WINDOW
            s0 = (qi - nct) * tq
            kstart = jnp.clip(n_ctx + s0 - D_WINDOW, n_ctx, t_all - band)
            kstart = pl.multiple_of(kstart, LANES)
            kb = k_ref[pl.ds(kstart, band), :]
            vb = v_ref[pl.ds(kstart, band), :]
            sb = _dot_nt(q, kb)
            rowq = lax.broadcasted_iota(I32, (rep * tq, band), 0) % tq + s0
            colk = lax.broadcasted_iota(I32, (rep * tq, band), 1) + (kstart - n_ctx)
            sb = jnp.where(jnp.abs(colk - rowq) <= D_WINDOW, sb, NEG_INF)
            kc = k_ref[0:n_ctx, :]
            vc = v_ref[0:n_ctx, :]
            emit(_softmax_pv([(_dot_nt(q, kc), vc), (sb, vb)], sink_col))


def attention(q, k, v, n_ctx, *, windowed=False, sink=None):
    b, g, rep, t, dk = q.shape
    dv = v.shape[-1]
    tq = ROW_TILE
    use_sink = sink is not None
    if sink is None:
        sink = jnp.zeros((g * rep,), F32)
    kern = functools.partial(_attn_kernel, n_ctx=n_ctx, tq=tq, rep=rep, windowed=windowed, use_sink=use_sink)
    return pl.pallas_call(
        kern,
        grid=(b, g, t // tq),
        in_specs=[
            pl.BlockSpec(memory_space=pltpu.SMEM),
            pl.BlockSpec((None, None, rep, tq, dk), lambda bi, gi, i: (bi, gi, 0, i, 0)),
            pl.BlockSpec((None, None, t, dk), lambda bi, gi, i: (bi, gi, 0, 0)),
            pl.BlockSpec((None, None, t, dv), lambda bi, gi, i: (bi, gi, 0, 0)),
        ],
        out_specs=pl.BlockSpec((None, tq, rep * dv), lambda bi, gi, i: (bi, i, gi)),
        out_shape=jax.ShapeDtypeStruct((b, t, g * rep * dv), BF16),
        compiler_params=_cparams(("parallel", "parallel", "arbitrary")),
        name="attention_win" if windowed else "attention",
    )(sink.astype(F32), q, k, v)


def _s5_tables(lam_re, lam_im, log_dt, b_re, b_im, c_re, c_im, d_skip):
    n_groups, n_state = lam_re.shape[1:]
    L, gs = S5_CHUNK, S5_GROUP
    lam = lax.complex(lam_re.astype(F32), lam_im.astype(F32))
    dt = jnp.exp(log_dt.astype(F32))[..., None]
    lam_dt = lam * dt
    lam_bar = jnp.exp(lam_dt)
    b_bar = ((lam_bar - 1.0) / lam)[..., None] * lax.complex(b_re.astype(F32), b_im.astype(F32))
    c_mat = lax.complex(c_re.astype(F32), c_im.astype(F32))
    pw = jnp.exp(lam_dt[None] * jnp.arange(L + 1, dtype=F32)[:, None, None, None])
    kker = jnp.einsum("dgop,tdgp,dgpi->dtgoi", c_mat, pw[:L], b_bar).real
    s_in = jnp.arange(L)[:, None]
    s_out = jnp.arange(L)[None, :]
    tau_f = s_out - s_in
    tau_r = s_in - s_out
    kf = jnp.where((tau_f >= 0)[:, :, None, None, None], kker[0][jnp.clip(tau_f, 0, L - 1)], 0.0)
    kr = jnp.where((tau_r >= 0)[:, :, None, None, None], kker[1][jnp.clip(tau_r, 0, L - 1)], 0.0)
    kt = kf + kr
    gp = S5_PACK
    n_packs = n_groups // gp
    eye = jnp.eye(gp, dtype=F32)
    t6 = kt.transpose(2, 0, 4, 1, 3).reshape(n_packs, gp, L, gs, L, gs)
    d6 = d_skip.astype(F32).reshape(n_packs, gp, 1, gs, 1, 1) * (
        jnp.eye(L, dtype=F32)[None, None, :, None, :, None] * jnp.eye(gs, dtype=F32)[None, None, None, :, None, :])
    tmat = jnp.einsum("pgsctd,gh->psgcthd", t6 + d6, eye).reshape(n_packs, L * LANES, L * LANES)
    steps = jnp.arange(L, dtype=F32)[:, None, None]
    wf = jnp.exp(lam_dt[0][None] * (L - 1 - steps))[:, :, :, None] * b_bar[0][None]
    wr = jnp.exp(lam_dt[1][None] * steps)[:, :, :, None] * b_bar[1][None]
    def w_pack(w):
        w2 = jnp.stack([w.real, w.imag], axis=0).reshape(2, L, n_packs, gp, n_state, gs)
        return jnp.einsum("bspgqc,gh->psgcbhq", w2, eye).reshape(n_packs, L * LANES, 2 * gp * n_state)
    vf = c_mat[0][None] * jnp.exp(lam_dt[0][None] * (steps + 1))[:, :, None, :]
    vr = c_mat[1][None] * jnp.exp(lam_dt[1][None] * (L - steps))[:, :, None, :]
    def v_pack(vv):
        v2 = jnp.stack([vv.real, -vv.imag], axis=0).reshape(2, L, n_packs, gp, gs, n_state)
        return jnp.einsum("bspgcq,gh->pbgqshc", v2, eye).reshape(n_packs, 2 * gp * n_state, L * LANES)
    def lam_pack(l):
        return jnp.stack([l.real, l.imag], axis=0).reshape(2, n_packs, gp * n_state).transpose(1, 0, 2)
    lam_l = pw[L]
    return (tmat.astype(BF16),
            (w_pack(wf).astype(BF16), v_pack(vf).astype(BF16), lam_pack(lam_l[0])),
            (w_pack(wr).astype(BF16), v_pack(vr).astype(BF16), lam_pack(lam_l[1])))


def _s5_pass_kernel(*refs, reverse):
    if reverse:
        u_ref, yin_ref, w_ref, v_ref, lam_ref, y_ref, s_scr, h_scr = refs
    else:
        u_ref, t_ref, w_ref, v_ref, lam_ref, y_ref, s_scr, h_scr = refs

    @pl.when(pl.program_id(1) == 0)
    def _():
        h_scr[...] = jnp.zeros_like(h_scr)

    nchs, L, b, _ = u_ref.shape
    rows = nchs * b
    half = h_scr.shape[1] // 2
    xg = jnp.concatenate([u_ref[:, s].reshape(rows, LANES) for s in range(L)], axis=-1)
    s_scr[...] = _dot(xg, w_ref[...])
    lr = jnp.broadcast_to(lam_ref[0:1, :], (b, half))
    li = jnp.broadcast_to(lam_ref[1:2, :], (b, half))

    def step(k, carry):
        hr, hi = carry
        j = nchs - 1 - k if reverse else k
        r0 = pl.multiple_of(j * b, b)
        sr = s_scr[pl.ds(r0, b), 0:half]
        si = s_scr[pl.ds(r0, b), half:]
        s_scr[pl.ds(r0, b), 0:half] = hr
        s_scr[pl.ds(r0, b), half:] = hi
        return lr * hr - li * hi + sr, lr * hi + li * hr + si

    hr, hi = lax.fori_loop(0, nchs, step, (h_scr[:, 0:half], h_scr[:, half:]))
    h_scr[:, 0:half] = hr
    h_scr[:, half:] = hi
    y = _dot(s_scr[...].astype(BF16), v_ref[...])
    if not reverse:
        y = y + _dot(xg, t_ref[...])
    for s in range(L):
        blk = y[:, s * LANES:(s + 1) * LANES].reshape(nchs, b, LANES)
        if reverse:
            blk = blk + yin_ref[:, s]
        y_ref[:, s] = blk


def s5_apply(u, tables, n_ctx):
    tmat, fwd, rev = tables
    b, t, ch = u.shape
    L = S5_CHUNK
    n_packs = ch // LANES
    seg = ROW_TILE
    nseg, nct, nchs = t // seg, n_ctx // seg, seg // L
    width = L * LANES
    n_state2 = fwd[0].shape[-1]
    ut = u.transpose(1, 0, 2).reshape(t // L, L, b, ch)
    blk = pl.BlockSpec((nchs, L, b, LANES), lambda p, i: (i, 0, 0, p))
    rseg = lambda i: jnp.where(i < nct, nct - 1 - i, nseg - 1 - (i - nct))
    rblk = pl.BlockSpec((nchs, L, b, LANES), lambda p, i: (rseg(i), 0, 0, p))
    tab = lambda shape: pl.BlockSpec((None,) + shape, lambda p, i: (p, 0, 0))
    scratch = [pltpu.VMEM((nchs * b, n_state2), F32), pltpu.VMEM((b, n_state2), F32)]
    y_shape = jax.ShapeDtypeStruct((t // L, L, b, ch), F32)
    y_f = pl.pallas_call(
        functools.partial(_s5_pass_kernel, reverse=False),
        grid=(n_packs, nseg),
        in_specs=[blk, tab((width, width)), tab((width, n_state2)), tab((n_state2, width)), tab((2, n_state2 // 2))],
        out_specs=blk,
        out_shape=y_shape,
        scratch_shapes=scratch,
        compiler_params=_cparams(("parallel", "arbitrary")),
        name="s5_forward",
    )(ut, tmat, *fwd)
    y = pl.pallas_call(
        functools.partial(_s5_pass_kernel, reverse=True),
        grid=(n_packs, nseg),
        in_specs=[rblk, rblk, tab((width, n_state2)), tab((n_state2, width)), tab((2, n_state2 // 2))],
        out_specs=rblk,
        out_shape=y_shape,
        scratch_shapes=scratch,
        input_output_aliases={1: 0},
        compiler_params=_cparams(("parallel", "arbitrary")),
        name="s5_reverse",
    )(ut, y_f, *rev)
    return y.reshape(t, b, ch).transpose(1, 0, 2)


def _gelu_tanh(x):
    return 0.5 * x * (1.0 + jnp.tanh(math.sqrt(2.0 / math.pi) * (x + 0.044715 * (x * x * x))))


def _post_kernel(*refs, alpha, glu, n_exp):
    if glu:
        (x_ref, a_ref, s_ref, wg_ref, bg_ref, wm_ref, mod_ref, g_ref, b_ref, wrh_ref, wrl_ref, br_ref,
         x1_ref, h2_ref, idx_ref, gate_ref, rank_ref, cnt_ref, cnt_scr) = refs
    else:
        (x_ref, a_ref, s_ref, wm_ref, mod_ref, g_ref, b_ref, wrh_ref, wrl_ref, br_ref,
         x1_ref, h2_ref, idx_ref, gate_ref, rank_ref, cnt_ref, cnt_scr) = refs
    first = (pl.program_id(0) == 0) & (pl.program_id(1) == 0)

    @pl.when(first)
    def _():
        cnt_scr[...] = jnp.zeros_like(cnt_scr)

    tq, d = x_ref.shape
    half = a_ref.shape[1]
    if glu:
        z = _gelu_tanh(s_ref[...])
        gate = jax.nn.sigmoid(_dot(z.astype(BF16), wg_ref[...]) + bg_ref[...])
        second = (z * gate).astype(BF16)
    else:
        second = s_ref[...]
    mix = _dot(a_ref[...], wm_ref[0:half, :]) + _dot(second, wm_ref[half:, :])
    x1 = _layer_norm(alpha * x_ref[...] + mod_ref[2:3, :] * mix, g_ref[...], b_ref[...])
    x1_ref[...] = x1
    h2 = x1 * (1.0 + mod_ref[4:5, :]) + mod_ref[3:4, :]
    h2_ref[...] = h2

    h_hi, h_lo = _split_bf16(h2)
    logits = _dot_nt(wrh_ref[...], h_hi) + _dot_nt(wrh_ref[...], h_lo) + _dot_nt(wrl_ref[...], h_hi) + br_ref[...]
    eidx = lax.broadcasted_iota(I32, (n_exp, tq), 0)
    work = logits
    tops, sels = [], []
    for k in range(TOP_K):
        m = jnp.max(work, axis=0, keepdims=True)
        ik = jnp.min(jnp.where(work == m, eidx, n_exp), axis=0, keepdims=True)
        sel = eidx == ik
        work = jnp.where(sel, -jnp.inf, work)
        tops.append(m)
        sels.append(sel)
        idx_ref[k:k + 1, :] = ik
    exps = [jnp.exp(tk - tops[0]) for tk in tops]
    den = exps[0] + exps[1] + exps[2] + exps[3]
    for k in range(TOP_K):
        gate_ref[k:k + 1, :] = exps[k] / den
    onehot = jnp.zeros((n_exp, tq), F32)
    for sel in sels:
        onehot = onehot + sel.astype(F32)
    tri = (lax.broadcasted_iota(I32, (tq, tq), 0) < lax.broadcasted_iota(I32, (tq, tq), 1)).astype(BF16)
    before = _dot(onehot.astype(BF16), tri) + cnt_scr[...]
    for k in range(TOP_K):
        rk = jnp.sum(jnp.where(sels[k], before, 0.0), axis=0, keepdims=True)
        rank_ref[k:k + 1, :] = rk.astype(I32)
    cnt_scr[...] += jnp.sum(onehot, axis=1, keepdims=True)
    cnt_ref[...] = jnp.broadcast_to(cnt_scr[...], cnt_ref.shape).astype(I32)


def post_mixer(x, a, s, modtab, w_mix, ln_g, ln_b, w_router, b_router, n_ctx, alpha, glu_w=None, glu_b=None):
    b, t, d = x.shape
    half = a.shape[-1]
    n_exp = w_router.shape[1]
    tq = ROW_TILE
    nct = n_ctx // tq
    seg = lambda i: jnp.minimum(i // nct, 1) if nct > 0 else 1
    glu = glu_w is not None
    wr_hi, wr_lo = _split_bf16(w_router.T.astype(F32))
    full = lambda shape: pl.BlockSpec(shape, lambda bi, i: (0,) * len(shape))
    tok = lambda width: pl.BlockSpec((None, tq, width), lambda bi, i: (bi, i, 0))
    in_specs = [tok(d), tok(half), tok(half)]
    args = [x, a, s]
    if glu:
        in_specs += [full((half, half)), full((1, half))]
        args += [glu_w.astype(BF16), glu_b.reshape(1, half)]
    in_specs += [full((d, d)), pl.BlockSpec((None, None, 6, d), lambda bi, i: (bi, seg(i), 0, 0)),
                 full((1, d)), full((1, d)), full((n_exp, d)), full((n_exp, d)), full((n_exp, 1))]
    args += [w_mix.astype(BF16), modtab, ln_g.reshape(1, d), ln_b.reshape(1, d), wr_hi, wr_lo,
             b_router.reshape(n_exp, 1)]
    nt = t // tq
    lane_out = lambda dt: (pl.BlockSpec((TOP_K, tq), lambda bi, i: (0, bi * nt + i)),
                           jax.ShapeDtypeStruct((TOP_K, b * t), dt))
    outs = [
        (tok(d), jax.ShapeDtypeStruct((b, t, d), F32)),
        (tok(d), jax.ShapeDtypeStruct((b, t, d), F32)),
        lane_out(I32), lane_out(F32), lane_out(I32),
        (pl.BlockSpec((n_exp, LANES), lambda bi, i: (0, 0)), jax.ShapeDtypeStruct((n_exp, LANES), I32)),
    ]
    kern = functools.partial(_post_kernel, alpha=alpha, glu=glu, n_exp=n_exp)
    return pl.pallas_call(
        kern,
        grid=(b, nt),
        in_specs=in_specs,
        out_specs=[o[0] for o in outs],
        out_shape=[o[1] for o in outs],
        scratch_shapes=[pltpu.VMEM((n_exp, 1), F32)],
        compiler_params=_cparams(("arbitrary", "arbitrary")),
        name="post_mixer_glu" if glu else "post_mixer",
    )(*args)


def _route_kernel(start_ref, idx_ref, rank_ref, pos_ref, *, n_exp, tq):
    eidx = lax.broadcasted_iota(I32, (n_exp, tq), 0)
    for k in range(TOP_K):
        sel = eidx == idx_ref[k:k + 1, :]
        base = jnp.sum(jnp.where(sel, start_ref[...], 0.0), axis=0, keepdims=True)
        pos_ref[:, k * tq:(k + 1) * tq] = base.astype(I32) + rank_ref[k:k + 1, :]


def route_positions(group_start, idx, rank):
    n_exp = group_start.shape[0]
    n_tok = idx.shape[1]
    tq = ROW_TILE
    return pl.pallas_call(
        functools.partial(_route_kernel, n_exp=n_exp, tq=tq),
        grid=(n_tok // tq,),
        in_specs=[
            pl.BlockSpec((n_exp, 1), lambda i: (0, 0)),
            pl.BlockSpec((TOP_K, tq), lambda i: (0, i)),
            pl.BlockSpec((TOP_K, tq), lambda i: (0, i)),
        ],
        out_specs=pl.BlockSpec((None, 1, TOP_K * tq), lambda i: (i, 0, 0)),
        out_shape=jax.ShapeDtypeStruct((n_tok // tq, 1, TOP_K * tq), I32),
        compiler_params=_cparams(("parallel",)),
        name="route_positions",
    )(group_start.astype(F32).reshape(n_exp, 1), idx, rank)


def _dispatch_kernel(pos_hbm, h_hbm, xs_hbm, idx_smem, idx_sem, row_sem, *, tq, n_tiles):
    i = pl.program_id(0)
    slot = i % 2
    n_rows = TOP_K * tq

    def idx_copy(tile, sl):
        return pltpu.make_async_copy(pos_hbm.at[tile], idx_smem.at[sl], idx_sem.at[sl])

    def rows_wait(sl):
        pltpu.make_async_copy(h_hbm.at[pl.ds(0, n_rows), :], xs_hbm.at[pl.ds(0, n_rows), :], row_sem.at[sl]).wait()

    @pl.when(i == 0)
    def _():
        idx_copy(0, 0).start()

    idx_copy(i, slot).wait()

    @pl.when(i + 1 < n_tiles)
    def _():
        idx_copy(i + 1, 1 - slot).start()

    base = i * tq
    for k in range(TOP_K):
        for r in range(tq):
            dst = idx_smem[slot, 0, k * tq + r]
            pltpu.make_async_copy(h_hbm.at[pl.ds(base + r, 1), :], xs_hbm.at[pl.ds(dst, 1), :],
                                  row_sem.at[slot]).start()

    @pl.when(i > 0)
    def _():
        rows_wait(1 - slot)

    @pl.when(i == n_tiles - 1)
    def _():
        rows_wait(slot)


def moe_dispatch(pos_tiles, h):
    n_tok, d = h.shape
    n_tiles = pos_tiles.shape[0]
    tq = n_tok // n_tiles
    return pl.pallas_call(
        functools.partial(_dispatch_kernel, tq=tq, n_tiles=n_tiles),
        grid=(n_tiles,),
        in_specs=[pl.BlockSpec(memory_space=pl.ANY), pl.BlockSpec(memory_space=pl.ANY)],
        out_specs=pl.BlockSpec(memory_space=pl.ANY),
        out_shape=jax.ShapeDtypeStruct((TOP_K * n_tok, d), F32),
        scratch_shapes=[
            pltpu.SMEM((2, 1, TOP_K * tq), I32),
            pltpu.SemaphoreType.DMA((2,)),
            pltpu.SemaphoreType.DMA((2,)),
        ],
        compiler_params=_cparams(("arbitrary",)),
        name="moe_dispatch",
    )(pos_tiles, h)


def _moe_kernel(e_ref, j_ref, lo_ref, hi_ref, first_ref, new_ref, x_ref, wi_ref, bi_ref, wo_ref, bo_ref,
                y_ref, wi_b, wo_b):
    i = pl.program_id(0)
    lo = lo_ref[i]
    hi = hi_ref[i]

    @pl.when(new_ref[i] == 1)
    def _():
        wi_b[...] = wi_ref[...].astype(BF16)
        wo_b[...] = wo_ref[...].astype(BF16)

    @pl.when(hi > lo)
    def _():
        x = x_ref[...].astype(BF16)
        z = _dot(x, wi_b[...]) + bi_ref[...]
        f = z.shape[1] // 2
        glu = jnp.minimum(z[:, :f], SWIGLU_LIMIT)
        lin = jnp.clip(z[:, f:], -SWIGLU_LIMIT, SWIGLU_LIMIT)
        act = glu * jax.nn.sigmoid(SWIGLU_ALPHA * glu) * (lin + 1.0)
        y = _dot(act.astype(BF16), wo_b[...]) + bo_ref[...]
        row = lax.broadcasted_iota(I32, (x_ref.shape[0], 1), 0)
        mine = (row >= lo) & (row < hi)

        @pl.when(first_ref[i] == 1)
        def _():
            y_ref[...] = jnp.where(mine, y, 0.0)

        @pl.when(first_ref[i] == 0)
        def _():
            y_ref[...] = jnp.where(mine, y, y_ref[...])


def moe_experts(xs, items, w_in, b_in, w_out, b_out, layer):
    n_slots, d = xs.shape
    _, n_exp, _, f2 = w_in.shape
    n_items = items[0].shape[0]
    tm = MOE_BLOCK
    wmap = lambda i, e, j, lo, hi, fi, nw: (layer, e[i], 0, 0)
    xmap = lambda i, e, j, lo, hi, fi, nw: (j[i], 0)
    grid_spec = pltpu.PrefetchScalarGridSpec(
        num_scalar_prefetch=6,
        grid=(n_items,),
        in_specs=[
            pl.BlockSpec((tm, d), xmap),
            pl.BlockSpec((None, None, d, f2), wmap),
            pl.BlockSpec((None, None, 1, f2), wmap),
            pl.BlockSpec((None, None, f2 // 2, d), wmap),
            pl.BlockSpec((None, None, 1, d), wmap),
        ],
        out_specs=pl.BlockSpec((tm, d), xmap),
        scratch_shapes=[pltpu.VMEM((d, f2), BF16), pltpu.VMEM((f2 // 2, d), BF16)],
    )
    return pl.pallas_call(
        _moe_kernel,
        grid_spec=grid_spec,
        out_shape=jax.ShapeDtypeStruct((n_slots, d), F32),
        compiler_params=_cparams(("arbitrary",)),
        name="moe_experts",
    )(*items, xs, w_in, b_in.reshape(b_in.shape[0], n_exp, 1, f2), w_out, b_out.reshape(b_out.shape[0], n_exp, 1, d))


def _combine_kernel(pos_hbm, x_ref, g_ref, mod_ref, lg_ref, lb_ref, ys_hbm, o_ref,
                    idx_smem, ybuf, idx_sem, row_sem, *, alpha, tq, n_tiles):
    i = pl.program_id(0)
    slot = i % 2
    d = x_ref.shape[1]

    def idx_copy(tile, sl):
        return pltpu.make_async_copy(pos_hbm.at[tile], idx_smem.at[sl], idx_sem.at[sl])

    def gather_start(sl):
        for r in range(TOP_K * tq):
            pltpu.make_async_copy(ys_hbm.at[pl.ds(idx_smem[sl, 0, r], 1), :], ybuf.at[sl, pl.ds(r, 1), :],
                                  row_sem.at[sl]).start()

    def rows_wait(sl):
        pltpu.make_async_copy(ys_hbm.at[pl.ds(0, TOP_K * tq), :], ybuf.at[sl], row_sem.at[sl]).wait()

    @pl.when(i == 0)
    def _():
        idx_copy(0, 0).start()
        idx_copy(0, 0).wait()
        gather_start(0)

        @pl.when(n_tiles > 1)
        def _():
            idx_copy(1, 1).start()

    @pl.when(i + 1 < n_tiles)
    def _():
        idx_copy(i + 1, 1 - slot).wait()
        gather_start(1 - slot)

    rows_wait(slot)

    @pl.when(i + 2 < n_tiles)
    def _():
        idx_copy(i + 2, slot).start()

    yb = ybuf.at[slot]
    gates = g_ref[...]
    y = None
    for k in range(TOP_K):
        term = gates[:, k:k + 1] * yb[k * tq:(k + 1) * tq, :]
        y = term if y is None else y + term
    o_ref[...] = _layer_norm(alpha * x_ref[...] + mod_ref[5:6, :] * y, lg_ref[...], lb_ref[...])


def moe_combine(x1, pos_tiles, gates_tok, modtab, ln_g, ln_b, ys, n_ctx, alpha):
    b, t, d = x1.shape
    tq = ROW_TILE
    nt = t // tq
    n_tiles = b * nt
    nct = n_ctx // tq
    seg = lambda i: jnp.minimum((i % nt) // nct, 1) if nct > 0 else 1
    kern = functools.partial(_combine_kernel, alpha=alpha, tq=tq, n_tiles=n_tiles)
    out = pl.pallas_call(
        kern,
        grid=(n_tiles,),
        in_specs=[
            pl.BlockSpec(memory_space=pl.ANY),
            pl.BlockSpec((tq, d), lambda i: (i, 0)),
            pl.BlockSpec((tq, TOP_K), lambda i: (i, 0)),
            pl.BlockSpec((None, None, 6, d), lambda i: (i // nt, seg(i), 0, 0)),
            pl.BlockSpec((1, d), lambda i: (0, 0)),
            pl.BlockSpec((1, d), lambda i: (0, 0)),
            pl.BlockSpec(memory_space=pl.ANY),
        ],
        out_specs=pl.BlockSpec((tq, d), lambda i: (i, 0)),
        out_shape=jax.ShapeDtypeStruct((b * t, d), F32),
        scratch_shapes=[
            pltpu.SMEM((2, 1, TOP_K * tq), I32),
            pltpu.VMEM((2, TOP_K * tq, d), F32),
            pltpu.SemaphoreType.DMA((2,)),
            pltpu.SemaphoreType.DMA((2,)),
        ],
        compiler_params=_cparams(("arbitrary",)),
        name="moe_combine",
    )(pos_tiles, x1.reshape(b * t, d), gates_tok, modtab, ln_g.reshape(1, d), ln_b.reshape(1, d), ys)
    return out.reshape(b, t, d)


def _group_items(counts, n_slots, tm):
    n_exp = counts.shape[0]
    le = (jnp.arange(n_exp)[:, None] <= jnp.arange(n_exp)[None, :]).astype(I32)
    g_end = counts @ le
    g_start = g_end - counts
    first_blk = g_start // tm
    n_blk = jnp.where(counts > 0, (g_end - 1) // tm - first_blk + 1, 0)
    i_end = n_blk @ le
    i_start = i_end - n_blk
    n_items = n_slots // tm + n_exp - 1
    it = jnp.arange(n_items, dtype=I32)
    valid = it < i_end[-1]
    e = jnp.minimum(jnp.sum((i_end[None, :] <= it[:, None]).astype(I32), axis=1), n_exp - 1)
    onehot = (e[:, None] == jnp.arange(n_exp)[None, :]).astype(I32)
    pick = lambda tbl: onehot @ tbl
    blk = pick(first_blk) + it - pick(i_start)
    lo = jnp.clip(pick(g_start) - blk * tm, 0, tm)
    hi = jnp.clip(pick(g_end) - blk * tm, 0, tm)
    last = jnp.maximum(i_end[-1] - 1, 0)
    e_last = jnp.sum(jnp.where(it == last, e, 0))
    e = jnp.where(valid, e, e_last)
    blk = jnp.where(valid, blk, n_slots // tm - 1)
    lo = jnp.where(valid, lo, 0)
    hi = jnp.where(valid, hi, 0)
    prev = lambda a: jnp.concatenate([jnp.full((1,), -1, I32), a[:-1]])
    first = (blk != prev(blk)).astype(I32)
    new_e = (e != prev(e)).astype(I32)
    return g_start, tuple(a.astype(I32) for a in (e, blk, lo, hi, first, new_e))


def kernel(x, c, ctx, c_ctx, w_mod, b_mod, ln_g, ln_b, w_mix_out, w_router, b_router, w_exp_in, b_exp_in, w_exp_out, b_exp_out, w_in_ab, a_q_norm, a_k_norm, s5_lam_re, s5_lam_im, s5_log_dt, s5_b_re, s5_b_im, s5_c_re, s5_c_im, s5_d, w_glu, b_glu, w_in_cd, c_q_norm, c_kv_norm, w_uq, w_ukv, d_sink):
    batch, n_lat, d = x.shape
    n_ctx = ctx.shape[1]
    depth = w_mod.shape[0]
    alpha = (2.0 * depth) ** 0.25
    t = n_ctx + n_lat
    n_tok = batch * t

    xs = jnp.concatenate([ctx, x], axis=1)
    n_rows = -(-(batch + 1) // SUBLANES) * SUBLANES
    cvec = jnp.zeros((n_rows, d), F32).at[:batch].set(c).at[batch].set(c_ctx)
    mods = modulation(cvec, w_mod, b_mod).reshape(depth, n_rows, 6, d)
    modtab = jnp.stack([jnp.broadcast_to(mods[:, batch:batch + 1], (depth, batch, 6, d)), mods[:, :batch]],
                       axis=2)

    cos_h, sin_h = _rope_tables(n_ctx, n_lat, HEAD_DIM)
    cos_m, sin_m = _rope_tables(n_ctx, n_lat, C_ROPE)

    for layer in range(depth):
        i = layer // 2
        mt = modtab[layer]
        if layer % 2 == 0:
            q, k, v, u = inproj_ab(xs, mt, w_in_ab[i].astype(BF16), a_q_norm[i], a_k_norm[i], cos_h, sin_h, n_ctx)
            att = attention(q, k, v, n_ctx)
            tables = _s5_tables(s5_lam_re[i], s5_lam_im[i], s5_log_dt[i], s5_b_re[i], s5_b_im[i],
                                s5_c_re[i], s5_c_im[i], s5_d[i])
            second = s5_apply(u, tables, n_ctx)
            glu_w, glu_b = w_glu[i], b_glu[i]
        else:
            w_in_r, uq_r, ukv_r = _prep_cd_weights(w_in_cd[i], w_uq[i], w_ukv[i])
            qm, km, vm, qd, kd, vd = inproj_cd(xs, mt, w_in_r, c_q_norm[i], c_kv_norm[i], uq_r, ukv_r,
                                               cos_h, sin_h, cos_m, sin_m, n_ctx)
            att = attention(qm, km, vm, n_ctx)
            second = attention(qd, kd, vd, n_ctx, windowed=True, sink=d_sink[i])
            glu_w = glu_b = None
        x1, h2, idx, gates, rank, counts = post_mixer(
            xs, att, second, mt, w_mix_out[layer], ln_g[layer, 0], ln_b[layer, 0],
            w_router[layer], b_router[layer], n_ctx, alpha, glu_w, glu_b)
        g_start, items = _group_items(counts[:, 0], TOP_K * n_tok, MOE_BLOCK)
        pos_tiles = route_positions(g_start, idx, rank)
        sorted_rows = moe_dispatch(pos_tiles, h2.reshape(n_tok, d))
        ys = moe_experts(sorted_rows, items, w_exp_in, b_exp_in, w_exp_out, b_exp_out, layer)
        xs = moe_combine(x1, pos_tiles, gates.T, mt, ln_g[layer, 1], ln_b[layer, 1], ys, n_ctx, alpha)
    return xs[:, n_ctx:, :]
```

```python
import functools
import math

import jax
import jax.numpy as jnp
from jax import lax
from jax.experimental import pallas as pl
from jax.experimental.pallas import tpu as pltpu

F32 = jnp.float32
BF16 = jnp.bfloat16
I32 = jnp.int32

GRID_W = 64
ROPE_THETA = 10000.0
HEAD_DIM = 128
N_HEADS = 4
N_KV_HEADS = 2
C_ROPE = 64
MLA_QK = 256
D_WINDOW = 128
S5_GROUP = 16
S5_STATE = 64
S5_CHUNK = 8
S5_PACK = 8
TOP_K = 4
SWIGLU_LIMIT = 7.0
SWIGLU_ALPHA = 1.702
NEG_INF = -1e30
LN_EPS = 1e-5
RMS_EPS = 1e-6

LANES = 128
SUBLANES = 8
VMEM_LIMIT_BYTES = 56 * 1024 * 1024

ROW_TILE = 256
MOE_BLOCK = 256


def _cparams(sem):
    return pltpu.CompilerParams(dimension_semantics=sem, vmem_limit_bytes=VMEM_LIMIT_BYTES)


def _split_bf16(a):
    hi = a.astype(BF16)
    lo = (a - hi.astype(F32)).astype(BF16)
    return hi, lo


def _dot(a, b):
    return jnp.dot(a, b, preferred_element_type=F32)


def _dot_nt(a, b):
    return lax.dot_general(a, b, (((1,), (1,)), ((), ())), preferred_element_type=F32)


def _store_token_tiles(ref, val):
    rows, d = val.shape
    n = d // LANES
    for s in range(n):
        ref[pl.ds(s, rows, stride=n), :] = val[:, s * LANES:(s + 1) * LANES]


def _load_token_tiles(ref, first_row, rows, n):
    return jnp.concatenate([ref[pl.ds(first_row + s, rows, stride=n), :] for s in range(n)], axis=-1)


def _layer_norm(x, g, b):
    mu = jnp.mean(x, axis=-1, keepdims=True)
    xc = x - mu
    var = jnp.mean(xc * xc, axis=-1, keepdims=True)
    return xc * lax.rsqrt(var + LN_EPS) * g + b


def _rms(x, g):
    return x * lax.rsqrt(jnp.mean(x * x, axis=-1, keepdims=True) + RMS_EPS) * g


def _mod_kernel(c_ref, w_ref, b_ref, o_ref):
    c = c_ref[...]
    s = c * jax.nn.sigmoid(c)
    s_hi, s_lo = _split_bf16(s)
    w_hi, w_lo = _split_bf16(w_ref[...])
    o_ref[...] = _dot(s_hi, w_hi) + _dot(s_lo, w_hi) + _dot(s_hi, w_lo) + b_ref[...]


def modulation(cvec, w_mod, b_mod):
    n_layers, d, d6 = w_mod.shape
    r = cvec.shape[0]
    tn = 1536
    return pl.pallas_call(
        _mod_kernel,
        grid=(n_layers, d6 // tn),
        in_specs=[
            pl.BlockSpec((r, d), lambda l, j: (0, 0)),
            pl.BlockSpec((None, d, tn), lambda l, j: (l, 0, j)),
            pl.BlockSpec((None, 1, tn), lambda l, j: (l, 0, j)),
        ],
        out_specs=pl.BlockSpec((None, r, tn), lambda l, j: (l, 0, j)),
        out_shape=jax.ShapeDtypeStruct((n_layers, r, d6), F32),
        compiler_params=_cparams(("arbitrary", "arbitrary")),
        name="modulation",
    )(cvec, w_mod, b_mod.reshape(n_layers, 1, d6))


def _rope_lanes(t, cos, sin, quarter):
    lane = lax.broadcasted_iota(I32, t.shape, 1)
    first = (lane % (2 * quarter)) < quarter
    partner = jnp.where(first, pltpu.roll(t, LANES - quarter, 1), pltpu.roll(t, quarter, 1))
    return t * cos + partner * sin


def _rope_tables(n_ctx, n_lat, rot_dim):
    quarter = rot_dim // 4
    pos = jnp.arange(n_lat, dtype=F32)
    row = jnp.floor(pos / GRID_W)
    col = pos - row * GRID_W
    inv_freq = ROPE_THETA ** (-jnp.arange(quarter, dtype=F32) / quarter)
    ang_r = row[:, None] * inv_freq
    ang_c = col[:, None] * inv_freq
    cos = jnp.concatenate([jnp.cos(ang_r)] * 2 + [jnp.cos(ang_c)] * 2, axis=-1)
    sin = jnp.concatenate([-jnp.sin(ang_r), jnp.sin(ang_r), -jnp.sin(ang_c), jnp.sin(ang_c)], axis=-1)
    pad = LANES - rot_dim
    cos = jnp.pad(cos, ((n_ctx, 0), (0, pad)), constant_values=1.0)
    sin = jnp.pad(sin, ((n_ctx, 0), (0, pad)))
    return cos, sin


def _inproj_ab_kernel(x_ref, mod_ref, w_ref, qn_ref, kn_ref, cos_ref, sin_ref,
                      q_ref, k_ref, v_ref, u_ref, *, scale):
    x = x_ref[...]
    h = (x * (1.0 + mod_ref[1:2, :]) + mod_ref[0:1, :]).astype(BF16)
    y = _dot(h, w_ref[...])
    cos = cos_ref[...]
    sin = sin_ref[...]
    rep = N_HEADS // N_KV_HEADS
    for hh in range(N_HEADS):
        qh = _rms(y[:, hh * HEAD_DIM:(hh + 1) * HEAD_DIM], qn_ref[...])
        qh = _rope_lanes(qh, cos, sin, HEAD_DIM // 4) * scale
        q_ref[hh // rep, hh % rep] = qh.astype(BF16)
    k0 = N_HEADS * HEAD_DIM
    v0 = k0 + N_KV_HEADS * HEAD_DIM
    for g in range(N_KV_HEADS):
        kh = _rms(y[:, k0 + g * HEAD_DIM:k0 + (g + 1) * HEAD_DIM], kn_ref[...])
        k_ref[g] = _rope_lanes(kh, cos, sin, HEAD_DIM // 4).astype(BF16)
        v_ref[g] = y[:, v0 + g * HEAD_DIM:v0 + (g + 1) * HEAD_DIM].astype(BF16)
    u_ref[...] = y[:, v0 + N_KV_HEADS * HEAD_DIM:].astype(BF16)


def inproj_ab(x, modtab, w_in, q_norm, k_norm, cos, sin, n_ctx):
    b, t, d = x.shape
    n_in = w_in.shape[1]
    s5_ch = n_in - (N_HEADS + 2 * N_KV_HEADS) * HEAD_DIM
    rep = N_HEADS // N_KV_HEADS
    tq = ROW_TILE
    nct = n_ctx // tq
    seg = lambda i: jnp.minimum(i // nct, 1) if nct > 0 else 1
    kern = functools.partial(_inproj_ab_kernel, scale=HEAD_DIM ** -0.5)
    return pl.pallas_call(
        kern,
        grid=(b, t // tq),
        in_specs=[
            pl.BlockSpec((None, tq, d), lambda bi, i: (bi, i, 0)),
            pl.BlockSpec((None, None, 6, d), lambda bi, i: (bi, seg(i), 0, 0)),
            pl.BlockSpec((d, n_in), lambda bi, i: (0, 0)),
            pl.BlockSpec((1, HEAD_DIM), lambda bi, i: (0, 0)),
            pl.BlockSpec((1, HEAD_DIM), lambda bi, i: (0, 0)),
            pl.BlockSpec((tq, LANES), lambda bi, i: (i, 0)),
            pl.BlockSpec((tq, LANES), lambda bi, i: (i, 0)),
        ],
        out_specs=[
            pl.BlockSpec((None, N_KV_HEADS, rep, tq, HEAD_DIM), lambda bi, i: (bi, 0, 0, i, 0)),
            pl.BlockSpec((None, N_KV_HEADS, tq, HEAD_DIM), lambda bi, i: (bi, 0, i, 0)),
            pl.BlockSpec((None, N_KV_HEADS, tq, HEAD_DIM), lambda bi, i: (bi, 0, i, 0)),
            pl.BlockSpec((None, tq, s5_ch), lambda bi, i: (bi, i, 0)),
        ],
        out_shape=[
            jax.ShapeDtypeStruct((b, N_KV_HEADS, rep, t, HEAD_DIM), BF16),
            jax.ShapeDtypeStruct((b, N_KV_HEADS, t, HEAD_DIM), BF16),
            jax.ShapeDtypeStruct((b, N_KV_HEADS, t, HEAD_DIM), BF16),
            jax.ShapeDtypeStruct((b, t, s5_ch), BF16),
        ],
        compiler_params=_cparams(("parallel", "parallel")),
        name="inproj_ab",
    )(x, modtab, w_in, q_norm.reshape(1, -1), k_norm.reshape(1, -1), cos, sin)


def _inproj_cd_kernel(x_ref, mod_ref, w_ref, qln_ref, kvln_ref, wuq_ref, wukv_ref,
                      cos_ref, sin_ref, cosm_ref, sinm_ref,
                      qm_ref, km_ref, vm_ref, qd_ref, kd_ref, vd_ref, *, scale_c, scale_d, q_lora, kv_lora):
    x = x_ref[...]
    h = (x * (1.0 + mod_ref[1:2, :]) + mod_ref[0:1, :]).astype(BF16)
    y = _dot(h, w_ref[...])
    cos = cos_ref[...]
    sin = sin_ref[...]
    cosm = cosm_ref[...]
    sinm = sinm_ref[...]
    rep = N_HEADS // N_KV_HEADS
    cq = _rms(y[:, :q_lora], qln_ref[...]).astype(BF16)
    ckv = _rms(y[:, q_lora:q_lora + kv_lora], kvln_ref[...]).astype(BF16)
    q = _dot(cq, wuq_ref[...])
    kv = _dot(ckv, wukv_ref[...])
    o = q_lora + kv_lora
    qd0, kd0 = o, o + N_HEADS * HEAD_DIM
    vd0 = kd0 + N_KV_HEADS * HEAD_DIM
    kr0 = vd0 + N_KV_HEADS * HEAD_DIM
    k_rope = _rope_lanes(y[:, kr0:kr0 + LANES], cosm, sinm, C_ROPE // 4)
    for hh in range(N_HEADS):
        qn = q[:, hh * MLA_QK:hh * MLA_QK + HEAD_DIM]
        qr = _rope_lanes(q[:, hh * MLA_QK + HEAD_DIM:(hh + 1) * MLA_QK], cosm, sinm, C_ROPE // 4)
        qm_ref[hh, 0] = (jnp.concatenate([qn, qr], axis=-1) * scale_c).astype(BF16)
        kn = kv[:, hh * HEAD_DIM:(hh + 1) * HEAD_DIM]
        km_ref[hh] = jnp.concatenate([kn, k_rope], axis=-1).astype(BF16)
        vm_ref[hh] = kv[:, (N_HEADS + hh) * HEAD_DIM:(N_HEADS + hh + 1) * HEAD_DIM].astype(BF16)
        qdh = _rope_lanes(y[:, qd0 + hh * HEAD_DIM:qd0 + (hh + 1) * HEAD_DIM], cos, sin, HEAD_DIM // 4)
        qd_ref[hh // rep, hh % rep] = (qdh * scale_d).astype(BF16)
    for g in range(N_KV_HEADS):
        kdh = _rope_lanes(y[:, kd0 + g * HEAD_DIM:kd0 + (g + 1) * HEAD_DIM], cos, sin, HEAD_DIM // 4)
        kd_ref[g] = kdh.astype(BF16)
        vd_ref[g] = y[:, vd0 + g * HEAD_DIM:vd0 + (g + 1) * HEAD_DIM].astype(BF16)


def inproj_cd(x, modtab, w_in, q_ln, kv_ln, w_uq, w_ukv, cos, sin, cosm, sinm, n_ctx):
    b, t, d = x.shape
    n_in = w_in.shape[1]
    q_lora, kv_lora = q_ln.shape[0], kv_ln.shape[0]
    rep = N_HEADS // N_KV_HEADS
    tq = ROW_TILE
    nct = n_ctx // tq
    seg = lambda i: jnp.minimum(i // nct, 1) if nct > 0 else 1
    kern = functools.partial(_inproj_cd_kernel, scale_c=(HEAD_DIM + C_ROPE) ** -0.5,
                             scale_d=HEAD_DIM ** -0.5, q_lora=q_lora, kv_lora=kv_lora)
    full = lambda shape: pl.BlockSpec(shape, lambda bi, i: (0,) * len(shape))
    tab = pl.BlockSpec((tq, LANES), lambda bi, i: (i, 0))
    return pl.pallas_call(
        kern,
        grid=(b, t // tq),
        in_specs=[
            pl.BlockSpec((None, tq, d), lambda bi, i: (bi, i, 0)),
            pl.BlockSpec((None, None, 6, d), lambda bi, i: (bi, seg(i), 0, 0)),
            full((d, n_in)), full((1, q_lora)), full((1, kv_lora)),
            full(w_uq.shape), full(w_ukv.shape), tab, tab, tab, tab,
        ],
        out_specs=[
            pl.BlockSpec((None, N_HEADS, 1, tq, MLA_QK), lambda bi, i: (bi, 0, 0, i, 0)),
            pl.BlockSpec((None, N_HEADS, tq, MLA_QK), lambda bi, i: (bi, 0, i, 0)),
            pl.BlockSpec((None, N_HEADS, tq, HEAD_DIM), lambda bi, i: (bi, 0, i, 0)),
            pl.BlockSpec((None, N_KV_HEADS, rep, tq, HEAD_DIM), lambda bi, i: (bi, 0, 0, i, 0)),
            pl.BlockSpec((None, N_KV_HEADS, tq, HEAD_DIM), lambda bi, i: (bi, 0, i, 0)),
            pl.BlockSpec((None, N_KV_HEADS, tq, HEAD_DIM), lambda bi, i: (bi, 0, i, 0)),
        ],
        out_shape=[
            jax.ShapeDtypeStruct((b, N_HEADS, 1, t, MLA_QK), BF16),
            jax.ShapeDtypeStruct((b, N_HEADS, t, MLA_QK), BF16),
            jax.ShapeDtypeStruct((b, N_HEADS, t, HEAD_DIM), BF16),
            jax.ShapeDtypeStruct((b, N_KV_HEADS, rep, t, HEAD_DIM), BF16),
            jax.ShapeDtypeStruct((b, N_KV_HEADS, t, HEAD_DIM), BF16),
            jax.ShapeDtypeStruct((b, N_KV_HEADS, t, HEAD_DIM), BF16),
        ],
        compiler_params=_cparams(("parallel", "parallel")),
        name="inproj_cd",
    )(x, modtab, w_in, q_ln.reshape(1, -1), kv_ln.reshape(1, -1), w_uq, w_ukv, cos, sin, cosm, sinm)


def _prep_cd_weights(w_in, w_uq, w_ukv):
    d = w_in.shape[0]
    q_lora = w_uq.shape[0]
    kv_lora = w_ukv.shape[0]
    o = q_lora + kv_lora
    k_rope = w_in[:, o:o + C_ROPE]
    rest = w_in[:, o + C_ROPE:]
    w_in_r = jnp.concatenate([w_in[:, :o], rest, k_rope, jnp.zeros((d, LANES - C_ROPE), w_in.dtype)], axis=1)
    uq = w_uq.reshape(q_lora, N_HEADS, HEAD_DIM + C_ROPE)
    uq = jnp.pad(uq, ((0, 0), (0, 0), (0, MLA_QK - HEAD_DIM - C_ROPE))).reshape(q_lora, N_HEADS * MLA_QK)
    ukv = w_ukv.reshape(kv_lora, N_HEADS, 2, HEAD_DIM).transpose(0, 2, 1, 3).reshape(kv_lora, 2 * N_HEADS * HEAD_DIM)
    return w_in_r.astype(BF16), uq.astype(BF16), ukv.astype(BF16)


def _softmax_pv(blocks, sink_col):
    mx = None
    for s, _ in blocks:
        bm = jnp.max(s, axis=-1, keepdims=True)
        mx = bm if mx is None else jnp.maximum(mx, bm)
    if sink_col is not None:
        mx = jnp.maximum(mx, sink_col)
    den = None
    acc = None
    for s, v in blocks:
        p = jnp.exp(s - mx)
        ps = jnp.sum(p, axis=-1, keepdims=True)
        den = ps if den is None else den + ps
        pv = _dot(p.astype(BF16), v)
        acc = pv if acc is None else acc + pv
    if sink_col is not None:
        den = den + jnp.exp(sink_col - mx)
    return acc / den


def _attn_kernel(sink_ref, q_ref, k_ref, v_ref, o_ref, *, n_ctx, tq, rep, windowed, use_sink):
    g = pl.program_id(1)
    qi = pl.program_id(2)
    t_all = k_ref.shape[0]
    dk = q_ref.shape[-1]
    dv = v_ref.shape[-1]
    q = q_ref[...].reshape(rep * tq, dk)
    nct = n_ctx // tq

    if use_sink:
        row = lax.broadcasted_iota(I32, (rep * tq, 1), 0)
        sink_col = jnp.full((rep * tq, 1), sink_ref[g * rep], F32)
        for r in range(1, rep):
            sink_col = jnp.where(row >= r * tq, sink_ref[g * rep + r], sink_col)
    else:
        sink_col = None

    def emit(o):
        for r in range(rep):
            o_ref[:, r * dv:(r + 1) * dv] = o[r * tq:(r + 1) * tq].astype(o_ref.dtype)

    @pl.when(qi < nct)
    def _():
        kc = k_ref[0:n_ctx, :]
        vc = v_ref[0:n_ctx, :]
        emit(_softmax_pv([(_dot_nt(q, kc), vc)], sink_col))

    @pl.when(qi >= nct)
    def _():
        if not windowed:
            emit(_softmax_pv([(_dot_nt(q, k_ref[...]), v_ref[...])], sink_col))
        else:
            band = tq + 2 * D_WINDOW
            s0 = (qi - nct) * tq
            kstart = jnp.clip(n_ctx + s0 - D_WINDOW, n_ctx, t_all - band)
            kstart = pl.multiple_of(kstart, LANES)
            kb = k_ref[pl.ds(kstart, band), :]
            vb = v_ref[pl.ds(kstart, band), :]
            sb = _dot_nt(q, kb)
            rowq = lax.broadcasted_iota(I32, (rep * tq, band), 0) % tq + s0
            colk = lax.broadcasted_iota(I32, (rep * tq, band), 1) + (kstart - n_ctx)
            sb = jnp.where(jnp.abs(colk - rowq) <= D_WINDOW, sb, NEG_INF)
            kc = k_ref[0:n_ctx, :]
            vc = v_ref[0:n_ctx, :]
            emit(_softmax_pv([(_dot_nt(q, kc), vc), (sb, vb)], sink_col))


def attention(q, k, v, n_ctx, *, windowed=False, sink=None):
    b, g, rep, t, dk = q.shape
    dv = v.shape[-1]
    tq = ROW_TILE
    use_sink = sink is not None
    if sink is None:
        sink = jnp.zeros((g * rep,), F32)
    kern = functools.partial(_attn_kernel, n_ctx=n_ctx, tq=tq, rep=rep, windowed=windowed, use_sink=use_sink)
    return pl.pallas_call(
        kern,
        grid=(b, g, t // tq),
        in_specs=[
            pl.BlockSpec(memory_space=pltpu.SMEM),
            pl.BlockSpec((None, None, rep, tq, dk), lambda bi, gi, i: (bi, gi, 0, i, 0)),
            pl.BlockSpec((None, None, t, dk), lambda bi, gi, i: (bi, gi, 0, 0)),
            pl.BlockSpec((None, None, t, dv), lambda bi, gi, i: (bi, gi, 0, 0)),
        ],
        out_specs=pl.BlockSpec((None, tq, rep * dv), lambda bi, gi, i: (bi, i, gi)),
        out_shape=jax.ShapeDtypeStruct((b, t, g * rep * dv), BF16),
        compiler_params=_cparams(("parallel", "parallel", "arbitrary")),
        name="attention_win" if windowed else "attention",
    )(sink.astype(F32), q, k, v)


def _s5_tables(lam_re, lam_im, log_dt, b_re, b_im, c_re, c_im, d_skip):
    n_groups, n_state = lam_re.shape[1:]
    L, gs = S5_CHUNK, S5_GROUP
    lam = lax.complex(lam_re.astype(F32), lam_im.astype(F32))
    dt = jnp.exp(log_dt.astype(F32))[..., None]
    lam_dt = lam * dt
    lam_bar = jnp.exp(lam_dt)
    b_bar = ((lam_bar - 1.0) / lam)[..., None] * lax.complex(b_re.astype(F32), b_im.astype(F32))
    c_mat = lax.complex(c_re.astype(F32), c_im.astype(F32))
    pw = jnp.exp(lam_dt[None] * jnp.arange(L + 1, dtype=F32)[:, None, None, None])
    kker = jnp.einsum("dgop,tdgp,dgpi->dtgoi", c_mat, pw[:L], b_bar).real
    s_in = jnp.arange(L)[:, None]
    s_out = jnp.arange(L)[None, :]
    tau_f = s_out - s_in
    tau_r = s_in - s_out
    kf = jnp.where((tau_f >= 0)[:, :, None, None, None], kker[0][jnp.clip(tau_f, 0, L - 1)], 0.0)
    kr = jnp.where((tau_r >= 0)[:, :, None, None, None], kker[1][jnp.clip(tau_r, 0, L - 1)], 0.0)
    kt = kf + kr
    gp = S5_PACK
    n_packs = n_groups // gp
    eye = jnp.eye(gp, dtype=F32)
    t6 = kt.transpose(2, 0, 4, 1, 3).reshape(n_packs, gp, L, gs, L, gs)
    d6 = d_skip.astype(F32).reshape(n_packs, gp, 1, gs, 1, 1) * (
        jnp.eye(L, dtype=F32)[None, None, :, None, :, None] * jnp.eye(gs, dtype=F32)[None, None, None, :, None, :])
    tmat = jnp.einsum("pgsctd,gh->psgcthd", t6 + d6, eye).reshape(n_packs, L * LANES, L * LANES)
    steps = jnp.arange(L, dtype=F32)[:, None, None]
    wf = jnp.exp(lam_dt[0][None] * (L - 1 - steps))[:, :, :, None] * b_bar[0][None]
    wr = jnp.exp(lam_dt[1][None] * steps)[:, :, :, None] * b_bar[1][None]
    def w_pack(w):
        w2 = jnp.stack([w.real, w.imag], axis=0).reshape(2, L, n_packs, gp, n_state, gs)
        return jnp.einsum("bspgqc,gh->psgcbhq", w2, eye).reshape(n_packs, L * LANES, 2 * gp * n_state)
    vf = c_mat[0][None] * jnp.exp(lam_dt[0][None] * (steps + 1))[:, :, None, :]
    vr = c_mat[1][None] * jnp.exp(lam_dt[1][None] * (L - steps))[:, :, None, :]
    def v_pack(vv):
        v2 = jnp.stack([vv.real, -vv.imag], axis=0).reshape(2, L, n_packs, gp, gs, n_state)
        return jnp.einsum("bspgcq,gh->pbgqshc", v2, eye).reshape(n_packs, 2 * gp * n_state, L * LANES)
    def lam_pack(l):
        return jnp.stack([l.real, l.imag], axis=0).reshape(2, n_packs, gp * n_state).transpose(1, 0, 2)
    lam_l = pw[L]
    return (tmat.astype(BF16),
            (w_pack(wf).astype(BF16), v_pack(vf).astype(BF16), lam_pack(lam_l[0])),
            (w_pack(wr).astype(BF16), v_pack(vr).astype(BF16), lam_pack(lam_l[1])))


def _s5_pass_kernel(*refs, reverse):
    if reverse:
        u_ref, yin_ref, w_ref, v_ref, lam_ref, y_ref, s_scr, h_scr = refs
    else:
        u_ref, t_ref, w_ref, v_ref, lam_ref, y_ref, s_scr, h_scr = refs

    @pl.when(pl.program_id(1) == 0)
    def _():
        h_scr[...] = jnp.zeros_like(h_scr)

    nchs, L, b, _ = u_ref.shape
    rows = nchs * b
    half = h_scr.shape[1] // 2
    xg = jnp.concatenate([u_ref[:, s].reshape(rows, LANES) for s in range(L)], axis=-1)
    s_scr[...] = _dot(xg, w_ref[...])
    lr = jnp.broadcast_to(lam_ref[0:1, :], (b, half))
    li = jnp.broadcast_to(lam_ref[1:2, :], (b, half))

    def step(k, carry):
        hr, hi = carry
        j = nchs - 1 - k if reverse else k
        r0 = pl.multiple_of(j * b, b)
        sr = s_scr[pl.ds(r0, b), 0:half]
        si = s_scr[pl.ds(r0, b), half:]
        s_scr[pl.ds(r0, b), 0:half] = hr
        s_scr[pl.ds(r0, b), half:] = hi
        return lr * hr - li * hi + sr, lr * hi + li * hr + si

    hr, hi = lax.fori_loop(0, nchs, step, (h_scr[:, 0:half], h_scr[:, half:]))
    h_scr[:, 0:half] = hr
    h_scr[:, half:] = hi
    y = _dot(s_scr[...].astype(BF16), v_ref[...])
    if not reverse:
        y = y + _dot(xg, t_ref[...])
    for s in range(L):
        blk = y[:, s * LANES:(s + 1) * LANES].reshape(nchs, b, LANES)
        if reverse:
            blk = blk + yin_ref[:, s]
        y_ref[:, s] = blk


def s5_apply(u, tables, n_ctx):
    tmat, fwd, rev = tables
    b, t, ch = u.shape
    L = S5_CHUNK
    n_packs = ch // LANES
    seg = ROW_TILE
    nseg, nct, nchs = t // seg, n_ctx // seg, seg // L
    width = L * LANES
    n_state2 = fwd[0].shape[-1]
    ut = u.transpose(1, 0, 2).reshape(t // L, L, b, ch)
    blk = pl.BlockSpec((nchs, L, b, LANES), lambda p, i: (i, 0, 0, p))
    rseg = lambda i: jnp.where(i < nct, nct - 1 - i, nseg - 1 - (i - nct))
    rblk = pl.BlockSpec((nchs, L, b, LANES), lambda p, i: (rseg(i), 0, 0, p))
    tab = lambda shape: pl.BlockSpec((None,) + shape, lambda p, i: (p, 0, 0))
    scratch = [pltpu.VMEM((nchs * b, n_state2), F32), pltpu.VMEM((b, n_state2), F32)]
    y_shape = jax.ShapeDtypeStruct((t // L, L, b, ch), F32)
    y_f = pl.pallas_call(
        functools.partial(_s5_pass_kernel, reverse=False),
        grid=(n_packs, nseg),
        in_specs=[blk, tab((width, width)), tab((width, n_state2)), tab((n_state2, width)), tab((2, n_state2 // 2))],
        out_specs=blk,
        out_shape=y_shape,
        scratch_shapes=scratch,
        compiler_params=_cparams(("parallel", "arbitrary")),
        name="s5_forward",
    )(ut, tmat, *fwd)
    y = pl.pallas_call(
        functools.partial(_s5_pass_kernel, reverse=True),
        grid=(n_packs, nseg),
        in_specs=[rblk, rblk, tab((width, n_state2)), tab((n_state2, width)), tab((2, n_state2 // 2))],
        out_specs=rblk,
        out_shape=y_shape,
        scratch_shapes=scratch,
        input_output_aliases={1: 0},
        compiler_params=_cparams(("parallel", "arbitrary")),
        name="s5_reverse",
    )(ut, y_f, *rev)
    return y.reshape(t, b, ch).transpose(1, 0, 2)


def _gelu_tanh(x):
    return 0.5 * x * (1.0 + jnp.tanh(math.sqrt(2.0 / math.pi) * (x + 0.044715 * (x * x * x))))


def _post_kernel(*refs, alpha, glu, n_exp):
    if glu:
        (x_ref, a_ref, s_ref, wg_ref, bg_ref, wm_ref, mod_ref, g_ref, b_ref, wrh_ref, wrl_ref, br_ref,
         x1_ref, h2_ref, idx_ref, gate_ref, rank_ref, cnt_ref, cnt_scr) = refs
    else:
        (x_ref, a_ref, s_ref, wm_ref, mod_ref, g_ref, b_ref, wrh_ref, wrl_ref, br_ref,
         x1_ref, h2_ref, idx_ref, gate_ref, rank_ref, cnt_ref, cnt_scr) = refs
    first = (pl.program_id(0) == 0) & (pl.program_id(1) == 0)

    @pl.when(first)
    def _():
        cnt_scr[...] = jnp.zeros_like(cnt_scr)

    tq, d = x_ref.shape
    half = a_ref.shape[1]
    if glu:
        z = _gelu_tanh(s_ref[...])
        gate = jax.nn.sigmoid(_dot(z.astype(BF16), wg_ref[...]) + bg_ref[...])
        second = (z * gate).astype(BF16)
    else:
        second = s_ref[...]
    mix = _dot(a_ref[...], wm_ref[0:half, :]) + _dot(second, wm_ref[half:, :])
    x1 = _layer_norm(alpha * x_ref[...] + mod_ref[2:3, :] * mix, g_ref[...], b_ref[...])
    x1_ref[...] = x1
    h2 = x1 * (1.0 + mod_ref[4:5, :]) + mod_ref[3:4, :]
    _store_token_tiles(h2_ref, h2)

    h_hi, h_lo = _split_bf16(h2)
    logits = _dot_nt(wrh_ref[...], h_hi) + _dot_nt(wrh_ref[...], h_lo) + _dot_nt(wrl_ref[...], h_hi) + br_ref[...]
    eidx = lax.broadcasted_iota(I32, (n_exp, tq), 0)
    work = logits
    tops, sels = [], []
    for k in range(TOP_K):
        m = jnp.max(work, axis=0, keepdims=True)
        ik = jnp.min(jnp.where(work == m, eidx, n_exp), axis=0, keepdims=True)
        sel = eidx == ik
        work = jnp.where(sel, -jnp.inf, work)
        tops.append(m)
        sels.append(sel)
        idx_ref[k:k + 1, :] = ik
    exps = [jnp.exp(tk - tops[0]) for tk in tops]
    den = exps[0] + exps[1] + exps[2] + exps[3]
    for k in range(TOP_K):
        gate_ref[k:k + 1, :] = exps[k] / den
    onehot = jnp.zeros((n_exp, tq), F32)
    for sel in sels:
        onehot = onehot + sel.astype(F32)
    tri = (lax.broadcasted_iota(I32, (tq, tq), 0) < lax.broadcasted_iota(I32, (tq, tq), 1)).astype(BF16)
    before = _dot(onehot.astype(BF16), tri) + cnt_scr[...]
    for k in range(TOP_K):
        rk = jnp.sum(jnp.where(sels[k], before, 0.0), axis=0, keepdims=True)
        rank_ref[k:k + 1, :] = rk.astype(I32)
    cnt_scr[...] += jnp.sum(onehot, axis=1, keepdims=True)
    cnt_ref[...] = jnp.broadcast_to(cnt_scr[...], cnt_ref.shape).astype(I32)


def post_mixer(x, a, s, modtab, w_mix, ln_g, ln_b, w_router, b_router, n_ctx, alpha, glu_w=None, glu_b=None):
    b, t, d = x.shape
    half = a.shape[-1]
    n_exp = w_router.shape[1]
    tq = ROW_TILE
    nct = n_ctx // tq
    seg = lambda i: jnp.minimum(i // nct, 1) if nct > 0 else 1
    glu = glu_w is not None
    wr_hi, wr_lo = _split_bf16(w_router.T.astype(F32))
    full = lambda shape: pl.BlockSpec(shape, lambda bi, i: (0,) * len(shape))
    tok = lambda width: pl.BlockSpec((None, tq, width), lambda bi, i: (bi, i, 0))
    in_specs = [tok(d), tok(half), tok(half)]
    args = [x, a, s]
    if glu:
        in_specs += [full((half, half)), full((1, half))]
        args += [glu_w.astype(BF16), glu_b.reshape(1, half)]
    in_specs += [full((d, d)), pl.BlockSpec((None, None, 6, d), lambda bi, i: (bi, seg(i), 0, 0)),
                 full((1, d)), full((1, d)), full((n_exp, d)), full((n_exp, d)), full((n_exp, 1))]
    args += [w_mix.astype(BF16), modtab, ln_g.reshape(1, d), ln_b.reshape(1, d), wr_hi, wr_lo,
             b_router.reshape(n_exp, 1)]
    nt = t // tq
    lane_out = lambda dt: (pl.BlockSpec((TOP_K, tq), lambda bi, i: (0, bi * nt + i)),
                           jax.ShapeDtypeStruct((TOP_K, b * t), dt))
    outs = [
        (tok(d), jax.ShapeDtypeStruct((b, t, d), F32)),
        (pl.BlockSpec((tq * SUBLANES, LANES), lambda bi, i: (bi * nt + i, 0)),
         jax.ShapeDtypeStruct((b * t * SUBLANES, LANES), F32)),
        lane_out(I32), lane_out(F32), lane_out(I32),
        (pl.BlockSpec((n_exp, LANES), lambda bi, i: (0, 0)), jax.ShapeDtypeStruct((n_exp, LANES), I32)),
    ]
    kern = functools.partial(_post_kernel, alpha=alpha, glu=glu, n_exp=n_exp)
    return pl.pallas_call(
        kern,
        grid=(b, nt),
        in_specs=in_specs,
        out_specs=[o[0] for o in outs],
        out_shape=[o[1] for o in outs],
        scratch_shapes=[pltpu.VMEM((n_exp, 1), F32)],
        compiler_params=_cparams(("arbitrary", "arbitrary")),
        name="post_mixer_glu" if glu else "post_mixer",
    )(*args)


def _route_kernel(start_ref, idx_ref, rank_ref, pos_ref, *, n_exp, tq):
    eidx = lax.broadcasted_iota(I32, (n_exp, tq), 0)
    for k in range(TOP_K):
        sel = eidx == idx_ref[k:k + 1, :]
        base = jnp.sum(jnp.where(sel, start_ref[...], 0.0), axis=0, keepdims=True)
        pos_ref[:, k * tq:(k + 1) * tq] = base.astype(I32) + rank_ref[k:k + 1, :]


def route_positions(group_start, idx, rank):
    n_exp = group_start.shape[0]
    n_tok = idx.shape[1]
    tq = ROW_TILE
    return pl.pallas_call(
        functools.partial(_route_kernel, n_exp=n_exp, tq=tq),
        grid=(n_tok // tq,),
        in_specs=[
            pl.BlockSpec((n_exp, 1), lambda i: (0, 0)),
            pl.BlockSpec((TOP_K, tq), lambda i: (0, i)),
            pl.BlockSpec((TOP_K, tq), lambda i: (0, i)),
        ],
        out_specs=pl.BlockSpec((None, 1, TOP_K * tq), lambda i: (i, 0, 0)),
        out_shape=jax.ShapeDtypeStruct((n_tok // tq, 1, TOP_K * tq), I32),
        compiler_params=_cparams(("parallel",)),
        name="route_positions",
    )(group_start.astype(F32).reshape(n_exp, 1), idx, rank)


def _dispatch_kernel(pos_hbm, h_ref, xs_hbm, idx_smem, idx_sem, row_sem, *, tq, n_tiles, tile_rows):
    i = pl.program_id(0)
    slot = i % 2

    def idx_copy(tile, sl):
        return pltpu.make_async_copy(pos_hbm.at[tile], idx_smem.at[sl], idx_sem.at[sl])

    @pl.when(i == 0)
    def _():
        idx_copy(0, 0).start()

    idx_copy(i, slot).wait()

    @pl.when(i + 1 < n_tiles)
    def _():
        idx_copy(i + 1, 1 - slot).start()

    for k in range(TOP_K):
        for r in range(tq):
            dst = pl.multiple_of(idx_smem[slot, 0, k * tq + r] * tile_rows, tile_rows)
            pltpu.make_async_copy(h_ref.at[pl.ds(r * tile_rows, tile_rows), :],
                                  xs_hbm.at[pl.ds(dst, tile_rows), :], row_sem.at[0]).start()
    for k in range(TOP_K):
        pltpu.make_async_copy(h_ref, xs_hbm.at[pl.ds(0, tq * tile_rows), :], row_sem.at[0]).wait()


def moe_dispatch(pos_tiles, h_tiles, tile_rows):
    n_tiles = pos_tiles.shape[0]
    tq = h_tiles.shape[0] // tile_rows // n_tiles
    return pl.pallas_call(
        functools.partial(_dispatch_kernel, tq=tq, n_tiles=n_tiles, tile_rows=tile_rows),
        grid=(n_tiles,),
        in_specs=[pl.BlockSpec(memory_space=pl.ANY),
                  pl.BlockSpec((tq * tile_rows, LANES), lambda i: (i, 0))],
        out_specs=pl.BlockSpec(memory_space=pl.ANY),
        out_shape=jax.ShapeDtypeStruct((TOP_K * h_tiles.shape[0], LANES), F32),
        scratch_shapes=[
            pltpu.SMEM((2, 1, TOP_K * tq), I32),
            pltpu.SemaphoreType.DMA((2,)),
            pltpu.SemaphoreType.DMA((1,)),
        ],
        compiler_params=_cparams(("arbitrary",)),
        name="moe_dispatch",
    )(pos_tiles, h_tiles)


def _moe_kernel(e_ref, j_ref, lo_ref, hi_ref, first_ref, new_ref, x_ref, wi_ref, bi_ref, wo_ref, bo_ref,
                y_ref, wi_b, wo_b, *, n_sub):
    i = pl.program_id(0)
    lo = lo_ref[i]
    hi = hi_ref[i]

    @pl.when(new_ref[i] == 1)
    def _():
        wi_b[...] = wi_ref[...].astype(BF16)
        wo_b[...] = wo_ref[...].astype(BF16)

    tm = x_ref.shape[0] // n_sub

    @pl.when(hi > lo)
    def _():
        x = _load_token_tiles(x_ref, 0, tm, n_sub).astype(BF16)
        z = _dot(x, wi_b[...]) + bi_ref[...]
        f = z.shape[1] // 2
        glu = jnp.minimum(z[:, :f], SWIGLU_LIMIT)
        lin = jnp.clip(z[:, f:], -SWIGLU_LIMIT, SWIGLU_LIMIT)
        act = glu * jax.nn.sigmoid(SWIGLU_ALPHA * glu) * (lin + 1.0)
        y = _dot(act.astype(BF16), wo_b[...]) + bo_ref[...]
        row = lax.broadcasted_iota(I32, (tm, 1), 0)
        mine = (row >= lo) & (row < hi)

        @pl.when(first_ref[i] == 1)
        def _():
            _store_token_tiles(y_ref, jnp.where(mine, y, 0.0))

        @pl.when(first_ref[i] == 0)
        def _():
            _store_token_tiles(y_ref, jnp.where(mine, y, _load_token_tiles(y_ref, 0, tm, n_sub)))


def moe_experts(xs, items, w_in, b_in, w_out, b_out, layer):
    _, n_exp, d, f2 = w_in.shape
    n_sub = d // LANES
    n_items = items[0].shape[0]
    tm = MOE_BLOCK
    wmap = lambda i, e, j, lo, hi, fi, nw: (layer, e[i], 0, 0)
    xmap = lambda i, e, j, lo, hi, fi, nw: (j[i], 0)
    grid_spec = pltpu.PrefetchScalarGridSpec(
        num_scalar_prefetch=6,
        grid=(n_items,),
        in_specs=[
            pl.BlockSpec((tm * n_sub, LANES), xmap),
            pl.BlockSpec((None, None, d, f2), wmap),
            pl.BlockSpec((None, None, 1, f2), wmap),
            pl.BlockSpec((None, None, f2 // 2, d), wmap),
            pl.BlockSpec((None, None, 1, d), wmap),
        ],
        out_specs=pl.BlockSpec((tm * n_sub, LANES), xmap),
        scratch_shapes=[pltpu.VMEM((d, f2), BF16), pltpu.VMEM((f2 // 2, d), BF16)],
    )
    return pl.pallas_call(
        functools.partial(_moe_kernel, n_sub=n_sub),
        grid_spec=grid_spec,
        out_shape=jax.ShapeDtypeStruct(xs.shape, F32),
        compiler_params=_cparams(("arbitrary",)),
        name="moe_experts",
    )(*items, xs, w_in, b_in.reshape(b_in.shape[0], n_exp, 1, f2), w_out, b_out.reshape(b_out.shape[0], n_exp, 1, d))


def _combine_kernel(pos_hbm, x_ref, g_ref, mod_ref, lg_ref, lb_ref, ys_hbm, o_ref,
                    idx_smem, ybuf, idx_sem, row_sem, *, alpha, tq, n_tiles):
    i = pl.program_id(0)
    slot = i % 2
    d = x_ref.shape[1]

    def idx_copy(tile, sl):
        return pltpu.make_async_copy(pos_hbm.at[tile], idx_smem.at[sl], idx_sem.at[sl])

    n_sub = d // LANES

    def gather_start(sl):
        for r in range(TOP_K * tq):
            src = pl.multiple_of(idx_smem[sl, 0, r] * n_sub, n_sub)
            pltpu.make_async_copy(ys_hbm.at[pl.ds(src, n_sub), :], ybuf.at[sl, pl.ds(r * n_sub, n_sub), :],
                                  row_sem.at[sl]).start()

    def rows_wait(sl):
        pltpu.make_async_copy(ys_hbm.at[pl.ds(0, TOP_K * tq * n_sub), :], ybuf.at[sl], row_sem.at[sl]).wait()

    @pl.when(i == 0)
    def _():
        idx_copy(0, 0).start()
        idx_copy(0, 0).wait()
        gather_start(0)

        @pl.when(n_tiles > 1)
        def _():
            idx_copy(1, 1).start()

    @pl.when(i + 1 < n_tiles)
    def _():
        idx_copy(i + 1, 1 - slot).wait()
        gather_start(1 - slot)

    rows_wait(slot)

    @pl.when(i + 2 < n_tiles)
    def _():
        idx_copy(i + 2, slot).start()

    yb = ybuf.at[slot]
    gates = g_ref[...]
    y = None
    for k in range(TOP_K):
        term = gates[:, k:k + 1] * _load_token_tiles(yb, k * tq * n_sub, tq, n_sub)
        y = term if y is None else y + term
    o_ref[...] = _layer_norm(alpha * x_ref[...] + mod_ref[5:6, :] * y, lg_ref[...], lb_ref[...])


def moe_combine(x1, pos_tiles, gates_tok, modtab, ln_g, ln_b, ys, n_ctx, alpha):
    b, t, d = x1.shape
    tq = ROW_TILE
    nt = t // tq
    n_tiles = b * nt
    nct = n_ctx // tq
    seg = lambda i: jnp.minimum((i % nt) // nct, 1) if nct > 0 else 1
    kern = functools.partial(_combine_kernel, alpha=alpha, tq=tq, n_tiles=n_tiles)
    out = pl.pallas_call(
        kern,
        grid=(n_tiles,),
        in_specs=[
            pl.BlockSpec(memory_space=pl.ANY),
            pl.BlockSpec((tq, d), lambda i: (i, 0)),
            pl.BlockSpec((tq, TOP_K), lambda i: (i, 0)),
            pl.BlockSpec((None, None, 6, d), lambda i: (i // nt, seg(i), 0, 0)),
            pl.BlockSpec((1, d), lambda i: (0, 0)),
            pl.BlockSpec((1, d), lambda i: (0, 0)),
            pl.BlockSpec(memory_space=pl.ANY),
        ],
        out_specs=pl.BlockSpec((tq, d), lambda i: (i, 0)),
        out_shape=jax.ShapeDtypeStruct((b * t, d), F32),
        scratch_shapes=[
            pltpu.SMEM((2, 1, TOP_K * tq), I32),
            pltpu.VMEM((2, TOP_K * tq * (d // LANES), LANES), F32),
            pltpu.SemaphoreType.DMA((2,)),
            pltpu.SemaphoreType.DMA((2,)),
        ],
        compiler_params=_cparams(("arbitrary",)),
        name="moe_combine",
    )(pos_tiles, x1.reshape(b * t, d), gates_tok, modtab, ln_g.reshape(1, d), ln_b.reshape(1, d), ys)
    return out.reshape(b, t, d)


def _group_items(counts, n_slots, tm):
    n_exp = counts.shape[0]
    le = (jnp.arange(n_exp)[:, None] <= jnp.arange(n_exp)[None, :]).astype(I32)
    g_end = counts @ le
    g_start = g_end - counts
    first_blk = g_start // tm
    n_blk = jnp.where(counts > 0, (g_end - 1) // tm - first_blk + 1, 0)
    i_end = n_blk @ le
    i_start = i_end - n_blk
    n_items = n_slots // tm + n_exp - 1
    it = jnp.arange(n_items, dtype=I32)
    valid = it < i_end[-1]
    e = jnp.minimum(jnp.sum((i_end[None, :] <= it[:, None]).astype(I32), axis=1), n_exp - 1)
    onehot = (e[:, None] == jnp.arange(n_exp)[None, :]).astype(I32)
    pick = lambda tbl: onehot @ tbl
    blk = pick(first_blk) + it - pick(i_start)
    lo = jnp.clip(pick(g_start) - blk * tm, 0, tm)
    hi = jnp.clip(pick(g_end) - blk * tm, 0, tm)
    last = jnp.maximum(i_end[-1] - 1, 0)
    e_last = jnp.sum(jnp.where(it == last, e, 0))
    e = jnp.where(valid, e, e_last)
    blk = jnp.where(valid, blk, n_slots // tm - 1)
    lo = jnp.where(valid, lo, 0)
    hi = jnp.where(valid, hi, 0)
    prev = lambda a: jnp.concatenate([jnp.full((1,), -1, I32), a[:-1]])
    first = (blk != prev(blk)).astype(I32)
    new_e = (e != prev(e)).astype(I32)
    return g_start, tuple(a.astype(I32) for a in (e, blk, lo, hi, first, new_e))


def kernel(x, c, ctx, c_ctx, w_mod, b_mod, ln_g, ln_b, w_mix_out, w_router, b_router, w_exp_in, b_exp_in, w_exp_out, b_exp_out, w_in_ab, a_q_norm, a_k_norm, s5_lam_re, s5_lam_im, s5_log_dt, s5_b_re, s5_b_im, s5_c_re, s5_c_im, s5_d, w_glu, b_glu, w_in_cd, c_q_norm, c_kv_norm, w_uq, w_ukv, d_sink):
    batch, n_lat, d = x.shape
    n_ctx = ctx.shape[1]
    depth = w_mod.shape[0]
    alpha = (2.0 * depth) ** 0.25
    t = n_ctx + n_lat
    n_tok = batch * t

    xs = jnp.concatenate([ctx, x], axis=1)
    n_rows = -(-(batch + 1) // SUBLANES) * SUBLANES
    cvec = jnp.zeros((n_rows, d), F32).at[:batch].set(c).at[batch].set(c_ctx)
    mods = modulation(cvec, w_mod, b_mod).reshape(depth, n_rows, 6, d)
    modtab = jnp.stack([jnp.broadcast_to(mods[:, batch:batch + 1], (depth, batch, 6, d)), mods[:, :batch]],
                       axis=2)

    cos_h, sin_h = _rope_tables(n_ctx, n_lat, HEAD_DIM)
    cos_m, sin_m = _rope_tables(n_ctx, n_lat, C_ROPE)

    for layer in range(depth):
        i = layer // 2
        mt = modtab[layer]
        if layer % 2 == 0:
            q, k, v, u = inproj_ab(xs, mt, w_in_ab[i].astype(BF16), a_q_norm[i], a_k_norm[i], cos_h, sin_h, n_ctx)
            att = attention(q, k, v, n_ctx)
            tables = _s5_tables(s5_lam_re[i], s5_lam_im[i], s5_log_dt[i], s5_b_re[i], s5_b_im[i],
                                s5_c_re[i], s5_c_im[i], s5_d[i])
            second = s5_apply(u, tables, n_ctx)
            glu_w, glu_b = w_glu[i], b_glu[i]
        else:
            w_in_r, uq_r, ukv_r = _prep_cd_weights(w_in_cd[i], w_uq[i], w_ukv[i])
            qm, km, vm, qd, kd, vd = inproj_cd(xs, mt, w_in_r, c_q_norm[i], c_kv_norm[i], uq_r, ukv_r,
                                               cos_h, sin_h, cos_m, sin_m, n_ctx)
            att = attention(qm, km, vm, n_ctx)
            second = attention(qd, kd, vd, n_ctx, windowed=True, sink=d_sink[i])
            glu_w = glu_b = None
        x1, h2, idx, gates, rank, counts = post_mixer(
            xs, att, second, mt, w_mix_out[layer], ln_g[layer, 0], ln_b[layer, 0],
            w_router[layer], b_router[layer], n_ctx, alpha, glu_w, glu_b)
        g_start, items = _group_items(counts[:, 0], TOP_K * n_tok, MOE_BLOCK)
        pos_tiles = route_positions(g_start, idx, rank)
        sorted_rows = moe_dispatch(pos_tiles, h2, d // LANES)
        ys = moe_experts(sorted_rows, items, w_exp_in, b_exp_in, w_exp_out, b_exp_out, layer)
        xs = moe_combine(x1, pos_tiles, gates.T, mt, ln_g[layer, 1], ln_b[layer, 1], ys, n_ctx, alpha)
    return xs[:, n_ctx:, :]
```

```python
import functools
import math

import jax
import jax.numpy as jnp
from jax import lax
from jax.experimental import pallas as pl
from jax.experimental.pallas import tpu as pltpu

F32 = jnp.float32
BF16 = jnp.bfloat16
I32 = jnp.int32

GRID_W = 64
ROPE_THETA = 10000.0
HEAD_DIM = 128
N_HEADS = 4
N_KV_HEADS = 2
C_ROPE = 64
MLA_QK = 256
D_WINDOW = 128
S5_GROUP = 16
S5_STATE = 64
S5_CHUNK = 8
S5_PACK = 8
TOP_K = 4
SWIGLU_LIMIT = 7.0
SWIGLU_ALPHA = 1.702
NEG_INF = -1e30
LN_EPS = 1e-5
RMS_EPS = 1e-6

LANES = 128
SUBLANES = 8
VMEM_LIMIT_BYTES = 56 * 1024 * 1024

ROW_TILE = 256
MOE_BLOCK = 256
FF_CHUNK = 256
KEY_CHUNK = 768


def _cparams(sem):
    return pltpu.CompilerParams(dimension_semantics=sem, vmem_limit_bytes=VMEM_LIMIT_BYTES)


def _split_bf16(a):
    hi = a.astype(BF16)
    lo = (a - hi.astype(F32)).astype(BF16)
    return hi, lo


def _dot(a, b):
    return jnp.dot(a, b, preferred_element_type=F32)


def _dot_nt(a, b):
    return lax.dot_general(a, b, (((1,), (1,)), ((), ())), preferred_element_type=F32)


def _store_token_tiles(ref, val):
    rows, d = val.shape
    n = d // LANES
    for s in range(n):
        ref[pl.ds(s, rows, stride=n), :] = val[:, s * LANES:(s + 1) * LANES]


def _load_token_tiles(ref, first_row, rows, n):
    return jnp.concatenate([ref[pl.ds(first_row + s, rows, stride=n), :] for s in range(n)], axis=-1)


def _layer_norm(x, g, b):
    mu = jnp.mean(x, axis=-1, keepdims=True)
    xc = x - mu
    var = jnp.mean(xc * xc, axis=-1, keepdims=True)
    return xc * lax.rsqrt(var + LN_EPS) * g + b


def _rms(x, g):
    return x * lax.rsqrt(jnp.mean(x * x, axis=-1, keepdims=True) + RMS_EPS) * g


def _mod_kernel(c_ref, w_ref, b_ref, o_ref):
    c = c_ref[...]
    s = c * jax.nn.sigmoid(c)
    s_hi, s_lo = _split_bf16(s)
    w_hi, w_lo = _split_bf16(w_ref[...])
    o_ref[...] = _dot(s_hi, w_hi) + _dot(s_lo, w_hi) + _dot(s_hi, w_lo) + b_ref[...]


def modulation(cvec, w_mod, b_mod):
    n_layers, d, d6 = w_mod.shape
    r = cvec.shape[0]
    tn = 1536
    return pl.pallas_call(
        _mod_kernel,
        grid=(n_layers, d6 // tn),
        in_specs=[
            pl.BlockSpec((r, d), lambda l, j: (0, 0)),
            pl.BlockSpec((None, d, tn), lambda l, j: (l, 0, j)),
            pl.BlockSpec((None, 1, tn), lambda l, j: (l, 0, j)),
        ],
        out_specs=pl.BlockSpec((None, r, tn), lambda l, j: (l, 0, j)),
        out_shape=jax.ShapeDtypeStruct((n_layers, r, d6), F32),
        compiler_params=_cparams(("arbitrary", "arbitrary")),
        name="modulation",
    )(cvec, w_mod, b_mod.reshape(n_layers, 1, d6))


def _rope_lanes(t, cos, sin, quarter):
    lane = lax.broadcasted_iota(I32, t.shape, 1)
    first = (lane % (2 * quarter)) < quarter
    partner = jnp.where(first, pltpu.roll(t, LANES - quarter, 1), pltpu.roll(t, quarter, 1))
    return t * cos + partner * sin


def _rope_tables(n_ctx, n_lat, rot_dim):
    quarter = rot_dim // 4
    pos = jnp.arange(n_lat, dtype=F32)
    row = jnp.floor(pos / GRID_W)
    col = pos - row * GRID_W
    inv_freq = ROPE_THETA ** (-jnp.arange(quarter, dtype=F32) / quarter)
    ang_r = row[:, None] * inv_freq
    ang_c = col[:, None] * inv_freq
    cos = jnp.concatenate([jnp.cos(ang_r)] * 2 + [jnp.cos(ang_c)] * 2, axis=-1)
    sin = jnp.concatenate([-jnp.sin(ang_r), jnp.sin(ang_r), -jnp.sin(ang_c), jnp.sin(ang_c)], axis=-1)
    pad = LANES - rot_dim
    cos = jnp.pad(cos, ((n_ctx, 0), (0, pad)), constant_values=1.0)
    sin = jnp.pad(sin, ((n_ctx, 0), (0, pad)))
    return cos, sin


def _inproj_ab_kernel(x_ref, mod_ref, w_ref, qn_ref, kn_ref, cos_ref, sin_ref,
                      q_ref, k_ref, v_ref, u_ref, *, scale):
    x = x_ref[...]
    h = (x * (1.0 + mod_ref[1:2, :]) + mod_ref[0:1, :]).astype(BF16)
    y = _dot(h, w_ref[...])
    cos = cos_ref[...]
    sin = sin_ref[...]
    rep = N_HEADS // N_KV_HEADS
    for hh in range(N_HEADS):
        qh = _rms(y[:, hh * HEAD_DIM:(hh + 1) * HEAD_DIM], qn_ref[...])
        qh = _rope_lanes(qh, cos, sin, HEAD_DIM // 4) * scale
        q_ref[hh // rep, hh % rep] = qh.astype(BF16)
    k0 = N_HEADS * HEAD_DIM
    v0 = k0 + N_KV_HEADS * HEAD_DIM
    for g in range(N_KV_HEADS):
        kh = _rms(y[:, k0 + g * HEAD_DIM:k0 + (g + 1) * HEAD_DIM], kn_ref[...])
        k_ref[g] = _rope_lanes(kh, cos, sin, HEAD_DIM // 4).astype(BF16)
        v_ref[g] = y[:, v0 + g * HEAD_DIM:v0 + (g + 1) * HEAD_DIM].astype(BF16)
    u_ref[...] = y[:, v0 + N_KV_HEADS * HEAD_DIM:].astype(BF16)


def inproj_ab(x, modtab, w_in, q_norm, k_norm, cos, sin, n_ctx):
    b, t, d = x.shape
    n_in = w_in.shape[1]
    s5_ch = n_in - (N_HEADS + 2 * N_KV_HEADS) * HEAD_DIM
    rep = N_HEADS // N_KV_HEADS
    tq = ROW_TILE
    nct = n_ctx // tq
    seg = lambda i: jnp.minimum(i // nct, 1) if nct > 0 else 1
    kern = functools.partial(_inproj_ab_kernel, scale=HEAD_DIM ** -0.5)
    return pl.pallas_call(
        kern,
        grid=(b, t // tq),
        in_specs=[
            pl.BlockSpec((None, tq, d), lambda bi, i: (bi, i, 0)),
            pl.BlockSpec((None, None, 6, d), lambda bi, i: (bi, seg(i), 0, 0)),
            pl.BlockSpec((d, n_in), lambda bi, i: (0, 0)),
            pl.BlockSpec((1, HEAD_DIM), lambda bi, i: (0, 0)),
            pl.BlockSpec((1, HEAD_DIM), lambda bi, i: (0, 0)),
            pl.BlockSpec((tq, LANES), lambda bi, i: (i, 0)),
            pl.BlockSpec((tq, LANES), lambda bi, i: (i, 0)),
        ],
        out_specs=[
            pl.BlockSpec((None, N_KV_HEADS, rep, tq, HEAD_DIM), lambda bi, i: (bi, 0, 0, i, 0)),
            pl.BlockSpec((None, N_KV_HEADS, tq, HEAD_DIM), lambda bi, i: (bi, 0, i, 0)),
            pl.BlockSpec((None, N_KV_HEADS, tq, HEAD_DIM), lambda bi, i: (bi, 0, i, 0)),
            pl.BlockSpec((None, tq, s5_ch), lambda bi, i: (bi, i, 0)),
        ],
        out_shape=[
            jax.ShapeDtypeStruct((b, N_KV_HEADS, rep, t, HEAD_DIM), BF16),
            jax.ShapeDtypeStruct((b, N_KV_HEADS, t, HEAD_DIM), BF16),
            jax.ShapeDtypeStruct((b, N_KV_HEADS, t, HEAD_DIM), BF16),
            jax.ShapeDtypeStruct((b, t, s5_ch), BF16),
        ],
        compiler_params=_cparams(("parallel", "parallel")),
        name="inproj_ab",
    )(x, modtab, w_in, q_norm.reshape(1, -1), k_norm.reshape(1, -1), cos, sin)


def _inproj_cd_kernel(x_ref, mod_ref, w_ref, qln_ref, kvln_ref, wuq_ref, wukv_ref,
                      cos_ref, sin_ref, cosm_ref, sinm_ref,
                      qm_ref, km_ref, vm_ref, qd_ref, kd_ref, vd_ref, *, scale_c, scale_d, q_lora, kv_lora):
    x = x_ref[...]
    h = (x * (1.0 + mod_ref[1:2, :]) + mod_ref[0:1, :]).astype(BF16)
    y = _dot(h, w_ref[...])
    cos = cos_ref[...]
    sin = sin_ref[...]
    cosm = cosm_ref[...]
    sinm = sinm_ref[...]
    rep = N_HEADS // N_KV_HEADS
    cq = _rms(y[:, :q_lora], qln_ref[...]).astype(BF16)
    ckv = _rms(y[:, q_lora:q_lora + kv_lora], kvln_ref[...]).astype(BF16)
    q = _dot(cq, wuq_ref[...])
    kv = _dot(ckv, wukv_ref[...])
    o = q_lora + kv_lora
    qd0, kd0 = o, o + N_HEADS * HEAD_DIM
    vd0 = kd0 + N_KV_HEADS * HEAD_DIM
    kr0 = vd0 + N_KV_HEADS * HEAD_DIM
    k_rope = _rope_lanes(y[:, kr0:kr0 + LANES], cosm, sinm, C_ROPE // 4)
    for hh in range(N_HEADS):
        qn = q[:, hh * MLA_QK:hh * MLA_QK + HEAD_DIM]
        qr = _rope_lanes(q[:, hh * MLA_QK + HEAD_DIM:(hh + 1) * MLA_QK], cosm, sinm, C_ROPE // 4)
        qm_ref[hh, 0] = (jnp.concatenate([qn, qr], axis=-1) * scale_c).astype(BF16)
        kn = kv[:, hh * HEAD_DIM:(hh + 1) * HEAD_DIM]
        km_ref[hh] = jnp.concatenate([kn, k_rope], axis=-1).astype(BF16)
        vm_ref[hh] = kv[:, (N_HEADS + hh) * HEAD_DIM:(N_HEADS + hh + 1) * HEAD_DIM].astype(BF16)
        qdh = _rope_lanes(y[:, qd0 + hh * HEAD_DIM:qd0 + (hh + 1) * HEAD_DIM], cos, sin, HEAD_DIM // 4)
        qd_ref[hh // rep, hh % rep] = (qdh * scale_d).astype(BF16)
    for g in range(N_KV_HEADS):
        kdh = _rope_lanes(y[:, kd0 + g * HEAD_DIM:kd0 + (g + 1) * HEAD_DIM], cos, sin, HEAD_DIM // 4)
        kd_ref[g] = kdh.astype(BF16)
        vd_ref[g] = y[:, vd0 + g * HEAD_DIM:vd0 + (g + 1) * HEAD_DIM].astype(BF16)


def inproj_cd(x, modtab, w_in, q_ln, kv_ln, w_uq, w_ukv, cos, sin, cosm, sinm, n_ctx):
    b, t, d = x.shape
    n_in = w_in.shape[1]
    q_lora, kv_lora = q_ln.shape[0], kv_ln.shape[0]
    rep = N_HEADS // N_KV_HEADS
    tq = ROW_TILE
    nct = n_ctx // tq
    seg = lambda i: jnp.minimum(i // nct, 1) if nct > 0 else 1
    kern = functools.partial(_inproj_cd_kernel, scale_c=(HEAD_DIM + C_ROPE) ** -0.5,
                             scale_d=HEAD_DIM ** -0.5, q_lora=q_lora, kv_lora=kv_lora)
    full = lambda shape: pl.BlockSpec(shape, lambda bi, i: (0,) * len(shape))
    tab = pl.BlockSpec((tq, LANES), lambda bi, i: (i, 0))
    return pl.pallas_call(
        kern,
        grid=(b, t // tq),
        in_specs=[
            pl.BlockSpec((None, tq, d), lambda bi, i: (bi, i, 0)),
            pl.BlockSpec((None, None, 6, d), lambda bi, i: (bi, seg(i), 0, 0)),
            full((d, n_in)), full((1, q_lora)), full((1, kv_lora)),
            full(w_uq.shape), full(w_ukv.shape), tab, tab, tab, tab,
        ],
        out_specs=[
            pl.BlockSpec((None, N_HEADS, 1, tq, MLA_QK), lambda bi, i: (bi, 0, 0, i, 0)),
            pl.BlockSpec((None, N_HEADS, tq, MLA_QK), lambda bi, i: (bi, 0, i, 0)),
            pl.BlockSpec((None, N_HEADS, tq, HEAD_DIM), lambda bi, i: (bi, 0, i, 0)),
            pl.BlockSpec((None, N_KV_HEADS, rep, tq, HEAD_DIM), lambda bi, i: (bi, 0, 0, i, 0)),
            pl.BlockSpec((None, N_KV_HEADS, tq, HEAD_DIM), lambda bi, i: (bi, 0, i, 0)),
            pl.BlockSpec((None, N_KV_HEADS, tq, HEAD_DIM), lambda bi, i: (bi, 0, i, 0)),
        ],
        out_shape=[
            jax.ShapeDtypeStruct((b, N_HEADS, 1, t, MLA_QK), BF16),
            jax.ShapeDtypeStruct((b, N_HEADS, t, MLA_QK), BF16),
            jax.ShapeDtypeStruct((b, N_HEADS, t, HEAD_DIM), BF16),
            jax.ShapeDtypeStruct((b, N_KV_HEADS, rep, t, HEAD_DIM), BF16),
            jax.ShapeDtypeStruct((b, N_KV_HEADS, t, HEAD_DIM), BF16),
            jax.ShapeDtypeStruct((b, N_KV_HEADS, t, HEAD_DIM), BF16),
        ],
        compiler_params=_cparams(("parallel", "parallel")),
        name="inproj_cd",
    )(x, modtab, w_in, q_ln.reshape(1, -1), kv_ln.reshape(1, -1), w_uq, w_ukv, cos, sin, cosm, sinm)


def _prep_cd_weights(w_in, w_uq, w_ukv):
    d = w_in.shape[0]
    q_lora = w_uq.shape[0]
    kv_lora = w_ukv.shape[0]
    o = q_lora + kv_lora
    k_rope = w_in[:, o:o + C_ROPE]
    rest = w_in[:, o + C_ROPE:]
    w_in_r = jnp.concatenate([w_in[:, :o], rest, k_rope, jnp.zeros((d, LANES - C_ROPE), w_in.dtype)], axis=1)
    uq = w_uq.reshape(q_lora, N_HEADS, HEAD_DIM + C_ROPE)
    uq = jnp.pad(uq, ((0, 0), (0, 0), (0, MLA_QK - HEAD_DIM - C_ROPE))).reshape(q_lora, N_HEADS * MLA_QK)
    ukv = w_ukv.reshape(kv_lora, N_HEADS, 2, HEAD_DIM).transpose(0, 2, 1, 3).reshape(kv_lora, 2 * N_HEADS * HEAD_DIM)
    return w_in_r.astype(BF16), uq.astype(BF16), ukv.astype(BF16)


def _softmax_pv(blocks, sink_col):
    mx = None
    for s, _ in blocks:
        bm = jnp.max(s, axis=-1, keepdims=True)
        mx = bm if mx is None else jnp.maximum(mx, bm)
    if sink_col is not None:
        mx = jnp.maximum(mx, sink_col)
    den = None
    acc = None
    for s, v in blocks:
        p = jnp.exp(s - mx)
        ps = jnp.sum(p, axis=-1, keepdims=True)
        den = ps if den is None else den + ps
        pv = _dot(p.astype(BF16), v)
        acc = pv if acc is None else acc + pv
    if sink_col is not None:
        den = den + jnp.exp(sink_col - mx)
    return acc / den


def _online_softmax_pv(q, k_ref, v_ref, sink_col):
    n_keys = k_ref.shape[0]
    m = sink_col if sink_col is not None else jnp.full((q.shape[0], 1), -jnp.inf, F32)
    den = None
    acc = None
    for c0 in range(0, n_keys, KEY_CHUNK):
        c1 = min(c0 + KEY_CHUNK, n_keys)
        s = _dot_nt(q, k_ref[c0:c1, :])
        m_new = jnp.maximum(m, jnp.max(s, axis=-1, keepdims=True))
        p = jnp.exp(s - m_new)
        ps = jnp.sum(p, axis=-1, keepdims=True)
        pv = _dot(p.astype(BF16), v_ref[c0:c1, :])
        if acc is None:
            den, acc = ps, pv
        else:
            corr = jnp.exp(m - m_new)
            den = den * corr + ps
            acc = acc * corr + pv
        m = m_new
    if sink_col is not None:
        den = den + jnp.exp(sink_col - m)
    return acc / den


def _attn_kernel(sink_ref, q_ref, k_ref, v_ref, o_ref, *, n_ctx, tq, rep, windowed, use_sink):
    g = pl.program_id(1)
    qi = pl.program_id(2)
    t_all = k_ref.shape[0]
    dk = q_ref.shape[-1]
    dv = v_ref.shape[-1]
    q = q_ref[...].reshape(rep * tq, dk)
    nct = n_ctx // tq

    if use_sink:
        row = lax.broadcasted_iota(I32, (rep * tq, 1), 0)
        sink_col = jnp.full((rep * tq, 1), sink_ref[g * rep], F32)
        for r in range(1, rep):
            sink_col = jnp.where(row >= r * tq, sink_ref[g * rep + r], sink_col)
    else:
        sink_col = None

    def emit(o):
        for r in range(rep):
            o_ref[:, r * dv:(r + 1) * dv] = o[r * tq:(r + 1) * tq].astype(o_ref.dtype)

    @pl.when(qi < nct)
    def _():
        kc = k_ref[0:n_ctx, :]
        vc = v_ref[0:n_ctx, :]
        emit(_softmax_pv([(_dot_nt(q, kc), vc)], sink_col))

    @pl.when(qi >= nct)
    def _():
        if not windowed:
            emit(_online_softmax_pv(q, k_ref, v_ref, sink_col))
        else:
            band = tq + 2 * D_WINDOW
            s0 = (qi - nct) * tq
            kstart = jnp.clip(n_ctx + s0 - D_WINDOW, n_ctx, t_all - band)
            kstart = pl.multiple_of(kstart, LANES)
            kb = k_ref[pl.ds(kstart, band), :]
            vb = v_ref[pl.ds(kstart, band), :]
            sb = _dot_nt(q, kb)
            rowq = lax.broadcasted_iota(I32, (rep * tq, band), 0) % tq + s0
            colk = lax.broadcasted_iota(I32, (rep * tq, band), 1) + (kstart - n_ctx)
            sb = jnp.where(jnp.abs(colk - rowq) <= D_WINDOW, sb, NEG_INF)
            kc = k_ref[0:n_ctx, :]
            vc = v_ref[0:n_ctx, :]
            emit(_softmax_pv([(_dot_nt(q, kc), vc), (sb, vb)], sink_col))


def attention(q, k, v, n_ctx, *, windowed=False, sink=None):
    b, g, rep, t, dk = q.shape
    dv = v.shape[-1]
    tq = ROW_TILE
    use_sink = sink is not None
    if sink is None:
        sink = jnp.zeros((g * rep,), F32)
    kern = functools.partial(_attn_kernel, n_ctx=n_ctx, tq=tq, rep=rep, windowed=windowed, use_sink=use_sink)
    return pl.pallas_call(
        kern,
        grid=(b, g, t // tq),
        in_specs=[
            pl.BlockSpec(memory_space=pltpu.SMEM),
            pl.BlockSpec((None, None, rep, tq, dk), lambda bi, gi, i: (bi, gi, 0, i, 0)),
            pl.BlockSpec((None, None, t, dk), lambda bi, gi, i: (bi, gi, 0, 0)),
            pl.BlockSpec((None, None, t, dv), lambda bi, gi, i: (bi, gi, 0, 0)),
        ],
        out_specs=pl.BlockSpec((None, tq, rep * dv), lambda bi, gi, i: (bi, i, gi)),
        out_shape=jax.ShapeDtypeStruct((b, t, g * rep * dv), BF16),
        compiler_params=_cparams(("parallel", "parallel", "arbitrary")),
        name="attention_win" if windowed else "attention",
    )(sink.astype(F32), q, k, v)


def _s5_tables(lam_re, lam_im, log_dt, b_re, b_im, c_re, c_im, d_skip):
    n_groups, n_state = lam_re.shape[1:]
    L, gs = S5_CHUNK, S5_GROUP
    lam = lax.complex(lam_re.astype(F32), lam_im.astype(F32))
    dt = jnp.exp(log_dt.astype(F32))[..., None]
    lam_dt = lam * dt
    lam_bar = jnp.exp(lam_dt)
    b_bar = ((lam_bar - 1.0) / lam)[..., None] * lax.complex(b_re.astype(F32), b_im.astype(F32))
    c_mat = lax.complex(c_re.astype(F32), c_im.astype(F32))
    pw = jnp.exp(lam_dt[None] * jnp.arange(L + 1, dtype=F32)[:, None, None, None])
    kker = jnp.einsum("dgop,tdgp,dgpi->dtgoi", c_mat, pw[:L], b_bar).real
    s_in = jnp.arange(L)[:, None]
    s_out = jnp.arange(L)[None, :]
    tau_f = s_out - s_in
    tau_r = s_in - s_out
    kf = jnp.where((tau_f >= 0)[:, :, None, None, None], kker[0][jnp.clip(tau_f, 0, L - 1)], 0.0)
    kr = jnp.where((tau_r >= 0)[:, :, None, None, None], kker[1][jnp.clip(tau_r, 0, L - 1)], 0.0)
    kt = kf + kr
    gp = S5_PACK
    n_packs = n_groups // gp
    eye = jnp.eye(gp, dtype=F32)
    t6 = kt.transpose(2, 0, 4, 1, 3).reshape(n_packs, gp, L, gs, L, gs)
    d6 = d_skip.astype(F32).reshape(n_packs, gp, 1, gs, 1, 1) * (
        jnp.eye(L, dtype=F32)[None, None, :, None, :, None] * jnp.eye(gs, dtype=F32)[None, None, None, :, None, :])
    tmat = jnp.einsum("pgsctd,gh->psgcthd", t6 + d6, eye).reshape(n_packs, L * LANES, L * LANES)
    steps = jnp.arange(L, dtype=F32)[:, None, None]
    wf = jnp.exp(lam_dt[0][None] * (L - 1 - steps))[:, :, :, None] * b_bar[0][None]
    wr = jnp.exp(lam_dt[1][None] * steps)[:, :, :, None] * b_bar[1][None]
    def w_pack(w):
        w2 = jnp.stack([w.real, w.imag], axis=0).reshape(2, L, n_packs, gp, n_state, gs)
        return jnp.einsum("bspgqc,gh->psgcbhq", w2, eye).reshape(n_packs, L * LANES, 2 * gp * n_state)
    vf = c_mat[0][None] * jnp.exp(lam_dt[0][None] * (steps + 1))[:, :, None, :]
    vr = c_mat[1][None] * jnp.exp(lam_dt[1][None] * (L - steps))[:, :, None, :]
    def v_pack(vv):
        v2 = jnp.stack([vv.real, -vv.imag], axis=0).reshape(2, L, n_packs, gp, gs, n_state)
        return jnp.einsum("bspgcq,gh->pbgqshc", v2, eye).reshape(n_packs, 2 * gp * n_state, L * LANES)
    def lam_pack(l):
        return jnp.stack([l.real, l.imag], axis=0).reshape(2, n_packs, gp * n_state).transpose(1, 0, 2)
    lam_l = pw[L]
    return (tmat.astype(BF16),
            (w_pack(wf).astype(BF16), v_pack(vf).astype(BF16), lam_pack(lam_l[0])),
            (w_pack(wr).astype(BF16), v_pack(vr).astype(BF16), lam_pack(lam_l[1])))


def _s5_pass_kernel(*refs, reverse):
    if reverse:
        u_ref, yin_ref, w_ref, v_ref, lam_ref, y_ref, s_scr, h_scr = refs
    else:
        u_ref, t_ref, w_ref, v_ref, lam_ref, y_ref, s_scr, h_scr = refs

    @pl.when(pl.program_id(1) == 0)
    def _():
        h_scr[...] = jnp.zeros_like(h_scr)

    nchs, L, b, _ = u_ref.shape
    rows = nchs * b
    half = h_scr.shape[1] // 2
    xg = jnp.concatenate([u_ref[:, s].reshape(rows, LANES) for s in range(L)], axis=-1)
    s_scr[...] = _dot(xg, w_ref[...])
    lr = jnp.broadcast_to(lam_ref[0:1, :], (b, half))
    li = jnp.broadcast_to(lam_ref[1:2, :], (b, half))

    def step(k, carry):
        hr, hi = carry
        j = nchs - 1 - k if reverse else k
        r0 = pl.multiple_of(j * b, b)
        sr = s_scr[pl.ds(r0, b), 0:half]
        si = s_scr[pl.ds(r0, b), half:]
        s_scr[pl.ds(r0, b), 0:half] = hr
        s_scr[pl.ds(r0, b), half:] = hi
        return lr * hr - li * hi + sr, lr * hi + li * hr + si

    hr, hi = lax.fori_loop(0, nchs, step, (h_scr[:, 0:half], h_scr[:, half:]))
    h_scr[:, 0:half] = hr
    h_scr[:, half:] = hi
    y = _dot(s_scr[...].astype(BF16), v_ref[...])
    if not reverse:
        y = y + _dot(xg, t_ref[...])
    for s in range(L):
        blk = y[:, s * LANES:(s + 1) * LANES].reshape(nchs, b, LANES)
        if reverse:
            blk = blk + yin_ref[:, s]
        y_ref[:, s] = blk


def s5_apply(u, tables, n_ctx):
    tmat, fwd, rev = tables
    b, t, ch = u.shape
    L = S5_CHUNK
    n_packs = ch // LANES
    seg = ROW_TILE
    nseg, nct, nchs = t // seg, n_ctx // seg, seg // L
    width = L * LANES
    n_state2 = fwd[0].shape[-1]
    ut = u.transpose(1, 0, 2).reshape(t // L, L, b, ch)
    blk = pl.BlockSpec((nchs, L, b, LANES), lambda p, i: (i, 0, 0, p))
    rseg = lambda i: jnp.where(i < nct, nct - 1 - i, nseg - 1 - (i - nct))
    rblk = pl.BlockSpec((nchs, L, b, LANES), lambda p, i: (rseg(i), 0, 0, p))
    tab = lambda shape: pl.BlockSpec((None,) + shape, lambda p, i: (p, 0, 0))
    scratch = [pltpu.VMEM((nchs * b, n_state2), F32), pltpu.VMEM((b, n_state2), F32)]
    y_shape = jax.ShapeDtypeStruct((t // L, L, b, ch), F32)
    y_f = pl.pallas_call(
        functools.partial(_s5_pass_kernel, reverse=False),
        grid=(n_packs, nseg),
        in_specs=[blk, tab((width, width)), tab((width, n_state2)), tab((n_state2, width)), tab((2, n_state2 // 2))],
        out_specs=blk,
        out_shape=y_shape,
        scratch_shapes=scratch,
        compiler_params=_cparams(("parallel", "arbitrary")),
        name="s5_forward",
    )(ut, tmat, *fwd)
    y = pl.pallas_call(
        functools.partial(_s5_pass_kernel, reverse=True),
        grid=(n_packs, nseg),
        in_specs=[rblk, rblk, tab((width, n_state2)), tab((n_state2, width)), tab((2, n_state2 // 2))],
        out_specs=rblk,
        out_shape=y_shape,
        scratch_shapes=scratch,
        input_output_aliases={1: 0},
        compiler_params=_cparams(("parallel", "arbitrary")),
        name="s5_reverse",
    )(ut, y_f, *rev)
    return y.reshape(t, b, ch).transpose(1, 0, 2)


def _gelu_tanh(x):
    return 0.5 * x * (1.0 + jnp.tanh(math.sqrt(2.0 / math.pi) * (x + 0.044715 * (x * x * x))))


def _post_kernel(*refs, alpha, glu, n_exp):
    if glu:
        (x_ref, a_ref, s_ref, wg_ref, bg_ref, wm_ref, mod_ref, g_ref, b_ref, wrh_ref, wrl_ref, br_ref,
         x1_ref, h2_ref, idx_ref, gate_ref, rank_ref, cnt_ref, cnt_scr) = refs
    else:
        (x_ref, a_ref, s_ref, wm_ref, mod_ref, g_ref, b_ref, wrh_ref, wrl_ref, br_ref,
         x1_ref, h2_ref, idx_ref, gate_ref, rank_ref, cnt_ref, cnt_scr) = refs
    first = (pl.program_id(0) == 0) & (pl.program_id(1) == 0)

    @pl.when(first)
    def _():
        cnt_scr[...] = jnp.zeros_like(cnt_scr)

    tq, d = x_ref.shape
    half = a_ref.shape[1]
    if glu:
        z = _gelu_tanh(s_ref[...])
        gate = jax.nn.sigmoid(_dot(z.astype(BF16), wg_ref[...]) + bg_ref[...])
        second = (z * gate).astype(BF16)
    else:
        second = s_ref[...]
    mix = _dot(a_ref[...], wm_ref[0:half, :]) + _dot(second, wm_ref[half:, :])
    x1 = _layer_norm(alpha * x_ref[...] + mod_ref[2:3, :] * mix, g_ref[...], b_ref[...])
    x1_ref[...] = x1
    h2 = x1 * (1.0 + mod_ref[4:5, :]) + mod_ref[3:4, :]
    _store_token_tiles(h2_ref, h2)

    h_hi, h_lo = _split_bf16(h2)
    logits = _dot_nt(wrh_ref[...], h_hi) + _dot_nt(wrh_ref[...], h_lo) + _dot_nt(wrl_ref[...], h_hi) + br_ref[...]
    eidx = lax.broadcasted_iota(I32, (n_exp, tq), 0)
    work = logits
    tops, sels = [], []
    for k in range(TOP_K):
        m = jnp.max(work, axis=0, keepdims=True)
        ik = jnp.min(jnp.where(work == m, eidx, n_exp), axis=0, keepdims=True)
        sel = eidx == ik
        work = jnp.where(sel, -jnp.inf, work)
        tops.append(m)
        sels.append(sel)
        idx_ref[k:k + 1, :] = ik
    exps = [jnp.exp(tk - tops[0]) for tk in tops]
    den = exps[0] + exps[1] + exps[2] + exps[3]
    for k in range(TOP_K):
        gate_ref[k:k + 1, :] = exps[k] / den
    onehot = jnp.zeros((n_exp, tq), F32)
    for sel in sels:
        onehot = onehot + sel.astype(F32)
    tri = (lax.broadcasted_iota(I32, (tq, tq), 0) < lax.broadcasted_iota(I32, (tq, tq), 1)).astype(BF16)
    before = _dot(onehot.astype(BF16), tri) + cnt_scr[...]
    for k in range(TOP_K):
        rk = jnp.sum(jnp.where(sels[k], before, 0.0), axis=0, keepdims=True)
        rank_ref[k:k + 1, :] = rk.astype(I32)
    cnt_scr[...] += jnp.sum(onehot, axis=1, keepdims=True)
    cnt_ref[...] = jnp.broadcast_to(cnt_scr[...], cnt_ref.shape).astype(I32)


def post_mixer(x, a, s, modtab, w_mix, ln_g, ln_b, w_router, b_router, n_ctx, alpha, glu_w=None, glu_b=None):
    b, t, d = x.shape
    half = a.shape[-1]
    n_exp = w_router.shape[1]
    tq = ROW_TILE
    nct = n_ctx // tq
    seg = lambda i: jnp.minimum(i // nct, 1) if nct > 0 else 1
    glu = glu_w is not None
    wr_hi, wr_lo = _split_bf16(w_router.T.astype(F32))
    full = lambda shape: pl.BlockSpec(shape, lambda bi, i: (0,) * len(shape))
    tok = lambda width: pl.BlockSpec((None, tq, width), lambda bi, i: (bi, i, 0))
    in_specs = [tok(d), tok(half), tok(half)]
    args = [x, a, s]
    if glu:
        in_specs += [full((half, half)), full((1, half))]
        args += [glu_w.astype(BF16), glu_b.reshape(1, half)]
    in_specs += [full((d, d)), pl.BlockSpec((None, None, 6, d), lambda bi, i: (bi, seg(i), 0, 0)),
                 full((1, d)), full((1, d)), full((n_exp, d)), full((n_exp, d)), full((n_exp, 1))]
    args += [w_mix.astype(BF16), modtab, ln_g.reshape(1, d), ln_b.reshape(1, d), wr_hi, wr_lo,
             b_router.reshape(n_exp, 1)]
    nt = t // tq
    lane_out = lambda dt: (pl.BlockSpec((TOP_K, tq), lambda bi, i: (0, bi * nt + i)),
                           jax.ShapeDtypeStruct((TOP_K, b * t), dt))
    outs = [
        (tok(d), jax.ShapeDtypeStruct((b, t, d), F32)),
        (pl.BlockSpec((tq * SUBLANES, LANES), lambda bi, i: (bi * nt + i, 0)),
         jax.ShapeDtypeStruct((b * t * SUBLANES, LANES), F32)),
        lane_out(I32), lane_out(F32), lane_out(I32),
        (pl.BlockSpec((n_exp, LANES), lambda bi, i: (0, 0)), jax.ShapeDtypeStruct((n_exp, LANES), I32)),
    ]
    kern = functools.partial(_post_kernel, alpha=alpha, glu=glu, n_exp=n_exp)
    return pl.pallas_call(
        kern,
        grid=(b, nt),
        in_specs=in_specs,
        out_specs=[o[0] for o in outs],
        out_shape=[o[1] for o in outs],
        scratch_shapes=[pltpu.VMEM((n_exp, 1), F32)],
        compiler_params=_cparams(("arbitrary", "arbitrary")),
        name="post_mixer_glu" if glu else "post_mixer",
    )(*args)


def _route_kernel(start_ref, idx_ref, rank_ref, pos_ref, *, n_exp, tq):
    eidx = lax.broadcasted_iota(I32, (n_exp, tq), 0)
    for k in range(TOP_K):
        sel = eidx == idx_ref[k:k + 1, :]
        base = jnp.sum(jnp.where(sel, start_ref[...], 0.0), axis=0, keepdims=True)
        pos_ref[:, k * tq:(k + 1) * tq] = base.astype(I32) + rank_ref[k:k + 1, :]


def route_positions(group_start, idx, rank):
    n_exp = group_start.shape[0]
    n_tok = idx.shape[1]
    tq = ROW_TILE
    return pl.pallas_call(
        functools.partial(_route_kernel, n_exp=n_exp, tq=tq),
        grid=(n_tok // tq,),
        in_specs=[
            pl.BlockSpec((n_exp, 1), lambda i: (0, 0)),
            pl.BlockSpec((TOP_K, tq), lambda i: (0, i)),
            pl.BlockSpec((TOP_K, tq), lambda i: (0, i)),
        ],
        out_specs=pl.BlockSpec((None, 1, TOP_K * tq), lambda i: (i, 0, 0)),
        out_shape=jax.ShapeDtypeStruct((n_tok // tq, 1, TOP_K * tq), I32),
        compiler_params=_cparams(("parallel",)),
        name="route_positions",
    )(group_start.astype(F32).reshape(n_exp, 1), idx, rank)


def _dispatch_kernel(pos_hbm, h_ref, xs_hbm, idx_smem, idx_sem, row_sem, *, tq, n_tiles, tile_rows):
    i = pl.program_id(0)
    slot = i % 2

    def idx_copy(tile, sl):
        return pltpu.make_async_copy(pos_hbm.at[tile], idx_smem.at[sl], idx_sem.at[sl])

    @pl.when(i == 0)
    def _():
        idx_copy(0, 0).start()

    idx_copy(i, slot).wait()

    @pl.when(i + 1 < n_tiles)
    def _():
        idx_copy(i + 1, 1 - slot).start()

    for k in range(TOP_K):
        for r in range(tq):
            dst = pl.multiple_of(idx_smem[slot, 0, k * tq + r] * tile_rows, tile_rows)
            pltpu.make_async_copy(h_ref.at[pl.ds(r * tile_rows, tile_rows), :],
                                  xs_hbm.at[pl.ds(dst, tile_rows), :], row_sem.at[0]).start(priority=r % 2)
    for k in range(TOP_K):
        pltpu.make_async_copy(h_ref, xs_hbm.at[pl.ds(0, tq * tile_rows), :], row_sem.at[0]).wait()


def moe_dispatch(pos_tiles, h_tiles, tile_rows):
    n_tiles = pos_tiles.shape[0]
    tq = h_tiles.shape[0] // tile_rows // n_tiles
    return pl.pallas_call(
        functools.partial(_dispatch_kernel, tq=tq, n_tiles=n_tiles, tile_rows=tile_rows),
        grid=(n_tiles,),
        in_specs=[pl.BlockSpec(memory_space=pl.ANY),
                  pl.BlockSpec((tq * tile_rows, LANES), lambda i: (i, 0))],
        out_specs=pl.BlockSpec(memory_space=pl.ANY),
        out_shape=jax.ShapeDtypeStruct((TOP_K * h_tiles.shape[0], LANES), F32),
        scratch_shapes=[
            pltpu.SMEM((2, 1, TOP_K * tq), I32),
            pltpu.SemaphoreType.DMA((2,)),
            pltpu.SemaphoreType.DMA((1,)),
        ],
        compiler_params=_cparams(("arbitrary",)),
        name="moe_dispatch",
    )(pos_tiles, h_tiles)


def _moe_kernel(e_ref, j_ref, lo_ref, hi_ref, first_ref, new_ref, x_ref, wi_ref, bi_ref, wo_ref, bo_ref,
                y_ref, wi_b, wo_b, *, n_sub):
    i = pl.program_id(0)
    lo = lo_ref[i]
    hi = hi_ref[i]

    @pl.when(new_ref[i] == 1)
    def _():
        wi_b[...] = wi_ref[...].astype(BF16)
        wo_b[...] = wo_ref[...].astype(BF16)

    tm = x_ref.shape[0] // n_sub

    @pl.when(hi > lo)
    def _():
        x = _load_token_tiles(x_ref, 0, tm, n_sub).astype(BF16)
        f = wo_b.shape[0]
        y = None
        for c0 in range(0, f, FF_CHUNK):
            c1 = c0 + FF_CHUNK
            glu = jnp.minimum(_dot(x, wi_b[:, c0:c1]) + bi_ref[:, c0:c1], SWIGLU_LIMIT)
            lin = jnp.clip(_dot(x, wi_b[:, f + c0:f + c1]) + bi_ref[:, f + c0:f + c1], -SWIGLU_LIMIT, SWIGLU_LIMIT)
            act = glu * jax.nn.sigmoid(SWIGLU_ALPHA * glu) * (lin + 1.0)
            part = _dot(act.astype(BF16), wo_b[c0:c1, :])
            y = part if y is None else y + part
        y = y + bo_ref[...]
        row = lax.broadcasted_iota(I32, (tm, 1), 0)
        mine = (row >= lo) & (row < hi)

        @pl.when(first_ref[i] == 1)
        def _():
            _store_token_tiles(y_ref, jnp.where(mine, y, 0.0))

        @pl.when(first_ref[i] == 0)
        def _():
            _store_token_tiles(y_ref, jnp.where(mine, y, _load_token_tiles(y_ref, 0, tm, n_sub)))


def moe_experts(xs, items, w_in, b_in, w_out, b_out, layer):
    _, n_exp, d, f2 = w_in.shape
    n_sub = d // LANES
    n_items = items[0].shape[0]
    tm = MOE_BLOCK
    wmap = lambda i, e, j, lo, hi, fi, nw: (layer, e[i], 0, 0)
    xmap = lambda i, e, j, lo, hi, fi, nw: (j[i], 0)
    grid_spec = pltpu.PrefetchScalarGridSpec(
        num_scalar_prefetch=6,
        grid=(n_items,),
        in_specs=[
            pl.BlockSpec((tm * n_sub, LANES), xmap),
            pl.BlockSpec((None, None, d, f2), wmap),
            pl.BlockSpec((None, None, 1, f2), wmap),
            pl.BlockSpec((None, None, f2 // 2, d), wmap),
            pl.BlockSpec((None, None, 1, d), wmap),
        ],
        out_specs=pl.BlockSpec((tm * n_sub, LANES), xmap),
        scratch_shapes=[pltpu.VMEM((d, f2), BF16), pltpu.VMEM((f2 // 2, d), BF16)],
    )
    return pl.pallas_call(
        functools.partial(_moe_kernel, n_sub=n_sub),
        grid_spec=grid_spec,
        out_shape=jax.ShapeDtypeStruct(xs.shape, F32),
        compiler_params=_cparams(("arbitrary",)),
        name="moe_experts",
    )(*items, xs, w_in, b_in.reshape(b_in.shape[0], n_exp, 1, f2), w_out, b_out.reshape(b_out.shape[0], n_exp, 1, d))


def _combine_kernel(pos_hbm, x_ref, g_ref, mod_ref, lg_ref, lb_ref, ys_hbm, o_ref,
                    idx_smem, ybuf, idx_sem, row_sem, *, alpha, tq, n_tiles):
    i = pl.program_id(0)
    slot = i % 2
    d = x_ref.shape[1]

    def idx_copy(tile, sl):
        return pltpu.make_async_copy(pos_hbm.at[tile], idx_smem.at[sl], idx_sem.at[sl])

    n_sub = d // LANES

    def gather_start(sl):
        for r in range(TOP_K * tq):
            src = pl.multiple_of(idx_smem[sl, 0, r] * n_sub, n_sub)
            pltpu.make_async_copy(ys_hbm.at[pl.ds(src, n_sub), :], ybuf.at[sl, pl.ds(r * n_sub, n_sub), :],
                                  row_sem.at[sl]).start(priority=r % 2)

    def rows_wait(sl):
        pltpu.make_async_copy(ys_hbm.at[pl.ds(0, TOP_K * tq * n_sub), :], ybuf.at[sl], row_sem.at[sl]).wait()

    @pl.when(i == 0)
    def _():
        idx_copy(0, 0).start()
        idx_copy(0, 0).wait()
        gather_start(0)

        @pl.when(n_tiles > 1)
        def _():
            idx_copy(1, 1).start()

    @pl.when(i + 1 < n_tiles)
    def _():
        idx_copy(i + 1, 1 - slot).wait()
        gather_start(1 - slot)

    rows_wait(slot)

    @pl.when(i + 2 < n_tiles)
    def _():
        idx_copy(i + 2, slot).start()

    yb = ybuf.at[slot]
    gates = g_ref[...]
    y = None
    for k in range(TOP_K):
        term = gates[:, k:k + 1] * _load_token_tiles(yb, k * tq * n_sub, tq, n_sub)
        y = term if y is None else y + term
    o_ref[...] = _layer_norm(alpha * x_ref[...] + mod_ref[5:6, :] * y, lg_ref[...], lb_ref[...])


def moe_combine(x1, pos_tiles, gates_tok, modtab, ln_g, ln_b, ys, n_ctx, alpha, lat_only=False):
    b, t, d = x1.shape
    tq = ROW_TILE
    nt = t // tq
    n_tiles = b * nt
    nct = n_ctx // tq
    seg = lambda i: jnp.minimum((i % nt) // nct, 1) if nct > 0 else 1
    kern = functools.partial(_combine_kernel, alpha=alpha, tq=tq, n_tiles=n_tiles)
    if lat_only:
        nlt = nt - nct
        out_map = lambda i: ((i // nt) * nlt + jnp.maximum(i % nt - nct, 0), 0)
        out_rows = b * nlt * tq
    else:
        out_map = lambda i: (i, 0)
        out_rows = b * t
    out = pl.pallas_call(
        kern,
        grid=(n_tiles,),
        in_specs=[
            pl.BlockSpec(memory_space=pl.ANY),
            pl.BlockSpec((tq, d), lambda i: (i, 0)),
            pl.BlockSpec((tq, TOP_K), lambda i: (i, 0)),
            pl.BlockSpec((None, None, 6, d), lambda i: (i // nt, seg(i), 0, 0)),
            pl.BlockSpec((1, d), lambda i: (0, 0)),
            pl.BlockSpec((1, d), lambda i: (0, 0)),
            pl.BlockSpec(memory_space=pl.ANY),
        ],
        out_specs=pl.BlockSpec((tq, d), out_map),
        out_shape=jax.ShapeDtypeStruct((out_rows, d), F32),
        scratch_shapes=[
            pltpu.SMEM((2, 1, TOP_K * tq), I32),
            pltpu.VMEM((2, TOP_K * tq * (d // LANES), LANES), F32),
            pltpu.SemaphoreType.DMA((2,)),
            pltpu.SemaphoreType.DMA((2,)),
        ],
        compiler_params=_cparams(("arbitrary",)),
        name="moe_combine",
    )(pos_tiles, x1.reshape(b * t, d), gates_tok, modtab, ln_g.reshape(1, d), ln_b.reshape(1, d), ys)
    return out.reshape(b, out_rows // b, d)


def _group_items(counts, n_slots, tm):
    n_exp = counts.shape[0]
    le = (jnp.arange(n_exp)[:, None] <= jnp.arange(n_exp)[None, :]).astype(I32)
    g_end = counts @ le
    g_start = g_end - counts
    first_blk = g_start // tm
    n_blk = jnp.where(counts > 0, (g_end - 1) // tm - first_blk + 1, 0)
    i_end = n_blk @ le
    i_start = i_end - n_blk
    n_items = n_slots // tm + n_exp - 1
    it = jnp.arange(n_items, dtype=I32)
    valid = it < i_end[-1]
    e = jnp.minimum(jnp.sum((i_end[None, :] <= it[:, None]).astype(I32), axis=1), n_exp - 1)
    onehot = (e[:, None] == jnp.arange(n_exp)[None, :]).astype(I32)
    pick = lambda tbl: onehot @ tbl
    blk = pick(first_blk) + it - pick(i_start)
    lo = jnp.clip(pick(g_start) - blk * tm, 0, tm)
    hi = jnp.clip(pick(g_end) - blk * tm, 0, tm)
    last = jnp.maximum(i_end[-1] - 1, 0)
    e_last = jnp.sum(jnp.where(it == last, e, 0))
    e = jnp.where(valid, e, e_last)
    blk = jnp.where(valid, blk, n_slots // tm - 1)
    lo = jnp.where(valid, lo, 0)
    hi = jnp.where(valid, hi, 0)
    prev = lambda a: jnp.concatenate([jnp.full((1,), -1, I32), a[:-1]])
    first = (blk != prev(blk)).astype(I32)
    new_e = (e != prev(e)).astype(I32)
    return g_start, tuple(a.astype(I32) for a in (e, blk, lo, hi, first, new_e))


def kernel(x, c, ctx, c_ctx, w_mod, b_mod, ln_g, ln_b, w_mix_out, w_router, b_router, w_exp_in, b_exp_in, w_exp_out, b_exp_out, w_in_ab, a_q_norm, a_k_norm, s5_lam_re, s5_lam_im, s5_log_dt, s5_b_re, s5_b_im, s5_c_re, s5_c_im, s5_d, w_glu, b_glu, w_in_cd, c_q_norm, c_kv_norm, w_uq, w_ukv, d_sink):
    batch, n_lat, d = x.shape
    n_ctx = ctx.shape[1]
    depth = w_mod.shape[0]
    alpha = (2.0 * depth) ** 0.25
    t = n_ctx + n_lat
    n_tok = batch * t

    xs = jnp.concatenate([ctx, x], axis=1)
    n_rows = -(-(batch + 1) // SUBLANES) * SUBLANES
    cvec = jnp.zeros((n_rows, d), F32).at[:batch].set(c).at[batch].set(c_ctx)
    mods = modulation(cvec, w_mod, b_mod).reshape(depth, n_rows, 6, d)
    modtab = jnp.stack([jnp.broadcast_to(mods[:, batch:batch + 1], (depth, batch, 6, d)), mods[:, :batch]],
                       axis=2)

    cos_h, sin_h = _rope_tables(n_ctx, n_lat, HEAD_DIM)
    cos_m, sin_m = _rope_tables(n_ctx, n_lat, C_ROPE)

    for layer in range(depth):
        i = layer // 2
        mt = modtab[layer]
        if layer % 2 == 0:
            q, k, v, u = inproj_ab(xs, mt, w_in_ab[i].astype(BF16), a_q_norm[i], a_k_norm[i], cos_h, sin_h, n_ctx)
            att = attention(q, k, v, n_ctx)
            tables = _s5_tables(s5_lam_re[i], s5_lam_im[i], s5_log_dt[i], s5_b_re[i], s5_b_im[i],
                                s5_c_re[i], s5_c_im[i], s5_d[i])
            second = s5_apply(u, tables, n_ctx)
            glu_w, glu_b = w_glu[i], b_glu[i]
        else:
            w_in_r, uq_r, ukv_r = _prep_cd_weights(w_in_cd[i], w_uq[i], w_ukv[i])
            qm, km, vm, qd, kd, vd = inproj_cd(xs, mt, w_in_r, c_q_norm[i], c_kv_norm[i], uq_r, ukv_r,
                                               cos_h, sin_h, cos_m, sin_m, n_ctx)
            att = attention(qm, km, vm, n_ctx)
            second = attention(qd, kd, vd, n_ctx, windowed=True, sink=d_sink[i])
            glu_w = glu_b = None
        x1, h2, idx, gates, rank, counts = post_mixer(
            xs, att, second, mt, w_mix_out[layer], ln_g[layer, 0], ln_b[layer, 0],
            w_router[layer], b_router[layer], n_ctx, alpha, glu_w, glu_b)
        g_start, items = _group_items(counts[:, 0], TOP_K * n_tok, MOE_BLOCK)
        pos_tiles = route_positions(g_start, idx, rank)
        sorted_rows = moe_dispatch(pos_tiles, h2, d // LANES)
        ys = moe_experts(sorted_rows, items, w_exp_in, b_exp_in, w_exp_out, b_exp_out, layer)
        xs = moe_combine(x1, pos_tiles, gates.T, mt, ln_g[layer, 1], ln_b[layer, 1], ys, n_ctx, alpha,
                         lat_only=layer == depth - 1)
    return xs
```

```python
import functools
import math

import jax
import jax.numpy as jnp
from jax import lax
from jax.experimental import pallas as pl
from jax.experimental.pallas import tpu as pltpu

F32 = jnp.float32
BF16 = jnp.bfloat16
I32 = jnp.int32

GRID_W = 64
ROPE_THETA = 10000.0
HEAD_DIM = 128
N_HEADS = 4
N_KV_HEADS = 2
C_ROPE = 64
MLA_QK = 256
D_WINDOW = 128
S5_GROUP = 16
S5_STATE = 64
S5_CHUNK = 8
S5_PACK = 8
TOP_K = 4
SWIGLU_LIMIT = 7.0
SWIGLU_ALPHA = 1.702
NEG_INF = -1e30
LN_EPS = 1e-5
RMS_EPS = 1e-6

LANES = 128
SUBLANES = 8
VMEM_LIMIT_BYTES = 56 * 1024 * 1024

ROW_TILE = 256
MOE_BLOCK = 512
MOE_CHAINS = 2
KEY_CHUNK = 768
ATTN_HEADS_PER_STEP = 2


def _cparams(sem):
    return pltpu.CompilerParams(dimension_semantics=sem, vmem_limit_bytes=VMEM_LIMIT_BYTES)


def _split_bf16(a):
    hi = a.astype(BF16)
    lo = (a - hi.astype(F32)).astype(BF16)
    return hi, lo


def _dot(a, b):
    return jnp.dot(a, b, preferred_element_type=F32)


def _dot_nt(a, b):
    return lax.dot_general(a, b, (((1,), (1,)), ((), ())), preferred_element_type=F32)


def _store_token_tiles(ref, val):
    rows, d = val.shape
    n = d // LANES
    for s in range(n):
        ref[pl.ds(s, rows, stride=n), :] = val[:, s * LANES:(s + 1) * LANES]


def _load_token_tiles(ref, first_row, rows, n):
    return jnp.concatenate([ref[pl.ds(first_row + s, rows, stride=n), :] for s in range(n)], axis=-1)


def _layer_norm(x, g, b):
    mu = jnp.mean(x, axis=-1, keepdims=True)
    xc = x - mu
    var = jnp.mean(xc * xc, axis=-1, keepdims=True)
    return xc * lax.rsqrt(var + LN_EPS) * g + b


def _rms(x, g):
    return x * lax.rsqrt(jnp.mean(x * x, axis=-1, keepdims=True) + RMS_EPS) * g


def _mod_kernel(c_ref, w_ref, b_ref, o_ref):
    c = c_ref[...]
    s = c * jax.nn.sigmoid(c)
    s_hi, s_lo = _split_bf16(s)
    w_hi, w_lo = _split_bf16(w_ref[...])
    o_ref[...] = _dot(s_hi, w_hi) + _dot(s_lo, w_hi) + _dot(s_hi, w_lo) + b_ref[...]


def modulation(cvec, w_mod, b_mod):
    n_layers, d, d6 = w_mod.shape
    r = cvec.shape[0]
    tn = 1536
    return pl.pallas_call(
        _mod_kernel,
        grid=(n_layers, d6 // tn),
        in_specs=[
            pl.BlockSpec((r, d), lambda l, j: (0, 0)),
            pl.BlockSpec((None, d, tn), lambda l, j: (l, 0, j)),
            pl.BlockSpec((None, 1, tn), lambda l, j: (l, 0, j)),
        ],
        out_specs=pl.BlockSpec((None, r, tn), lambda l, j: (l, 0, j)),
        out_shape=jax.ShapeDtypeStruct((n_layers, r, d6), F32),
        compiler_params=_cparams(("arbitrary", "arbitrary")),
        name="modulation",
    )(cvec, w_mod, b_mod.reshape(n_layers, 1, d6))


def _rope_lanes(t, cos, sin, quarter):
    lane = lax.broadcasted_iota(I32, t.shape, 1)
    first = (lane % (2 * quarter)) < quarter
    partner = jnp.where(first, pltpu.roll(t, LANES - quarter, 1), pltpu.roll(t, quarter, 1))
    return t * cos + partner * sin


def _rope_tables(n_ctx, n_lat, rot_dim):
    quarter = rot_dim // 4
    pos = jnp.arange(n_lat, dtype=F32)
    row = jnp.floor(pos / GRID_W)
    col = pos - row * GRID_W
    inv_freq = ROPE_THETA ** (-jnp.arange(quarter, dtype=F32) / quarter)
    ang_r = row[:, None] * inv_freq
    ang_c = col[:, None] * inv_freq
    cos = jnp.concatenate([jnp.cos(ang_r)] * 2 + [jnp.cos(ang_c)] * 2, axis=-1)
    sin = jnp.concatenate([-jnp.sin(ang_r), jnp.sin(ang_r), -jnp.sin(ang_c), jnp.sin(ang_c)], axis=-1)
    pad = LANES - rot_dim
    cos = jnp.pad(cos, ((n_ctx, 0), (0, pad)), constant_values=1.0)
    sin = jnp.pad(sin, ((n_ctx, 0), (0, pad)))
    return cos, sin


def _inproj_ab_kernel(x_ref, mod_ref, w_ref, qn_ref, kn_ref, cos_ref, sin_ref,
                      q_ref, k_ref, v_ref, u_ref, *, scale):
    x = x_ref[...]
    h = (x * (1.0 + mod_ref[1:2, :]) + mod_ref[0:1, :]).astype(BF16)
    y = _dot(h, w_ref[...])
    cos = cos_ref[...]
    sin = sin_ref[...]
    rep = N_HEADS // N_KV_HEADS
    for hh in range(N_HEADS):
        qh = _rms(y[:, hh * HEAD_DIM:(hh + 1) * HEAD_DIM], qn_ref[...])
        qh = _rope_lanes(qh, cos, sin, HEAD_DIM // 4) * scale
        q_ref[hh // rep, hh % rep] = qh.astype(BF16)
    k0 = N_HEADS * HEAD_DIM
    v0 = k0 + N_KV_HEADS * HEAD_DIM
    for g in range(N_KV_HEADS):
        kh = _rms(y[:, k0 + g * HEAD_DIM:k0 + (g + 1) * HEAD_DIM], kn_ref[...])
        k_ref[g] = _rope_lanes(kh, cos, sin, HEAD_DIM // 4).astype(BF16)
        v_ref[g] = y[:, v0 + g * HEAD_DIM:v0 + (g + 1) * HEAD_DIM].astype(BF16)
    u_ref[...] = y[:, v0 + N_KV_HEADS * HEAD_DIM:].astype(BF16)


def inproj_ab(x, modtab, w_in, q_norm, k_norm, cos, sin, n_ctx):
    b, t, d = x.shape
    n_in = w_in.shape[1]
    s5_ch = n_in - (N_HEADS + 2 * N_KV_HEADS) * HEAD_DIM
    rep = N_HEADS // N_KV_HEADS
    tq = ROW_TILE
    nct = n_ctx // tq
    seg = lambda i: jnp.minimum(i // nct, 1) if nct > 0 else 1
    kern = functools.partial(_inproj_ab_kernel, scale=HEAD_DIM ** -0.5)
    return pl.pallas_call(
        kern,
        grid=(b, t // tq),
        in_specs=[
            pl.BlockSpec((None, tq, d), lambda bi, i: (bi, i, 0)),
            pl.BlockSpec((None, None, 6, d), lambda bi, i: (bi, seg(i), 0, 0)),
            pl.BlockSpec((d, n_in), lambda bi, i: (0, 0)),
            pl.BlockSpec((1, HEAD_DIM), lambda bi, i: (0, 0)),
            pl.BlockSpec((1, HEAD_DIM), lambda bi, i: (0, 0)),
            pl.BlockSpec((tq, LANES), lambda bi, i: (i, 0)),
            pl.BlockSpec((tq, LANES), lambda bi, i: (i, 0)),
        ],
        out_specs=[
            pl.BlockSpec((None, N_KV_HEADS, rep, tq, HEAD_DIM), lambda bi, i: (bi, 0, 0, i, 0)),
            pl.BlockSpec((None, N_KV_HEADS, tq, HEAD_DIM), lambda bi, i: (bi, 0, i, 0)),
            pl.BlockSpec((None, N_KV_HEADS, tq, HEAD_DIM), lambda bi, i: (bi, 0, i, 0)),
            pl.BlockSpec((None, tq, s5_ch), lambda bi, i: (bi, i, 0)),
        ],
        out_shape=[
            jax.ShapeDtypeStruct((b, N_KV_HEADS, rep, t, HEAD_DIM), BF16),
            jax.ShapeDtypeStruct((b, N_KV_HEADS, t, HEAD_DIM), BF16),
            jax.ShapeDtypeStruct((b, N_KV_HEADS, t, HEAD_DIM), BF16),
            jax.ShapeDtypeStruct((b, t, s5_ch), BF16),
        ],
        compiler_params=_cparams(("parallel", "parallel")),
        name="inproj_ab",
    )(x, modtab, w_in, q_norm.reshape(1, -1), k_norm.reshape(1, -1), cos, sin)


def _inproj_cd_kernel(x_ref, mod_ref, w_ref, qln_ref, kvln_ref, wuq_ref, wukv_ref,
                      cos_ref, sin_ref, cosm_ref, sinm_ref,
                      qm_ref, km_ref, vm_ref, qd_ref, kd_ref, vd_ref, *, scale_c, scale_d, q_lora, kv_lora):
    x = x_ref[...]
    h = (x * (1.0 + mod_ref[1:2, :]) + mod_ref[0:1, :]).astype(BF16)
    y = _dot(h, w_ref[...])
    cos = cos_ref[...]
    sin = sin_ref[...]
    cosm = cosm_ref[...]
    sinm = sinm_ref[...]
    rep = N_HEADS // N_KV_HEADS
    cq = _rms(y[:, :q_lora], qln_ref[...]).astype(BF16)
    ckv = _rms(y[:, q_lora:q_lora + kv_lora], kvln_ref[...]).astype(BF16)
    q = _dot(cq, wuq_ref[...])
    kv = _dot(ckv, wukv_ref[...])
    o = q_lora + kv_lora
    qd0, kd0 = o, o + N_HEADS * HEAD_DIM
    vd0 = kd0 + N_KV_HEADS * HEAD_DIM
    kr0 = vd0 + N_KV_HEADS * HEAD_DIM
    k_rope = _rope_lanes(y[:, kr0:kr0 + LANES], cosm, sinm, C_ROPE // 4)
    for hh in range(N_HEADS):
        qn = q[:, hh * MLA_QK:hh * MLA_QK + HEAD_DIM]
        qr = _rope_lanes(q[:, hh * MLA_QK + HEAD_DIM:(hh + 1) * MLA_QK], cosm, sinm, C_ROPE // 4)
        qm_ref[hh, 0] = (jnp.concatenate([qn, qr], axis=-1) * scale_c).astype(BF16)
        kn = kv[:, hh * HEAD_DIM:(hh + 1) * HEAD_DIM]
        km_ref[hh] = jnp.concatenate([kn, k_rope], axis=-1).astype(BF16)
        vm_ref[hh] = kv[:, (N_HEADS + hh) * HEAD_DIM:(N_HEADS + hh + 1) * HEAD_DIM].astype(BF16)
        qdh = _rope_lanes(y[:, qd0 + hh * HEAD_DIM:qd0 + (hh + 1) * HEAD_DIM], cos, sin, HEAD_DIM // 4)
        qd_ref[hh // rep, hh % rep] = (qdh * scale_d).astype(BF16)
    for g in range(N_KV_HEADS):
        kdh = _rope_lanes(y[:, kd0 + g * HEAD_DIM:kd0 + (g + 1) * HEAD_DIM], cos, sin, HEAD_DIM // 4)
        kd_ref[g] = kdh.astype(BF16)
        vd_ref[g] = y[:, vd0 + g * HEAD_DIM:vd0 + (g + 1) * HEAD_DIM].astype(BF16)


def inproj_cd(x, modtab, w_in, q_ln, kv_ln, w_uq, w_ukv, cos, sin, cosm, sinm, n_ctx):
    b, t, d = x.shape
    n_in = w_in.shape[1]
    q_lora, kv_lora = q_ln.shape[0], kv_ln.shape[0]
    rep = N_HEADS // N_KV_HEADS
    tq = ROW_TILE
    nct = n_ctx // tq
    seg = lambda i: jnp.minimum(i // nct, 1) if nct > 0 else 1
    kern = functools.partial(_inproj_cd_kernel, scale_c=(HEAD_DIM + C_ROPE) ** -0.5,
                             scale_d=HEAD_DIM ** -0.5, q_lora=q_lora, kv_lora=kv_lora)
    full = lambda shape: pl.BlockSpec(shape, lambda bi, i: (0,) * len(shape))
    tab = pl.BlockSpec((tq, LANES), lambda bi, i: (i, 0))
    return pl.pallas_call(
        kern,
        grid=(b, t // tq),
        in_specs=[
            pl.BlockSpec((None, tq, d), lambda bi, i: (bi, i, 0)),
            pl.BlockSpec((None, None, 6, d), lambda bi, i: (bi, seg(i), 0, 0)),
            full((d, n_in)), full((1, q_lora)), full((1, kv_lora)),
            full(w_uq.shape), full(w_ukv.shape), tab, tab, tab, tab,
        ],
        out_specs=[
            pl.BlockSpec((None, N_HEADS, 1, tq, MLA_QK), lambda bi, i: (bi, 0, 0, i, 0)),
            pl.BlockSpec((None, N_HEADS, tq, MLA_QK), lambda bi, i: (bi, 0, i, 0)),
            pl.BlockSpec((None, N_HEADS, tq, HEAD_DIM), lambda bi, i: (bi, 0, i, 0)),
            pl.BlockSpec((None, N_KV_HEADS, rep, tq, HEAD_DIM), lambda bi, i: (bi, 0, 0, i, 0)),
            pl.BlockSpec((None, N_KV_HEADS, tq, HEAD_DIM), lambda bi, i: (bi, 0, i, 0)),
            pl.BlockSpec((None, N_KV_HEADS, tq, HEAD_DIM), lambda bi, i: (bi, 0, i, 0)),
        ],
        out_shape=[
            jax.ShapeDtypeStruct((b, N_HEADS, 1, t, MLA_QK), BF16),
            jax.ShapeDtypeStruct((b, N_HEADS, t, MLA_QK), BF16),
            jax.ShapeDtypeStruct((b, N_HEADS, t, HEAD_DIM), BF16),
            jax.ShapeDtypeStruct((b, N_KV_HEADS, rep, t, HEAD_DIM), BF16),
            jax.ShapeDtypeStruct((b, N_KV_HEADS, t, HEAD_DIM), BF16),
            jax.ShapeDtypeStruct((b, N_KV_HEADS, t, HEAD_DIM), BF16),
        ],
        compiler_params=_cparams(("parallel", "parallel")),
        name="inproj_cd",
    )(x, modtab, w_in, q_ln.reshape(1, -1), kv_ln.reshape(1, -1), w_uq, w_ukv, cos, sin, cosm, sinm)


def _prep_cd_weights(w_in, w_uq, w_ukv):
    d = w_in.shape[0]
    q_lora = w_uq.shape[0]
    kv_lora = w_ukv.shape[0]
    o = q_lora + kv_lora
    k_rope = w_in[:, o:o + C_ROPE]
    rest = w_in[:, o + C_ROPE:]
    w_in_r = jnp.concatenate([w_in[:, :o], rest, k_rope, jnp.zeros((d, LANES - C_ROPE), w_in.dtype)], axis=1)
    uq = w_uq.reshape(q_lora, N_HEADS, HEAD_DIM + C_ROPE)
    uq = jnp.pad(uq, ((0, 0), (0, 0), (0, MLA_QK - HEAD_DIM - C_ROPE))).reshape(q_lora, N_HEADS * MLA_QK)
    ukv = w_ukv.reshape(kv_lora, N_HEADS, 2, HEAD_DIM).transpose(0, 2, 1, 3).reshape(kv_lora, 2 * N_HEADS * HEAD_DIM)
    return w_in_r.astype(BF16), uq.astype(BF16), ukv.astype(BF16)


def _softmax_pv(blocks, sink_col):
    mx = None
    for s, _ in blocks:
        bm = jnp.max(s, axis=-1, keepdims=True)
        mx = bm if mx is None else jnp.maximum(mx, bm)
    if sink_col is not None:
        mx = jnp.maximum(mx, sink_col)
    den = None
    acc = None
    for s, v in blocks:
        p = jnp.exp(s - mx)
        ps = jnp.sum(p, axis=-1, keepdims=True)
        den = ps if den is None else den + ps
        pv = _dot(p.astype(BF16), v)
        acc = pv if acc is None else acc + pv
    if sink_col is not None:
        den = den + jnp.exp(sink_col - mx)
    return acc / den


def _online_softmax_pv(q, k_ref, v_ref, sink_col):
    n_keys = k_ref.shape[0]
    m = sink_col if sink_col is not None else jnp.full((q.shape[0], 1), -jnp.inf, F32)
    den = None
    acc = None
    for c0 in range(0, n_keys, KEY_CHUNK):
        c1 = min(c0 + KEY_CHUNK, n_keys)
        s = _dot_nt(q, k_ref[c0:c1, :])
        m_new = jnp.maximum(m, jnp.max(s, axis=-1, keepdims=True))
        p = jnp.exp(s - m_new)
        ps = jnp.sum(p, axis=-1, keepdims=True)
        pv = _dot(p.astype(BF16), v_ref[c0:c1, :])
        if acc is None:
            den, acc = ps, pv
        else:
            corr = jnp.exp(m - m_new)
            den = den * corr + ps
            acc = acc * corr + pv
        m = m_new
    if sink_col is not None:
        den = den + jnp.exp(sink_col - m)
    return acc / den


def _attn_kernel(sink_ref, q_ref, k_ref, v_ref, o_ref, *, n_ctx, tq, rep, hps, windowed, use_sink):
    gi = pl.program_id(1)
    qi = pl.program_id(2)
    t_all = k_ref.shape[1]
    dk = q_ref.shape[-1]
    dv = v_ref.shape[-1]
    nct = n_ctx // tq

    def head(h, ctx_tile):
        g = gi * hps + h
        q = q_ref[h].reshape(rep * tq, dk)
        kh = k_ref.at[h]
        vh = v_ref.at[h]
        if use_sink:
            row = lax.broadcasted_iota(I32, (rep * tq, 1), 0)
            sink_col = jnp.full((rep * tq, 1), sink_ref[g * rep], F32)
            for r in range(1, rep):
                sink_col = jnp.where(row >= r * tq, sink_ref[g * rep + r], sink_col)
        else:
            sink_col = None
        if ctx_tile:
            o = _softmax_pv([(_dot_nt(q, kh[0:n_ctx, :]), vh[0:n_ctx, :])], sink_col)
        elif not windowed:
            o = _online_softmax_pv(q, kh, vh, sink_col)
        else:
            band = tq + 2 * D_WINDOW
            s0 = (qi - nct) * tq
            kstart = jnp.clip(n_ctx + s0 - D_WINDOW, n_ctx, t_all - band)
            kstart = pl.multiple_of(kstart, LANES)
            sb = _dot_nt(q, kh[pl.ds(kstart, band), :])
            rowq = lax.broadcasted_iota(I32, (rep * tq, band), 0) % tq + s0
            colk = lax.broadcasted_iota(I32, (rep * tq, band), 1) + (kstart - n_ctx)
            sb = jnp.where(jnp.abs(colk - rowq) <= D_WINDOW, sb, NEG_INF)
            o = _softmax_pv([(_dot_nt(q, kh[0:n_ctx, :]), vh[0:n_ctx, :]), (sb, vh[pl.ds(kstart, band), :])],
                            sink_col)
        for r in range(rep):
            c0 = (h * rep + r) * dv
            o_ref[:, c0:c0 + dv] = o[r * tq:(r + 1) * tq].astype(o_ref.dtype)

    @pl.when(qi < nct)
    def _():
        for h in range(hps):
            head(h, True)

    @pl.when(qi >= nct)
    def _():
        for h in range(hps):
            head(h, False)


def attention(q, k, v, n_ctx, *, windowed=False, sink=None):
    b, g, rep, t, dk = q.shape
    dv = v.shape[-1]
    tq = ROW_TILE
    use_sink = sink is not None
    if sink is None:
        sink = jnp.zeros((g * rep,), F32)
    hps = ATTN_HEADS_PER_STEP
    kern = functools.partial(_attn_kernel, n_ctx=n_ctx, tq=tq, rep=rep, hps=hps, windowed=windowed,
                             use_sink=use_sink)
    return pl.pallas_call(
        kern,
        grid=(b, g // hps, t // tq),
        in_specs=[
            pl.BlockSpec(memory_space=pltpu.SMEM),
            pl.BlockSpec((None, hps, rep, tq, dk), lambda bi, gi, i: (bi, gi, 0, i, 0)),
            pl.BlockSpec((None, hps, t, dk), lambda bi, gi, i: (bi, gi, 0, 0)),
            pl.BlockSpec((None, hps, t, dv), lambda bi, gi, i: (bi, gi, 0, 0)),
        ],
        out_specs=pl.BlockSpec((None, tq, hps * rep * dv), lambda bi, gi, i: (bi, i, gi)),
        out_shape=jax.ShapeDtypeStruct((b, t, g * rep * dv), BF16),
        compiler_params=_cparams(("parallel", "parallel", "arbitrary")),
        name="attention_win" if windowed else "attention",
    )(sink.astype(F32), q, k, v)


def _s5_tables(lam_re, lam_im, log_dt, b_re, b_im, c_re, c_im, d_skip):
    n_groups, n_state = lam_re.shape[1:]
    L, gs = S5_CHUNK, S5_GROUP
    lam = lax.complex(lam_re.astype(F32), lam_im.astype(F32))
    dt = jnp.exp(log_dt.astype(F32))[..., None]
    lam_dt = lam * dt
    lam_bar = jnp.exp(lam_dt)
    b_bar = ((lam_bar - 1.0) / lam)[..., None] * lax.complex(b_re.astype(F32), b_im.astype(F32))
    c_mat = lax.complex(c_re.astype(F32), c_im.astype(F32))
    pw = jnp.exp(lam_dt[None] * jnp.arange(L + 1, dtype=F32)[:, None, None, None])
    kker = jnp.einsum("dgop,tdgp,dgpi->dtgoi", c_mat, pw[:L], b_bar).real
    s_in = jnp.arange(L)[:, None]
    s_out = jnp.arange(L)[None, :]
    tau_f = s_out - s_in
    tau_r = s_in - s_out
    kf = jnp.where((tau_f >= 0)[:, :, None, None, None], kker[0][jnp.clip(tau_f, 0, L - 1)], 0.0)
    kr = jnp.where((tau_r >= 0)[:, :, None, None, None], kker[1][jnp.clip(tau_r, 0, L - 1)], 0.0)
    kt = kf + kr
    gp = S5_PACK
    n_packs = n_groups // gp
    eye = jnp.eye(gp, dtype=F32)
    t6 = kt.transpose(2, 0, 4, 1, 3).reshape(n_packs, gp, L, gs, L, gs)
    d6 = d_skip.astype(F32).reshape(n_packs, gp, 1, gs, 1, 1) * (
        jnp.eye(L, dtype=F32)[None, None, :, None, :, None] * jnp.eye(gs, dtype=F32)[None, None, None, :, None, :])
    tmat = jnp.einsum("pgsctd,gh->psgcthd", t6 + d6, eye).reshape(n_packs, L * LANES, L * LANES)
    steps = jnp.arange(L, dtype=F32)[:, None, None]
    wf = jnp.exp(lam_dt[0][None] * (L - 1 - steps))[:, :, :, None] * b_bar[0][None]
    wr = jnp.exp(lam_dt[1][None] * steps)[:, :, :, None] * b_bar[1][None]
    def w_pack(w):
        w2 = jnp.stack([w.real, w.imag], axis=0).reshape(2, L, n_packs, gp, n_state, gs)
        return jnp.einsum("bspgqc,gh->psgcbhq", w2, eye).reshape(n_packs, L * LANES, 2 * gp * n_state)
    vf = c_mat[0][None] * jnp.exp(lam_dt[0][None] * (steps + 1))[:, :, None, :]
    vr = c_mat[1][None] * jnp.exp(lam_dt[1][None] * (L - steps))[:, :, None, :]
    def v_pack(vv):
        v2 = jnp.stack([vv.real, -vv.imag], axis=0).reshape(2, L, n_packs, gp, gs, n_state)
        return jnp.einsum("bspgcq,gh->pbgqshc", v2, eye).reshape(n_packs, 2 * gp * n_state, L * LANES)
    def lam_pack(l):
        return jnp.stack([l.real, l.imag], axis=0).reshape(2, n_packs, gp * n_state).transpose(1, 0, 2)
    lam_l = pw[L]
    return (tmat.astype(BF16),
            (w_pack(wf).astype(BF16), v_pack(vf).astype(BF16), lam_pack(lam_l[0])),
            (w_pack(wr).astype(BF16), v_pack(vr).astype(BF16), lam_pack(lam_l[1])))


def _s5_pass_kernel(*refs, reverse):
    if reverse:
        u_ref, yin_ref, w_ref, v_ref, lam_ref, y_ref, s_scr, h_scr = refs
    else:
        u_ref, t_ref, w_ref, v_ref, lam_ref, y_ref, s_scr, h_scr = refs

    @pl.when(pl.program_id(1) == 0)
    def _():
        h_scr[...] = jnp.zeros_like(h_scr)

    nchs, L, b, _ = u_ref.shape
    rows = nchs * b
    half = h_scr.shape[1] // 2
    xg = jnp.concatenate([u_ref[:, s].reshape(rows, LANES) for s in range(L)], axis=-1)
    s_scr[...] = _dot(xg, w_ref[...])
    lr = jnp.broadcast_to(lam_ref[0:1, :], (b, half))
    li = jnp.broadcast_to(lam_ref[1:2, :], (b, half))

    def step(k, carry):
        hr, hi = carry
        j = nchs - 1 - k if reverse else k
        r0 = pl.multiple_of(j * b, b)
        sr = s_scr[pl.ds(r0, b), 0:half]
        si = s_scr[pl.ds(r0, b), half:]
        s_scr[pl.ds(r0, b), 0:half] = hr
        s_scr[pl.ds(r0, b), half:] = hi
        return lr * hr - li * hi + sr, lr * hi + li * hr + si

    hr, hi = lax.fori_loop(0, nchs, step, (h_scr[:, 0:half], h_scr[:, half:]))
    h_scr[:, 0:half] = hr
    h_scr[:, half:] = hi
    y = _dot(s_scr[...].astype(BF16), v_ref[...])
    if not reverse:
        y = y + _dot(xg, t_ref[...])
    for s in range(L):
        blk = y[:, s * LANES:(s + 1) * LANES].reshape(nchs, b, LANES)
        if reverse:
            blk = blk + yin_ref[:, s]
        y_ref[:, s] = blk


def s5_apply(u, tables, n_ctx):
    tmat, fwd, rev = tables
    b, t, ch = u.shape
    L = S5_CHUNK
    n_packs = ch // LANES
    seg = ROW_TILE
    nseg, nct, nchs = t // seg, n_ctx // seg, seg // L
    width = L * LANES
    n_state2 = fwd[0].shape[-1]
    ut = u.transpose(1, 0, 2).reshape(t // L, L, b, ch)
    blk = pl.BlockSpec((nchs, L, b, LANES), lambda p, i: (i, 0, 0, p))
    rseg = lambda i: jnp.where(i < nct, nct - 1 - i, nseg - 1 - (i - nct))
    rblk = pl.BlockSpec((nchs, L, b, LANES), lambda p, i: (rseg(i), 0, 0, p))
    tab = lambda shape: pl.BlockSpec((None,) + shape, lambda p, i: (p, 0, 0))
    scratch = [pltpu.VMEM((nchs * b, n_state2), F32), pltpu.VMEM((b, n_state2), F32)]
    y_shape = jax.ShapeDtypeStruct((t // L, L, b, ch), F32)
    y_f = pl.pallas_call(
        functools.partial(_s5_pass_kernel, reverse=False),
        grid=(n_packs, nseg),
        in_specs=[blk, tab((width, width)), tab((width, n_state2)), tab((n_state2, width)), tab((2, n_state2 // 2))],
        out_specs=blk,
        out_shape=y_shape,
        scratch_shapes=scratch,
        compiler_params=_cparams(("parallel", "arbitrary")),
        name="s5_forward",
    )(ut, tmat, *fwd)
    y = pl.pallas_call(
        functools.partial(_s5_pass_kernel, reverse=True),
        grid=(n_packs, nseg),
        in_specs=[rblk, rblk, tab((width, n_state2)), tab((n_state2, width)), tab((2, n_state2 // 2))],
        out_specs=rblk,
        out_shape=y_shape,
        scratch_shapes=scratch,
        input_output_aliases={1: 0},
        compiler_params=_cparams(("parallel", "arbitrary")),
        name="s5_reverse",
    )(ut, y_f, *rev)
    return y.reshape(t, b, ch).transpose(1, 0, 2)


def _gelu_tanh(x):
    return 0.5 * x * (1.0 + jnp.tanh(math.sqrt(2.0 / math.pi) * (x + 0.044715 * (x * x * x))))


def _post_kernel(*refs, alpha, glu, n_exp):
    if glu:
        (x_ref, a_ref, s_ref, wg_ref, bg_ref, wm_ref, mod_ref, g_ref, b_ref, wrh_ref, wrl_ref, br_ref,
         x1_ref, h2_ref, idx_ref, gate_ref, rank_ref, cnt_ref, cnt_scr) = refs
    else:
        (x_ref, a_ref, s_ref, wm_ref, mod_ref, g_ref, b_ref, wrh_ref, wrl_ref, br_ref,
         x1_ref, h2_ref, idx_ref, gate_ref, rank_ref, cnt_ref, cnt_scr) = refs
    first = (pl.program_id(0) == 0) & (pl.program_id(1) == 0)

    @pl.when(first)
    def _():
        cnt_scr[...] = jnp.zeros_like(cnt_scr)

    tq, d = x_ref.shape
    half = a_ref.shape[1]
    if glu:
        z = _gelu_tanh(s_ref[...])
        gate = jax.nn.sigmoid(_dot(z.astype(BF16), wg_ref[...]) + bg_ref[...])
        second = (z * gate).astype(BF16)
    else:
        second = s_ref[...]
    mix = _dot(a_ref[...], wm_ref[0:half, :]) + _dot(second, wm_ref[half:, :])
    x1 = _layer_norm(alpha * x_ref[...] + mod_ref[2:3, :] * mix, g_ref[...], b_ref[...])
    x1_ref[...] = x1
    h2 = x1 * (1.0 + mod_ref[4:5, :]) + mod_ref[3:4, :]
    _store_token_tiles(h2_ref, h2)

    h_hi, h_lo = _split_bf16(h2)
    logits = _dot_nt(wrh_ref[...], h_hi) + _dot_nt(wrh_ref[...], h_lo) + _dot_nt(wrl_ref[...], h_hi) + br_ref[...]
    eidx = lax.broadcasted_iota(I32, (n_exp, tq), 0)
    work = logits
    tops, sels = [], []
    for k in range(TOP_K):
        m = jnp.max(work, axis=0, keepdims=True)
        ik = jnp.min(jnp.where(work == m, eidx, n_exp), axis=0, keepdims=True)
        sel = eidx == ik
        work = jnp.where(sel, -jnp.inf, work)
        tops.append(m)
        sels.append(sel)
        idx_ref[k:k + 1, :] = ik
    exps = [jnp.exp(tk - tops[0]) for tk in tops]
    den = exps[0] + exps[1] + exps[2] + exps[3]
    for k in range(TOP_K):
        gate_ref[k:k + 1, :] = exps[k] / den
    onehot = jnp.zeros((n_exp, tq), F32)
    for sel in sels:
        onehot = onehot + sel.astype(F32)
    tri = (lax.broadcasted_iota(I32, (tq, tq), 0) < lax.broadcasted_iota(I32, (tq, tq), 1)).astype(BF16)
    before = _dot(onehot.astype(BF16), tri) + cnt_scr[...]
    for k in range(TOP_K):
        rk = jnp.sum(jnp.where(sels[k], before, 0.0), axis=0, keepdims=True)
        rank_ref[k:k + 1, :] = rk.astype(I32)
    cnt_scr[...] += jnp.sum(onehot, axis=1, keepdims=True)
    cnt_ref[...] = jnp.broadcast_to(cnt_scr[...], cnt_ref.shape).astype(I32)


def post_mixer(x, a, s, modtab, w_mix, ln_g, ln_b, w_router, b_router, n_ctx, alpha, glu_w=None, glu_b=None):
    b, t, d = x.shape
    half = a.shape[-1]
    n_exp = w_router.shape[1]
    tq = ROW_TILE
    nct = n_ctx // tq
    seg = lambda i: jnp.minimum(i // nct, 1) if nct > 0 else 1
    glu = glu_w is not None
    wr_hi, wr_lo = _split_bf16(w_router.T.astype(F32))
    full = lambda shape: pl.BlockSpec(shape, lambda bi, i: (0,) * len(shape))
    tok = lambda width: pl.BlockSpec((None, tq, width), lambda bi, i: (bi, i, 0))
    in_specs = [tok(d), tok(half), tok(half)]
    args = [x, a, s]
    if glu:
        in_specs += [full((half, half)), full((1, half))]
        args += [glu_w.astype(BF16), glu_b.reshape(1, half)]
    in_specs += [full((d, d)), pl.BlockSpec((None, None, 6, d), lambda bi, i: (bi, seg(i), 0, 0)),
                 full((1, d)), full((1, d)), full((n_exp, d)), full((n_exp, d)), full((n_exp, 1))]
    args += [w_mix.astype(BF16), modtab, ln_g.reshape(1, d), ln_b.reshape(1, d), wr_hi, wr_lo,
             b_router.reshape(n_exp, 1)]
    nt = t // tq
    lane_out = lambda dt: (pl.BlockSpec((TOP_K, tq), lambda bi, i: (0, bi * nt + i)),
                           jax.ShapeDtypeStruct((TOP_K, b * t), dt))
    outs = [
        (tok(d), jax.ShapeDtypeStruct((b, t, d), F32)),
        (pl.BlockSpec((tq * SUBLANES, LANES), lambda bi, i: (bi * nt + i, 0)),
         jax.ShapeDtypeStruct((b * t * SUBLANES, LANES), F32)),
        lane_out(I32), lane_out(F32), lane_out(I32),
        (pl.BlockSpec((n_exp, LANES), lambda bi, i: (0, 0)), jax.ShapeDtypeStruct((n_exp, LANES), I32)),
    ]
    kern = functools.partial(_post_kernel, alpha=alpha, glu=glu, n_exp=n_exp)
    return pl.pallas_call(
        kern,
        grid=(b, nt),
        in_specs=in_specs,
        out_specs=[o[0] for o in outs],
        out_shape=[o[1] for o in outs],
        scratch_shapes=[pltpu.VMEM((n_exp, 1), F32)],
        compiler_params=_cparams(("arbitrary", "arbitrary")),
        name="post_mixer_glu" if glu else "post_mixer",
    )(*args)


def _route_kernel(start_ref, idx_ref, rank_ref, pos_ref, *, n_exp, tq):
    eidx = lax.broadcasted_iota(I32, (n_exp, tq), 0)
    for k in range(TOP_K):
        sel = eidx == idx_ref[k:k + 1, :]
        base = jnp.sum(jnp.where(sel, start_ref[...], 0.0), axis=0, keepdims=True)
        pos_ref[:, k * tq:(k + 1) * tq] = base.astype(I32) + rank_ref[k:k + 1, :]


def route_positions(group_start, idx, rank):
    n_exp = group_start.shape[0]
    n_tok = idx.shape[1]
    tq = ROW_TILE
    return pl.pallas_call(
        functools.partial(_route_kernel, n_exp=n_exp, tq=tq),
        grid=(n_tok // tq,),
        in_specs=[
            pl.BlockSpec((n_exp, 1), lambda i: (0, 0)),
            pl.BlockSpec((TOP_K, tq), lambda i: (0, i)),
            pl.BlockSpec((TOP_K, tq), lambda i: (0, i)),
        ],
        out_specs=pl.BlockSpec((None, 1, TOP_K * tq), lambda i: (i, 0, 0)),
        out_shape=jax.ShapeDtypeStruct((n_tok // tq, 1, TOP_K * tq), I32),
        compiler_params=_cparams(("parallel",)),
        name="route_positions",
    )(group_start.astype(F32).reshape(n_exp, 1), idx, rank)


def _dispatch_kernel(pos_hbm, h_ref, xs_hbm, idx_smem, idx_sem, row_sem, *, tq, n_tiles, tile_rows):
    i = pl.program_id(0)
    slot = i % 2

    def idx_copy(tile, sl):
        return pltpu.make_async_copy(pos_hbm.at[tile], idx_smem.at[sl], idx_sem.at[sl])

    @pl.when(i == 0)
    def _():
        idx_copy(0, 0).start()

    idx_copy(i, slot).wait()

    @pl.when(i + 1 < n_tiles)
    def _():
        idx_copy(i + 1, 1 - slot).start()

    for k in range(TOP_K):
        for r in range(tq):
            dst = pl.multiple_of(idx_smem[slot, 0, k * tq + r] * tile_rows, tile_rows)
            pltpu.make_async_copy(h_ref.at[pl.ds(r * tile_rows, tile_rows), :],
                                  xs_hbm.at[pl.ds(dst, tile_rows), :], row_sem.at[0]).start(priority=r % 2)
    for k in range(TOP_K):
        pltpu.make_async_copy(h_ref, xs_hbm.at[pl.ds(0, tq * tile_rows), :], row_sem.at[0]).wait()


def moe_dispatch(pos_tiles, h_tiles, tile_rows):
    n_tiles = pos_tiles.shape[0]
    tq = h_tiles.shape[0] // tile_rows // n_tiles
    return pl.pallas_call(
        functools.partial(_dispatch_kernel, tq=tq, n_tiles=n_tiles, tile_rows=tile_rows),
        grid=(n_tiles,),
        in_specs=[pl.BlockSpec(memory_space=pl.ANY),
                  pl.BlockSpec((tq * tile_rows, LANES), lambda i: (i, 0))],
        out_specs=pl.BlockSpec(memory_space=pl.ANY),
        out_shape=jax.ShapeDtypeStruct((TOP_K * h_tiles.shape[0], LANES), F32),
        scratch_shapes=[
            pltpu.SMEM((2, 1, TOP_K * tq), I32),
            pltpu.SemaphoreType.DMA((2,)),
            pltpu.SemaphoreType.DMA((1,)),
        ],
        compiler_params=_cparams(("arbitrary",)),
        name="moe_dispatch",
    )(pos_tiles, h_tiles)


def _moe_kernel(e_ref, j_ref, lo_ref, hi_ref, first_ref, new_ref, x_ref, wi_ref, bi_ref, wo_ref, bo_ref,
                y_ref, wi_b, wo_b, *, n_sub):
    i = pl.program_id(0)
    lo = lo_ref[i]
    hi = hi_ref[i]

    @pl.when(new_ref[i] == 1)
    def _():
        wi_b[...] = wi_ref[...].astype(BF16)
        wo_b[...] = wo_ref[...].astype(BF16)

    tm = x_ref.shape[0] // n_sub
    sub = tm // MOE_CHAINS

    def expert_rows(r0):
        x = _load_token_tiles(x_ref, r0 * n_sub, sub, n_sub).astype(BF16)
        z = _dot(x, wi_b[...]) + bi_ref[...]
        f = z.shape[1] // 2
        glu = jnp.minimum(z[:, :f], SWIGLU_LIMIT)
        lin = jnp.clip(z[:, f:], -SWIGLU_LIMIT, SWIGLU_LIMIT)
        act = glu * jax.nn.sigmoid(SWIGLU_ALPHA * glu) * (lin + 1.0)
        y = _dot(act.astype(BF16), wo_b[...]) + bo_ref[...]
        row = lax.broadcasted_iota(I32, (sub, 1), 0) + r0
        return y, (row >= lo) & (row < hi)

    def out_rows(r0):
        return y_ref.at[pl.ds(r0 * n_sub, sub * n_sub), :]

    @pl.when((hi > lo) & (first_ref[i] == 1))
    def _():
        for r0 in range(0, tm, sub):
            y, mine = expert_rows(r0)
            _store_token_tiles(out_rows(r0), jnp.where(mine, y, 0.0))

    @pl.when((hi > lo) & (first_ref[i] == 0))
    def _():
        for r0 in range(0, tm, sub):
            y, mine = expert_rows(r0)
            _store_token_tiles(out_rows(r0), jnp.where(mine, y, _load_token_tiles(out_rows(r0), 0, sub, n_sub)))


def moe_experts(xs, items, w_in, b_in, w_out, b_out, layer):
    _, n_exp, d, f2 = w_in.shape
    n_sub = d // LANES
    n_items = items[0].shape[0]
    tm = MOE_BLOCK
    wmap = lambda i, e, j, lo, hi, fi, nw: (layer, e[i], 0, 0)
    xmap = lambda i, e, j, lo, hi, fi, nw: (j[i], 0)
    grid_spec = pltpu.PrefetchScalarGridSpec(
        num_scalar_prefetch=6,
        grid=(n_items,),
        in_specs=[
            pl.BlockSpec((tm * n_sub, LANES), xmap),
            pl.BlockSpec((None, None, d, f2), wmap),
            pl.BlockSpec((None, None, 1, f2), wmap),
            pl.BlockSpec((None, None, f2 // 2, d), wmap),
            pl.BlockSpec((None, None, 1, d), wmap),
        ],
        out_specs=pl.BlockSpec((tm * n_sub, LANES), xmap),
        scratch_shapes=[pltpu.VMEM((d, f2), BF16), pltpu.VMEM((f2 // 2, d), BF16)],
    )
    return pl.pallas_call(
        functools.partial(_moe_kernel, n_sub=n_sub),
        grid_spec=grid_spec,
        out_shape=jax.ShapeDtypeStruct(xs.shape, F32),
        compiler_params=_cparams(("arbitrary",)),
        name="moe_experts",
    )(*items, xs, w_in, b_in.reshape(b_in.shape[0], n_exp, 1, f2), w_out, b_out.reshape(b_out.shape[0], n_exp, 1, d))


def _combine_kernel(pos_hbm, x_ref, g_ref, mod_ref, lg_ref, lb_ref, ys_hbm, o_ref,
                    idx_smem, ybuf, idx_sem, row_sem, *, alpha, tq, n_tiles):
    i = pl.program_id(0)
    slot = i % 2
    d = x_ref.shape[1]

    def idx_copy(tile, sl):
        return pltpu.make_async_copy(pos_hbm.at[tile], idx_smem.at[sl], idx_sem.at[sl])

    n_sub = d // LANES

    def gather_start(sl):
        for r in range(TOP_K * tq):
            src = pl.multiple_of(idx_smem[sl, 0, r] * n_sub, n_sub)
            pltpu.make_async_copy(ys_hbm.at[pl.ds(src, n_sub), :], ybuf.at[sl, pl.ds(r * n_sub, n_sub), :],
                                  row_sem.at[sl]).start(priority=r % 2)

    def rows_wait(sl):
        pltpu.make_async_copy(ys_hbm.at[pl.ds(0, TOP_K * tq * n_sub), :], ybuf.at[sl], row_sem.at[sl]).wait()

    @pl.when(i == 0)
    def _():
        idx_copy(0, 0).start()
        idx_copy(0, 0).wait()
        gather_start(0)

        @pl.when(n_tiles > 1)
        def _():
            idx_copy(1, 1).start()

    @pl.when(i + 1 < n_tiles)
    def _():
        idx_copy(i + 1, 1 - slot).wait()
        gather_start(1 - slot)

    rows_wait(slot)

    @pl.when(i + 2 < n_tiles)
    def _():
        idx_copy(i + 2, slot).start()

    yb = ybuf.at[slot]
    gates = g_ref[...]
    y = None
    for k in range(TOP_K):
        term = gates[:, k:k + 1] * _load_token_tiles(yb, k * tq * n_sub, tq, n_sub)
        y = term if y is None else y + term
    o_ref[...] = _layer_norm(alpha * x_ref[...] + mod_ref[5:6, :] * y, lg_ref[...], lb_ref[...])


def moe_combine(x1, pos_tiles, gates_tok, modtab, ln_g, ln_b, ys, n_ctx, alpha, lat_only=False):
    b, t, d = x1.shape
    tq = ROW_TILE
    nt = t // tq
    n_tiles = b * nt
    nct = n_ctx // tq
    seg = lambda i: jnp.minimum((i % nt) // nct, 1) if nct > 0 else 1
    kern = functools.partial(_combine_kernel, alpha=alpha, tq=tq, n_tiles=n_tiles)
    if lat_only:
        nlt = nt - nct
        out_map = lambda i: ((i // nt) * nlt + jnp.maximum(i % nt - nct, 0), 0)
        out_rows = b * nlt * tq
    else:
        out_map = lambda i: (i, 0)
        out_rows = b * t
    out = pl.pallas_call(
        kern,
        grid=(n_tiles,),
        in_specs=[
            pl.BlockSpec(memory_space=pl.ANY),
            pl.BlockSpec((tq, d), lambda i: (i, 0)),
            pl.BlockSpec((tq, TOP_K), lambda i: (i, 0)),
            pl.BlockSpec((None, None, 6, d), lambda i: (i // nt, seg(i), 0, 0)),
            pl.BlockSpec((1, d), lambda i: (0, 0)),
            pl.BlockSpec((1, d), lambda i: (0, 0)),
            pl.BlockSpec(memory_space=pl.ANY),
        ],
        out_specs=pl.BlockSpec((tq, d), out_map),
        out_shape=jax.ShapeDtypeStruct((out_rows, d), F32),
        scratch_shapes=[
            pltpu.SMEM((2, 1, TOP_K * tq), I32),
            pltpu.VMEM((2, TOP_K * tq * (d // LANES), LANES), F32),
            pltpu.SemaphoreType.DMA((2,)),
            pltpu.SemaphoreType.DMA((2,)),
        ],
        compiler_params=_cparams(("arbitrary",)),
        name="moe_combine",
    )(pos_tiles, x1.reshape(b * t, d), gates_tok, modtab, ln_g.reshape(1, d), ln_b.reshape(1, d), ys)
    return out.reshape(b, out_rows // b, d)


def _group_items(counts, n_slots, tm):
    n_exp = counts.shape[0]
    le = (jnp.arange(n_exp)[:, None] <= jnp.arange(n_exp)[None, :]).astype(I32)
    g_end = counts @ le
    g_start = g_end - counts
    first_blk = g_start // tm
    n_blk = jnp.where(counts > 0, (g_end - 1) // tm - first_blk + 1, 0)
    i_end = n_blk @ le
    i_start = i_end - n_blk
    n_items = n_slots // tm + n_exp - 1
    it = jnp.arange(n_items, dtype=I32)
    valid = it < i_end[-1]
    e = jnp.minimum(jnp.sum((i_end[None, :] <= it[:, None]).astype(I32), axis=1), n_exp - 1)
    onehot = (e[:, None] == jnp.arange(n_exp)[None, :]).astype(I32)
    pick = lambda tbl: onehot @ tbl
    blk = pick(first_blk) + it - pick(i_start)
    lo = jnp.clip(pick(g_start) - blk * tm, 0, tm)
    hi = jnp.clip(pick(g_end) - blk * tm, 0, tm)
    last = jnp.maximum(i_end[-1] - 1, 0)
    e_last = jnp.sum(jnp.where(it == last, e, 0))
    e = jnp.where(valid, e, e_last)
    blk = jnp.where(valid, blk, n_slots // tm - 1)
    lo = jnp.where(valid, lo, 0)
    hi = jnp.where(valid, hi, 0)
    prev = lambda a: jnp.concatenate([jnp.full((1,), -1, I32), a[:-1]])
    first = (blk != prev(blk)).astype(I32)
    new_e = (e != prev(e)).astype(I32)
    return g_start, tuple(a.astype(I32) for a in (e, blk, lo, hi, first, new_e))


def kernel(x, c, ctx, c_ctx, w_mod, b_mod, ln_g, ln_b, w_mix_out, w_router, b_router, w_exp_in, b_exp_in, w_exp_out, b_exp_out, w_in_ab, a_q_norm, a_k_norm, s5_lam_re, s5_lam_im, s5_log_dt, s5_b_re, s5_b_im, s5_c_re, s5_c_im, s5_d, w_glu, b_glu, w_in_cd, c_q_norm, c_kv_norm, w_uq, w_ukv, d_sink):
    batch, n_lat, d = x.shape
    n_ctx = ctx.shape[1]
    depth = w_mod.shape[0]
    alpha = (2.0 * depth) ** 0.25
    t = n_ctx + n_lat
    n_tok = batch * t

    xs = jnp.concatenate([ctx, x], axis=1)
    n_rows = -(-(batch + 1) // SUBLANES) * SUBLANES
    cvec = jnp.zeros((n_rows, d), F32).at[:batch].set(c).at[batch].set(c_ctx)
    mods = modulation(cvec, w_mod, b_mod).reshape(depth, n_rows, 6, d)
    modtab = jnp.stack([jnp.broadcast_to(mods[:, batch:batch + 1], (depth, batch, 6, d)), mods[:, :batch]],
                       axis=2)

    cos_h, sin_h = _rope_tables(n_ctx, n_lat, HEAD_DIM)
    cos_m, sin_m = _rope_tables(n_ctx, n_lat, C_ROPE)

    for layer in range(depth):
        i = layer // 2
        mt = modtab[layer]
        if layer % 2 == 0:
            q, k, v, u = inproj_ab(xs, mt, w_in_ab[i].astype(BF16), a_q_norm[i], a_k_norm[i], cos_h, sin_h, n_ctx)
            att = attention(q, k, v, n_ctx)
            tables = _s5_tables(s5_lam_re[i], s5_lam_im[i], s5_log_dt[i], s5_b_re[i], s5_b_im[i],
                                s5_c_re[i], s5_c_im[i], s5_d[i])
            second = s5_apply(u, tables, n_ctx)
            glu_w, glu_b = w_glu[i], b_glu[i]
        else:
            w_in_r, uq_r, ukv_r = _prep_cd_weights(w_in_cd[i], w_uq[i], w_ukv[i])
            qm, km, vm, qd, kd, vd = inproj_cd(xs, mt, w_in_r, c_q_norm[i], c_kv_norm[i], uq_r, ukv_r,
                                               cos_h, sin_h, cos_m, sin_m, n_ctx)
            att = attention(qm, km, vm, n_ctx)
            second = attention(qd, kd, vd, n_ctx, windowed=True, sink=d_sink[i])
            glu_w = glu_b = None
        x1, h2, idx, gates, rank, counts = post_mixer(
            xs, att, second, mt, w_mix_out[layer], ln_g[layer, 0], ln_b[layer, 0],
            w_router[layer], b_router[layer], n_ctx, alpha, glu_w, glu_b)
        g_start, items = _group_items(counts[:, 0], TOP_K * n_tok, MOE_BLOCK)
        pos_tiles = route_positions(g_start, idx, rank)
        sorted_rows = moe_dispatch(pos_tiles, h2, d // LANES)
        ys = moe_experts(sorted_rows, items, w_exp_in, b_exp_in, w_exp_out, b_exp_out, layer)
        xs = moe_combine(x1, pos_tiles, gates.T, mt, ln_g[layer, 1], ln_b[layer, 1], ys, n_ctx, alpha,
                         lat_only=layer == depth - 1)
    return xs
```

```python
import functools
import math

import jax
import jax.numpy as jnp
from jax import lax
from jax.experimental import pallas as pl
from jax.experimental.pallas import tpu as pltpu

F32 = jnp.float32
BF16 = jnp.bfloat16
I32 = jnp.int32

GRID_W = 64
ROPE_THETA = 10000.0
HEAD_DIM = 128
N_HEADS = 4
N_KV_HEADS = 2
C_ROPE = 64
MLA_QK = 256
D_WINDOW = 128
S5_GROUP = 16
S5_STATE = 64
S5_CHUNK = 8
S5_PACK = 8
TOP_K = 4
SWIGLU_LIMIT = 7.0
SWIGLU_ALPHA = 1.702
NEG_INF = -1e30
LOG2E = 1.4426950408889634
LN_EPS = 1e-5
RMS_EPS = 1e-6

LANES = 128
SUBLANES = 8
VMEM_LIMIT_BYTES = 56 * 1024 * 1024

ROW_TILE = 256
MOE_BLOCK = 512
MOE_CHAINS = 2
KEY_CHUNK = 768
ATTN_HEADS_PER_STEP = 2


def _cparams(sem):
    return pltpu.CompilerParams(dimension_semantics=sem, vmem_limit_bytes=VMEM_LIMIT_BYTES)


def _split_bf16(a):
    hi = a.astype(BF16)
    lo = (a - hi.astype(F32)).astype(BF16)
    return hi, lo


def _dot(a, b):
    return jnp.dot(a, b, preferred_element_type=F32)


def _dot_nt(a, b):
    return lax.dot_general(a, b, (((1,), (1,)), ((), ())), preferred_element_type=F32)


def _store_token_tiles(ref, val):
    rows, d = val.shape
    n = d // LANES
    for s in range(n):
        ref[pl.ds(s, rows, stride=n), :] = val[:, s * LANES:(s + 1) * LANES]


def _load_token_tiles(ref, first_row, rows, n):
    return jnp.concatenate([ref[pl.ds(first_row + s, rows, stride=n), :] for s in range(n)], axis=-1)


def _layer_norm(x, g, b):
    mu = jnp.mean(x, axis=-1, keepdims=True)
    xc = x - mu
    var = jnp.mean(xc * xc, axis=-1, keepdims=True)
    return xc * lax.rsqrt(var + LN_EPS) * g + b


def _rms(x, g):
    return x * lax.rsqrt(jnp.mean(x * x, axis=-1, keepdims=True) + RMS_EPS) * g


def _mod_kernel(c_ref, w_ref, b_ref, o_ref):
    c = c_ref[...]
    s = c * jax.nn.sigmoid(c)
    s_hi, s_lo = _split_bf16(s)
    w_hi, w_lo = _split_bf16(w_ref[...])
    o_ref[...] = _dot(s_hi, w_hi) + _dot(s_lo, w_hi) + _dot(s_hi, w_lo) + b_ref[...]


def modulation(cvec, w_mod, b_mod):
    n_layers, d, d6 = w_mod.shape
    r = cvec.shape[0]
    tn = 1536
    return pl.pallas_call(
        _mod_kernel,
        grid=(n_layers, d6 // tn),
        in_specs=[
            pl.BlockSpec((r, d), lambda l, j: (0, 0)),
            pl.BlockSpec((None, d, tn), lambda l, j: (l, 0, j)),
            pl.BlockSpec((None, 1, tn), lambda l, j: (l, 0, j)),
        ],
        out_specs=pl.BlockSpec((None, r, tn), lambda l, j: (l, 0, j)),
        out_shape=jax.ShapeDtypeStruct((n_layers, r, d6), F32),
        compiler_params=_cparams(("arbitrary", "arbitrary")),
        name="modulation",
    )(cvec, w_mod, b_mod.reshape(n_layers, 1, d6))


def _rope_lanes(t, cos, sin, quarter):
    lane = lax.broadcasted_iota(I32, t.shape, 1)
    first = (lane % (2 * quarter)) < quarter
    partner = jnp.where(first, pltpu.roll(t, LANES - quarter, 1), pltpu.roll(t, quarter, 1))
    return t * cos + partner * sin


def _rope_tables(n_ctx, n_lat, rot_dim):
    quarter = rot_dim // 4
    pos = jnp.arange(n_lat, dtype=F32)
    row = jnp.floor(pos / GRID_W)
    col = pos - row * GRID_W
    inv_freq = ROPE_THETA ** (-jnp.arange(quarter, dtype=F32) / quarter)
    ang_r = row[:, None] * inv_freq
    ang_c = col[:, None] * inv_freq
    cos = jnp.concatenate([jnp.cos(ang_r)] * 2 + [jnp.cos(ang_c)] * 2, axis=-1)
    sin = jnp.concatenate([-jnp.sin(ang_r), jnp.sin(ang_r), -jnp.sin(ang_c), jnp.sin(ang_c)], axis=-1)
    pad = LANES - rot_dim
    cos = jnp.pad(cos, ((n_ctx, 0), (0, pad)), constant_values=1.0)
    sin = jnp.pad(sin, ((n_ctx, 0), (0, pad)))
    return cos, sin


def _inproj_ab_kernel(x_ref, mod_ref, w_ref, qn_ref, kn_ref, cos_ref, sin_ref,
                      q_ref, k_ref, v_ref, u_ref, *, scale):
    x = x_ref[...]
    h = (x * (1.0 + mod_ref[1:2, :]) + mod_ref[0:1, :]).astype(BF16)
    y = _dot(h, w_ref[...])
    cos = cos_ref[...]
    sin = sin_ref[...]
    rep = N_HEADS // N_KV_HEADS
    for hh in range(N_HEADS):
        qh = _rms(y[:, hh * HEAD_DIM:(hh + 1) * HEAD_DIM], qn_ref[...])
        qh = _rope_lanes(qh, cos, sin, HEAD_DIM // 4) * scale
        q_ref[hh // rep, hh % rep] = qh.astype(BF16)
    k0 = N_HEADS * HEAD_DIM
    v0 = k0 + N_KV_HEADS * HEAD_DIM
    for g in range(N_KV_HEADS):
        kh = _rms(y[:, k0 + g * HEAD_DIM:k0 + (g + 1) * HEAD_DIM], kn_ref[...])
        k_ref[g] = _rope_lanes(kh, cos, sin, HEAD_DIM // 4).astype(BF16)
        v_ref[g] = y[:, v0 + g * HEAD_DIM:v0 + (g + 1) * HEAD_DIM].astype(BF16)
    u_ref[...] = y[:, v0 + N_KV_HEADS * HEAD_DIM:].astype(BF16)


def inproj_ab(x, modtab, w_in, q_norm, k_norm, cos, sin, n_ctx):
    b, t, d = x.shape
    n_in = w_in.shape[1]
    s5_ch = n_in - (N_HEADS + 2 * N_KV_HEADS) * HEAD_DIM
    rep = N_HEADS // N_KV_HEADS
    tq = ROW_TILE
    nct = n_ctx // tq
    seg = lambda i: jnp.minimum(i // nct, 1) if nct > 0 else 1
    kern = functools.partial(_inproj_ab_kernel, scale=HEAD_DIM ** -0.5 * LOG2E)
    return pl.pallas_call(
        kern,
        grid=(b, t // tq),
        in_specs=[
            pl.BlockSpec((None, tq, d), lambda bi, i: (bi, i, 0)),
            pl.BlockSpec((None, None, 6, d), lambda bi, i: (bi, seg(i), 0, 0)),
            pl.BlockSpec((d, n_in), lambda bi, i: (0, 0)),
            pl.BlockSpec((1, HEAD_DIM), lambda bi, i: (0, 0)),
            pl.BlockSpec((1, HEAD_DIM), lambda bi, i: (0, 0)),
            pl.BlockSpec((tq, LANES), lambda bi, i: (i, 0)),
            pl.BlockSpec((tq, LANES), lambda bi, i: (i, 0)),
        ],
        out_specs=[
            pl.BlockSpec((None, N_KV_HEADS, rep, tq, HEAD_DIM), lambda bi, i: (bi, 0, 0, i, 0)),
            pl.BlockSpec((None, N_KV_HEADS, tq, HEAD_DIM), lambda bi, i: (bi, 0, i, 0)),
            pl.BlockSpec((None, N_KV_HEADS, tq, HEAD_DIM), lambda bi, i: (bi, 0, i, 0)),
            pl.BlockSpec((None, tq, s5_ch), lambda bi, i: (bi, i, 0)),
        ],
        out_shape=[
            jax.ShapeDtypeStruct((b, N_KV_HEADS, rep, t, HEAD_DIM), BF16),
            jax.ShapeDtypeStruct((b, N_KV_HEADS, t, HEAD_DIM), BF16),
            jax.ShapeDtypeStruct((b, N_KV_HEADS, t, HEAD_DIM), BF16),
            jax.ShapeDtypeStruct((b, t, s5_ch), BF16),
        ],
        compiler_params=_cparams(("parallel", "parallel")),
        name="inproj_ab",
    )(x, modtab, w_in, q_norm.reshape(1, -1), k_norm.reshape(1, -1), cos, sin)


def _inproj_cd_kernel(x_ref, mod_ref, w_ref, qln_ref, kvln_ref, wuq_ref, wukv_ref,
                      cos_ref, sin_ref, cosm_ref, sinm_ref,
                      qm_ref, km_ref, vm_ref, qd_ref, kd_ref, vd_ref, *, scale_c, scale_d, q_lora, kv_lora):
    x = x_ref[...]
    h = (x * (1.0 + mod_ref[1:2, :]) + mod_ref[0:1, :]).astype(BF16)
    y = _dot(h, w_ref[...])
    cos = cos_ref[...]
    sin = sin_ref[...]
    cosm = cosm_ref[...]
    sinm = sinm_ref[...]
    rep = N_HEADS // N_KV_HEADS
    cq = _rms(y[:, :q_lora], qln_ref[...]).astype(BF16)
    ckv = _rms(y[:, q_lora:q_lora + kv_lora], kvln_ref[...]).astype(BF16)
    q = _dot(cq, wuq_ref[...])
    kv = _dot(ckv, wukv_ref[...])
    o = q_lora + kv_lora
    qd0, kd0 = o, o + N_HEADS * HEAD_DIM
    vd0 = kd0 + N_KV_HEADS * HEAD_DIM
    kr0 = vd0 + N_KV_HEADS * HEAD_DIM
    k_rope = _rope_lanes(y[:, kr0:kr0 + LANES], cosm, sinm, C_ROPE // 4)
    for hh in range(N_HEADS):
        qn = q[:, hh * MLA_QK:hh * MLA_QK + HEAD_DIM]
        qr = _rope_lanes(q[:, hh * MLA_QK + HEAD_DIM:(hh + 1) * MLA_QK], cosm, sinm, C_ROPE // 4)
        qm_ref[hh, 0] = (jnp.concatenate([qn, qr], axis=-1) * scale_c).astype(BF16)
        kn = kv[:, hh * HEAD_DIM:(hh + 1) * HEAD_DIM]
        km_ref[hh] = jnp.concatenate([kn, k_rope], axis=-1).astype(BF16)
        vm_ref[hh] = kv[:, (N_HEADS + hh) * HEAD_DIM:(N_HEADS + hh + 1) * HEAD_DIM].astype(BF16)
        qdh = _rope_lanes(y[:, qd0 + hh * HEAD_DIM:qd0 + (hh + 1) * HEAD_DIM], cos, sin, HEAD_DIM // 4)
        qd_ref[hh // rep, hh % rep] = (qdh * scale_d).astype(BF16)
    for g in range(N_KV_HEADS):
        kdh = _rope_lanes(y[:, kd0 + g * HEAD_DIM:kd0 + (g + 1) * HEAD_DIM], cos, sin, HEAD_DIM // 4)
        kd_ref[g] = kdh.astype(BF16)
        vd_ref[g] = y[:, vd0 + g * HEAD_DIM:vd0 + (g + 1) * HEAD_DIM].astype(BF16)


def inproj_cd(x, modtab, w_in, q_ln, kv_ln, w_uq, w_ukv, cos, sin, cosm, sinm, n_ctx):
    b, t, d = x.shape
    n_in = w_in.shape[1]
    q_lora, kv_lora = q_ln.shape[0], kv_ln.shape[0]
    rep = N_HEADS // N_KV_HEADS
    tq = ROW_TILE
    nct = n_ctx // tq
    seg = lambda i: jnp.minimum(i // nct, 1) if nct > 0 else 1
    kern = functools.partial(_inproj_cd_kernel, scale_c=(HEAD_DIM + C_ROPE) ** -0.5 * LOG2E,
                             scale_d=HEAD_DIM ** -0.5 * LOG2E, q_lora=q_lora, kv_lora=kv_lora)
    full = lambda shape: pl.BlockSpec(shape, lambda bi, i: (0,) * len(shape))
    tab = pl.BlockSpec((tq, LANES), lambda bi, i: (i, 0))
    return pl.pallas_call(
        kern,
        grid=(b, t // tq),
        in_specs=[
            pl.BlockSpec((None, tq, d), lambda bi, i: (bi, i, 0)),
            pl.BlockSpec((None, None, 6, d), lambda bi, i: (bi, seg(i), 0, 0)),
            full((d, n_in)), full((1, q_lora)), full((1, kv_lora)),
            full(w_uq.shape), full(w_ukv.shape), tab, tab, tab, tab,
        ],
        out_specs=[
            pl.BlockSpec((None, N_HEADS, 1, tq, MLA_QK), lambda bi, i: (bi, 0, 0, i, 0)),
            pl.BlockSpec((None, N_HEADS, tq, MLA_QK), lambda bi, i: (bi, 0, i, 0)),
            pl.BlockSpec((None, N_HEADS, tq, HEAD_DIM), lambda bi, i: (bi, 0, i, 0)),
            pl.BlockSpec((None, N_KV_HEADS, rep, tq, HEAD_DIM), lambda bi, i: (bi, 0, 0, i, 0)),
            pl.BlockSpec((None, N_KV_HEADS, tq, HEAD_DIM), lambda bi, i: (bi, 0, i, 0)),
            pl.BlockSpec((None, N_KV_HEADS, tq, HEAD_DIM), lambda bi, i: (bi, 0, i, 0)),
        ],
        out_shape=[
            jax.ShapeDtypeStruct((b, N_HEADS, 1, t, MLA_QK), BF16),
            jax.ShapeDtypeStruct((b, N_HEADS, t, MLA_QK), BF16),
            jax.ShapeDtypeStruct((b, N_HEADS, t, HEAD_DIM), BF16),
            jax.ShapeDtypeStruct((b, N_KV_HEADS, rep, t, HEAD_DIM), BF16),
            jax.ShapeDtypeStruct((b, N_KV_HEADS, t, HEAD_DIM), BF16),
            jax.ShapeDtypeStruct((b, N_KV_HEADS, t, HEAD_DIM), BF16),
        ],
        compiler_params=_cparams(("parallel", "parallel")),
        name="inproj_cd",
    )(x, modtab, w_in, q_ln.reshape(1, -1), kv_ln.reshape(1, -1), w_uq, w_ukv, cos, sin, cosm, sinm)


def _prep_cd_weights(w_in, w_uq, w_ukv):
    d = w_in.shape[0]
    q_lora = w_uq.shape[0]
    kv_lora = w_ukv.shape[0]
    o = q_lora + kv_lora
    k_rope = w_in[:, o:o + C_ROPE]
    rest = w_in[:, o + C_ROPE:]
    w_in_r = jnp.concatenate([w_in[:, :o], rest, k_rope, jnp.zeros((d, LANES - C_ROPE), w_in.dtype)], axis=1)
    uq = w_uq.reshape(q_lora, N_HEADS, HEAD_DIM + C_ROPE)
    uq = jnp.pad(uq, ((0, 0), (0, 0), (0, MLA_QK - HEAD_DIM - C_ROPE))).reshape(q_lora, N_HEADS * MLA_QK)
    ukv = w_ukv.reshape(kv_lora, N_HEADS, 2, HEAD_DIM).transpose(0, 2, 1, 3).reshape(kv_lora, 2 * N_HEADS * HEAD_DIM)
    return w_in_r.astype(BF16), uq.astype(BF16), ukv.astype(BF16)


def _softmax_pv(blocks, sink_col):
    mx = None
    for s, _ in blocks:
        bm = jnp.max(s, axis=-1, keepdims=True)
        mx = bm if mx is None else jnp.maximum(mx, bm)
    if sink_col is not None:
        mx = jnp.maximum(mx, sink_col)
    den = None
    acc = None
    for s, v in blocks:
        p = jnp.exp2(s - mx)
        ps = jnp.sum(p, axis=-1, keepdims=True)
        den = ps if den is None else den + ps
        pv = _dot(p.astype(BF16), v)
        acc = pv if acc is None else acc + pv
    if sink_col is not None:
        den = den + jnp.exp2(sink_col - mx)
    return acc / den


def _online_softmax_pv(q, k_ref, v_ref, sink_col):
    n_keys = k_ref.shape[0]
    m = sink_col if sink_col is not None else jnp.full((q.shape[0], 1), -jnp.inf, F32)
    den = None
    acc = None
    for c0 in range(0, n_keys, KEY_CHUNK):
        c1 = min(c0 + KEY_CHUNK, n_keys)
        s = _dot_nt(q, k_ref[c0:c1, :])
        m_new = jnp.maximum(m, jnp.max(s, axis=-1, keepdims=True))
        p = jnp.exp2(s - m_new)
        ps = jnp.sum(p, axis=-1, keepdims=True)
        pv = _dot(p.astype(BF16), v_ref[c0:c1, :])
        if acc is None:
            den, acc = ps, pv
        else:
            corr = jnp.exp2(m - m_new)
            den = den * corr + ps
            acc = acc * corr + pv
        m = m_new
    if sink_col is not None:
        den = den + jnp.exp2(sink_col - m)
    return acc / den


def _attn_kernel(sink_ref, q_ref, k_ref, v_ref, o_ref, *, n_ctx, tq, rep, hps, windowed, use_sink):
    gi = pl.program_id(1)
    qi = pl.program_id(2)
    t_all = k_ref.shape[1]
    dk = q_ref.shape[-1]
    dv = v_ref.shape[-1]
    nct = n_ctx // tq

    def head(h, ctx_tile):
        g = gi * hps + h
        q = q_ref[h].reshape(rep * tq, dk)
        kh = k_ref.at[h]
        vh = v_ref.at[h]
        if use_sink:
            row = lax.broadcasted_iota(I32, (rep * tq, 1), 0)
            sink_col = jnp.full((rep * tq, 1), sink_ref[g * rep], F32)
            for r in range(1, rep):
                sink_col = jnp.where(row >= r * tq, sink_ref[g * rep + r], sink_col)
            sink_col = sink_col * LOG2E
        else:
            sink_col = None
        if ctx_tile:
            o = _softmax_pv([(_dot_nt(q, kh[0:n_ctx, :]), vh[0:n_ctx, :])], sink_col)
        elif not windowed:
            o = _online_softmax_pv(q, kh, vh, sink_col)
        else:
            band = tq + 2 * D_WINDOW
            s0 = (qi - nct) * tq
            kstart = jnp.clip(n_ctx + s0 - D_WINDOW, n_ctx, t_all - band)
            kstart = pl.multiple_of(kstart, LANES)
            sb = _dot_nt(q, kh[pl.ds(kstart, band), :])
            rowq = lax.broadcasted_iota(I32, (rep * tq, band), 0) % tq + s0
            colk = lax.broadcasted_iota(I32, (rep * tq, band), 1) + (kstart - n_ctx)
            sb = jnp.where(jnp.abs(colk - rowq) <= D_WINDOW, sb, NEG_INF)
            o = _softmax_pv([(_dot_nt(q, kh[0:n_ctx, :]), vh[0:n_ctx, :]), (sb, vh[pl.ds(kstart, band), :])],
                            sink_col)
        for r in range(rep):
            c0 = (h * rep + r) * dv
            o_ref[:, c0:c0 + dv] = o[r * tq:(r + 1) * tq].astype(o_ref.dtype)

    @pl.when(qi < nct)
    def _():
        for h in range(hps):
            head(h, True)

    @pl.when(qi >= nct)
    def _():
        for h in range(hps):
            head(h, False)


def attention(q, k, v, n_ctx, *, windowed=False, sink=None, hps=ATTN_HEADS_PER_STEP):
    b, g, rep, t, dk = q.shape
    dv = v.shape[-1]
    tq = ROW_TILE
    use_sink = sink is not None
    if sink is None:
        sink = jnp.zeros((g * rep,), F32)
    kern = functools.partial(_attn_kernel, n_ctx=n_ctx, tq=tq, rep=rep, hps=hps, windowed=windowed,
                             use_sink=use_sink)
    return pl.pallas_call(
        kern,
        grid=(b, g // hps, t // tq),
        in_specs=[
            pl.BlockSpec(memory_space=pltpu.SMEM),
            pl.BlockSpec((None, hps, rep, tq, dk), lambda bi, gi, i: (bi, gi, 0, i, 0)),
            pl.BlockSpec((None, hps, t, dk), lambda bi, gi, i: (bi, gi, 0, 0)),
            pl.BlockSpec((None, hps, t, dv), lambda bi, gi, i: (bi, gi, 0, 0)),
        ],
        out_specs=pl.BlockSpec((None, tq, hps * rep * dv), lambda bi, gi, i: (bi, i, gi)),
        out_shape=jax.ShapeDtypeStruct((b, t, g * rep * dv), BF16),
        compiler_params=_cparams(("parallel", "parallel", "arbitrary")),
        name="attention_win" if windowed else "attention",
    )(sink.astype(F32), q, k, v)


def _s5_tables(lam_re, lam_im, log_dt, b_re, b_im, c_re, c_im, d_skip):
    n_groups, n_state = lam_re.shape[1:]
    L, gs = S5_CHUNK, S5_GROUP
    lam = lax.complex(lam_re.astype(F32), lam_im.astype(F32))
    dt = jnp.exp(log_dt.astype(F32))[..., None]
    lam_dt = lam * dt
    lam_bar = jnp.exp(lam_dt)
    b_bar = ((lam_bar - 1.0) / lam)[..., None] * lax.complex(b_re.astype(F32), b_im.astype(F32))
    c_mat = lax.complex(c_re.astype(F32), c_im.astype(F32))
    pw = jnp.exp(lam_dt[None] * jnp.arange(L + 1, dtype=F32)[:, None, None, None])
    kker = jnp.einsum("dgop,tdgp,dgpi->dtgoi", c_mat, pw[:L], b_bar).real
    s_in = jnp.arange(L)[:, None]
    s_out = jnp.arange(L)[None, :]
    tau_f = s_out - s_in
    tau_r = s_in - s_out
    kf = jnp.where((tau_f >= 0)[:, :, None, None, None], kker[0][jnp.clip(tau_f, 0, L - 1)], 0.0)
    kr = jnp.where((tau_r >= 0)[:, :, None, None, None], kker[1][jnp.clip(tau_r, 0, L - 1)], 0.0)
    kt = kf + kr
    gp = S5_PACK
    n_packs = n_groups // gp
    eye = jnp.eye(gp, dtype=F32)
    t6 = kt.transpose(2, 0, 4, 1, 3).reshape(n_packs, gp, L, gs, L, gs)
    d6 = d_skip.astype(F32).reshape(n_packs, gp, 1, gs, 1, 1) * (
        jnp.eye(L, dtype=F32)[None, None, :, None, :, None] * jnp.eye(gs, dtype=F32)[None, None, None, :, None, :])
    tmat = jnp.einsum("pgsctd,gh->psgcthd", t6 + d6, eye).reshape(n_packs, L * LANES, L * LANES)
    steps = jnp.arange(L, dtype=F32)[:, None, None]
    wf = jnp.exp(lam_dt[0][None] * (L - 1 - steps))[:, :, :, None] * b_bar[0][None]
    wr = jnp.exp(lam_dt[1][None] * steps)[:, :, :, None] * b_bar[1][None]
    def w_pack(w):
        w2 = jnp.stack([w.real, w.imag], axis=0).reshape(2, L, n_packs, gp, n_state, gs)
        return jnp.einsum("bspgqc,gh->psgcbhq", w2, eye).reshape(n_packs, L * LANES, 2 * gp * n_state)
    vf = c_mat[0][None] * jnp.exp(lam_dt[0][None] * (steps + 1))[:, :, None, :]
    vr = c_mat[1][None] * jnp.exp(lam_dt[1][None] * (L - steps))[:, :, None, :]
    def v_pack(vv):
        v2 = jnp.stack([vv.real, -vv.imag], axis=0).reshape(2, L, n_packs, gp, gs, n_state)
        return jnp.einsum("bspgcq,gh->pbgqshc", v2, eye).reshape(n_packs, 2 * gp * n_state, L * LANES)
    def lam_pack(l):
        return jnp.stack([l.real, l.imag], axis=0).reshape(2, n_packs, gp * n_state).transpose(1, 0, 2)
    lam_l = pw[L]
    return (tmat.astype(BF16),
            (w_pack(wf).astype(BF16), v_pack(vf).astype(BF16), lam_pack(lam_l[0])),
            (w_pack(wr).astype(BF16), v_pack(vr).astype(BF16), lam_pack(lam_l[1])))


def _s5_pass_kernel(*refs, reverse):
    if reverse:
        u_ref, yin_ref, w_ref, v_ref, lam_ref, y_ref, s_scr, h_scr = refs
    else:
        u_ref, t_ref, w_ref, v_ref, lam_ref, y_ref, s_scr, h_scr = refs

    @pl.when(pl.program_id(1) == 0)
    def _():
        h_scr[...] = jnp.zeros_like(h_scr)

    nchs, L, b, _ = u_ref.shape
    rows = nchs * b
    half = h_scr.shape[1] // 2
    xg = jnp.concatenate([u_ref[:, s].reshape(rows, LANES) for s in range(L)], axis=-1)
    s_scr[...] = _dot(xg, w_ref[...])
    lr = jnp.broadcast_to(lam_ref[0:1, :], (b, half))
    li = jnp.broadcast_to(lam_ref[1:2, :], (b, half))

    def step(k, carry):
        hr, hi = carry
        j = nchs - 1 - k if reverse else k
        r0 = pl.multiple_of(j * b, b)
        sr = s_scr[pl.ds(r0, b), 0:half]
        si = s_scr[pl.ds(r0, b), half:]
        s_scr[pl.ds(r0, b), 0:half] = hr
        s_scr[pl.ds(r0, b), half:] = hi
        return lr * hr - li * hi + sr, lr * hi + li * hr + si

    hr, hi = lax.fori_loop(0, nchs, step, (h_scr[:, 0:half], h_scr[:, half:]))
    h_scr[:, 0:half] = hr
    h_scr[:, half:] = hi
    y = _dot(s_scr[...].astype(BF16), v_ref[...])
    if not reverse:
        y = y + _dot(xg, t_ref[...])
    for s in range(L):
        blk = y[:, s * LANES:(s + 1) * LANES].reshape(nchs, b, LANES)
        if reverse:
            blk = blk + yin_ref[:, s]
        y_ref[:, s] = blk


def s5_apply(u, tables, n_ctx):
    tmat, fwd, rev = tables
    b, t, ch = u.shape
    L = S5_CHUNK
    n_packs = ch // LANES
    seg = ROW_TILE
    nseg, nct, nchs = t // seg, n_ctx // seg, seg // L
    width = L * LANES
    n_state2 = fwd[0].shape[-1]
    ut = u.transpose(1, 0, 2).reshape(t // L, L, b, ch)
    blk = pl.BlockSpec((nchs, L, b, LANES), lambda p, i: (i, 0, 0, p))
    rseg = lambda i: jnp.where(i < nct, nct - 1 - i, nseg - 1 - (i - nct))
    rblk = pl.BlockSpec((nchs, L, b, LANES), lambda p, i: (rseg(i), 0, 0, p))
    tab = lambda shape: pl.BlockSpec((None,) + shape, lambda p, i: (p, 0, 0))
    scratch = [pltpu.VMEM((nchs * b, n_state2), F32), pltpu.VMEM((b, n_state2), F32)]
    y_shape = jax.ShapeDtypeStruct((t // L, L, b, ch), F32)
    y_f = pl.pallas_call(
        functools.partial(_s5_pass_kernel, reverse=False),
        grid=(n_packs, nseg),
        in_specs=[blk, tab((width, width)), tab((width, n_state2)), tab((n_state2, width)), tab((2, n_state2 // 2))],
        out_specs=blk,
        out_shape=y_shape,
        scratch_shapes=scratch,
        compiler_params=_cparams(("parallel", "arbitrary")),
        name="s5_forward",
    )(ut, tmat, *fwd)
    y = pl.pallas_call(
        functools.partial(_s5_pass_kernel, reverse=True),
        grid=(n_packs, nseg),
        in_specs=[rblk, rblk, tab((width, n_state2)), tab((n_state2, width)), tab((2, n_state2 // 2))],
        out_specs=rblk,
        out_shape=y_shape,
        scratch_shapes=scratch,
        input_output_aliases={1: 0},
        compiler_params=_cparams(("parallel", "arbitrary")),
        name="s5_reverse",
    )(ut, y_f, *rev)
    return y.reshape(t, b, ch).transpose(1, 0, 2)


def _gelu_tanh(x):
    return 0.5 * x * (1.0 + jnp.tanh(math.sqrt(2.0 / math.pi) * (x + 0.044715 * (x * x * x))))


def _post_kernel(*refs, alpha, glu, n_exp):
    if glu:
        (x_ref, a_ref, s_ref, wg_ref, bg_ref, wm_ref, mod_ref, g_ref, b_ref, wrh_ref, wrl_ref, br_ref,
         x1_ref, h2_ref, idx_ref, gate_ref, rank_ref, cnt_ref, cnt_scr) = refs
    else:
        (x_ref, a_ref, s_ref, wm_ref, mod_ref, g_ref, b_ref, wrh_ref, wrl_ref, br_ref,
         x1_ref, h2_ref, idx_ref, gate_ref, rank_ref, cnt_ref, cnt_scr) = refs
    first = (pl.program_id(0) == 0) & (pl.program_id(1) == 0)

    @pl.when(first)
    def _():
        cnt_scr[...] = jnp.zeros_like(cnt_scr)

    tq, d = x_ref.shape
    half = a_ref.shape[1]
    if glu:
        z = _gelu_tanh(s_ref[...])
        gate = jax.nn.sigmoid(_dot(z.astype(BF16), wg_ref[...]) + bg_ref[...])
        second = (z * gate).astype(BF16)
    else:
        second = s_ref[...]
    mix = _dot(a_ref[...], wm_ref[0:half, :]) + _dot(second, wm_ref[half:, :])
    x1 = _layer_norm(alpha * x_ref[...] + mod_ref[2:3, :] * mix, g_ref[...], b_ref[...])
    x1_ref[...] = x1
    h2 = x1 * (1.0 + mod_ref[4:5, :]) + mod_ref[3:4, :]
    _store_token_tiles(h2_ref, h2)

    h_hi, h_lo = _split_bf16(h2)
    logits = _dot_nt(wrh_ref[...], h_hi) + _dot_nt(wrh_ref[...], h_lo) + _dot_nt(wrl_ref[...], h_hi) + br_ref[...]
    eidx = lax.broadcasted_iota(I32, (n_exp, tq), 0)
    work = logits
    tops, sels = [], []
    for k in range(TOP_K):
        m = jnp.max(work, axis=0, keepdims=True)
        ik = jnp.min(jnp.where(work == m, eidx, n_exp), axis=0, keepdims=True)
        sel = eidx == ik
        work = jnp.where(sel, -jnp.inf, work)
        tops.append(m)
        sels.append(sel)
        idx_ref[k:k + 1, :] = ik
    exps = [jnp.exp(tk - tops[0]) for tk in tops]
    den = exps[0] + exps[1] + exps[2] + exps[3]
    for k in range(TOP_K):
        gate_ref[k:k + 1, :] = exps[k] / den
    onehot = jnp.zeros((n_exp, tq), F32)
    for sel in sels:
        onehot = onehot + sel.astype(F32)
    tri = (lax.broadcasted_iota(I32, (tq, tq), 0) < lax.broadcasted_iota(I32, (tq, tq), 1)).astype(BF16)
    before = _dot(onehot.astype(BF16), tri) + cnt_scr[...]
    for k in range(TOP_K):
        rk = jnp.sum(jnp.where(sels[k], before, 0.0), axis=0, keepdims=True)
        rank_ref[k:k + 1, :] = rk.astype(I32)
    cnt_scr[...] += jnp.sum(onehot, axis=1, keepdims=True)
    cnt_ref[...] = jnp.broadcast_to(cnt_scr[...], cnt_ref.shape).astype(I32)


def post_mixer(x, a, s, modtab, w_mix, ln_g, ln_b, w_router, b_router, n_ctx, alpha, glu_w=None, glu_b=None):
    b, t, d = x.shape
    half = a.shape[-1]
    n_exp = w_router.shape[1]
    tq = ROW_TILE
    nct = n_ctx // tq
    seg = lambda i: jnp.minimum(i // nct, 1) if nct > 0 else 1
    glu = glu_w is not None
    wr_hi, wr_lo = _split_bf16(w_router.T.astype(F32))
    full = lambda shape: pl.BlockSpec(shape, lambda bi, i: (0,) * len(shape))
    tok = lambda width: pl.BlockSpec((None, tq, width), lambda bi, i: (bi, i, 0))
    in_specs = [tok(d), tok(half), tok(half)]
    args = [x, a, s]
    if glu:
        in_specs += [full((half, half)), full((1, half))]
        args += [glu_w.astype(BF16), glu_b.reshape(1, half)]
    in_specs += [full((d, d)), pl.BlockSpec((None, None, 6, d), lambda bi, i: (bi, seg(i), 0, 0)),
                 full((1, d)), full((1, d)), full((n_exp, d)), full((n_exp, d)), full((n_exp, 1))]
    args += [w_mix.astype(BF16), modtab, ln_g.reshape(1, d), ln_b.reshape(1, d), wr_hi, wr_lo,
             b_router.reshape(n_exp, 1)]
    nt = t // tq
    lane_out = lambda dt: (pl.BlockSpec((TOP_K, tq), lambda bi, i: (0, bi * nt + i)),
                           jax.ShapeDtypeStruct((TOP_K, b * t), dt))
    outs = [
        (tok(d), jax.ShapeDtypeStruct((b, t, d), F32)),
        (pl.BlockSpec((tq * SUBLANES, LANES), lambda bi, i: (bi * nt + i, 0)),
         jax.ShapeDtypeStruct((b * t * SUBLANES, LANES), F32)),
        lane_out(I32), lane_out(F32), lane_out(I32),
        (pl.BlockSpec((n_exp, LANES), lambda bi, i: (0, 0)), jax.ShapeDtypeStruct((n_exp, LANES), I32)),
    ]
    kern = functools.partial(_post_kernel, alpha=alpha, glu=glu, n_exp=n_exp)
    return pl.pallas_call(
        kern,
        grid=(b, nt),
        in_specs=in_specs,
        out_specs=[o[0] for o in outs],
        out_shape=[o[1] for o in outs],
        scratch_shapes=[pltpu.VMEM((n_exp, 1), F32)],
        compiler_params=_cparams(("arbitrary", "arbitrary")),
        name="post_mixer_glu" if glu else "post_mixer",
    )(*args)


def _route_kernel(start_ref, idx_ref, rank_ref, pos_ref, *, n_exp, tq):
    eidx = lax.broadcasted_iota(I32, (n_exp, tq), 0)
    for k in range(TOP_K):
        sel = eidx == idx_ref[k:k + 1, :]
        base = jnp.sum(jnp.where(sel, start_ref[...], 0.0), axis=0, keepdims=True)
        pos_ref[:, k * tq:(k + 1) * tq] = base.astype(I32) + rank_ref[k:k + 1, :]


def route_positions(group_start, idx, rank):
    n_exp = group_start.shape[0]
    n_tok = idx.shape[1]
    tq = ROW_TILE
    return pl.pallas_call(
        functools.partial(_route_kernel, n_exp=n_exp, tq=tq),
        grid=(n_tok // tq,),
        in_specs=[
            pl.BlockSpec((n_exp, 1), lambda i: (0, 0)),
            pl.BlockSpec((TOP_K, tq), lambda i: (0, i)),
            pl.BlockSpec((TOP_K, tq), lambda i: (0, i)),
        ],
        out_specs=pl.BlockSpec((None, 1, TOP_K * tq), lambda i: (i, 0, 0)),
        out_shape=jax.ShapeDtypeStruct((n_tok // tq, 1, TOP_K * tq), I32),
        compiler_params=_cparams(("parallel",)),
        name="route_positions",
    )(group_start.astype(F32).reshape(n_exp, 1), idx, rank)


def _dispatch_kernel(pos_hbm, h_ref, xs_hbm, idx_smem, idx_sem, row_sem, *, tq, n_tiles, tile_rows):
    i = pl.program_id(0)
    slot = i % 2

    def idx_copy(tile, sl):
        return pltpu.make_async_copy(pos_hbm.at[tile], idx_smem.at[sl], idx_sem.at[sl])

    @pl.when(i == 0)
    def _():
        idx_copy(0, 0).start()

    idx_copy(i, slot).wait()

    @pl.when(i + 1 < n_tiles)
    def _():
        idx_copy(i + 1, 1 - slot).start()

    for k in range(TOP_K):
        for r in range(tq):
            dst = pl.multiple_of(idx_smem[slot, 0, k * tq + r] * tile_rows, tile_rows)
            pltpu.make_async_copy(h_ref.at[pl.ds(r * tile_rows, tile_rows), :],
                                  xs_hbm.at[pl.ds(dst, tile_rows), :], row_sem.at[0]).start(priority=r % 2)
    for k in range(TOP_K):
        pltpu.make_async_copy(h_ref, xs_hbm.at[pl.ds(0, tq * tile_rows), :], row_sem.at[0]).wait()


def moe_dispatch(pos_tiles, h_tiles, tile_rows):
    n_tiles = pos_tiles.shape[0]
    tq = h_tiles.shape[0] // tile_rows // n_tiles
    return pl.pallas_call(
        functools.partial(_dispatch_kernel, tq=tq, n_tiles=n_tiles, tile_rows=tile_rows),
        grid=(n_tiles,),
        in_specs=[pl.BlockSpec(memory_space=pl.ANY),
                  pl.BlockSpec((tq * tile_rows, LANES), lambda i: (i, 0))],
        out_specs=pl.BlockSpec(memory_space=pl.ANY),
        out_shape=jax.ShapeDtypeStruct((TOP_K * h_tiles.shape[0], LANES), F32),
        scratch_shapes=[
            pltpu.SMEM((2, 1, TOP_K * tq), I32),
            pltpu.SemaphoreType.DMA((2,)),
            pltpu.SemaphoreType.DMA((1,)),
        ],
        compiler_params=_cparams(("arbitrary",)),
        name="moe_dispatch",
    )(pos_tiles, h_tiles)


def _moe_kernel(e_ref, j_ref, lo_ref, hi_ref, first_ref, new_ref, xs_hbm, wi_ref, bi_ref, wo_ref, bo_ref,
                y_ref, wi_b, wo_b, xbuf, x_sem, *, n_sub, n_items):
    i = pl.program_id(0)
    slot = i % 2
    lo = lo_ref[i]
    hi = hi_ref[i]
    tm = xbuf.shape[1]
    sub = tm // MOE_CHAINS

    def x_copies(item, sl):
        base = j_ref[item] * tm
        return [pltpu.make_async_copy(xs_hbm.at[pl.ds(base, tm), s, :],
                                      xbuf.at[sl, :, s * LANES:(s + 1) * LANES], x_sem.at[sl])
                for s in range(n_sub)]

    @pl.when(i == 0)
    def _():
        for cp in x_copies(0, 0):
            cp.start()

    @pl.when(i + 1 < n_items)
    def _():
        for cp in x_copies(i + 1, 1 - slot):
            cp.start()

    @pl.when(new_ref[i] == 1)
    def _():
        wi_b[...] = wi_ref[...].astype(BF16)
        wo_b[...] = wo_ref[...].astype(BF16)

    for cp in x_copies(i, slot):
        cp.wait()

    def expert_rows(r0):
        x = xbuf[slot, r0:r0 + sub, :].astype(BF16)
        z = _dot(x, wi_b[...]) + bi_ref[...]
        f = z.shape[1] // 2
        glu = jnp.minimum(z[:, :f], SWIGLU_LIMIT)
        lin = jnp.clip(z[:, f:], -SWIGLU_LIMIT, SWIGLU_LIMIT)
        act = glu * jax.nn.sigmoid(SWIGLU_ALPHA * glu) * (lin + 1.0)
        y = _dot(act.astype(BF16), wo_b[...]) + bo_ref[...]
        row = lax.broadcasted_iota(I32, (sub, 1), 0) + r0
        return y, (row >= lo) & (row < hi)

    def out_rows(r0):
        return y_ref.at[pl.ds(r0 * n_sub, sub * n_sub), :]

    @pl.when((hi > lo) & (first_ref[i] == 1))
    def _():
        for r0 in range(0, tm, sub):
            y, mine = expert_rows(r0)
            _store_token_tiles(out_rows(r0), jnp.where(mine, y, 0.0))

    @pl.when((hi > lo) & (first_ref[i] == 0))
    def _():
        for r0 in range(0, tm, sub):
            y, mine = expert_rows(r0)
            _store_token_tiles(out_rows(r0), jnp.where(mine, y, _load_token_tiles(out_rows(r0), 0, sub, n_sub)))


def moe_experts(xs, items, w_in, b_in, w_out, b_out, layer):
    _, n_exp, d, f2 = w_in.shape
    n_sub = d // LANES
    n_items = items[0].shape[0]
    tm = MOE_BLOCK
    wmap = lambda i, e, j, lo, hi, fi, nw: (layer, e[i], 0, 0)
    xmap = lambda i, e, j, lo, hi, fi, nw: (j[i], 0)
    grid_spec = pltpu.PrefetchScalarGridSpec(
        num_scalar_prefetch=6,
        grid=(n_items,),
        in_specs=[
            pl.BlockSpec(memory_space=pl.ANY),
            pl.BlockSpec((None, None, d, f2), wmap),
            pl.BlockSpec((None, None, 1, f2), wmap),
            pl.BlockSpec((None, None, f2 // 2, d), wmap),
            pl.BlockSpec((None, None, 1, d), wmap),
        ],
        out_specs=pl.BlockSpec((tm * n_sub, LANES), xmap),
        scratch_shapes=[pltpu.VMEM((d, f2), BF16), pltpu.VMEM((f2 // 2, d), BF16),
                        pltpu.VMEM((2, tm, d), F32), pltpu.SemaphoreType.DMA((2,))],
    )
    return pl.pallas_call(
        functools.partial(_moe_kernel, n_sub=n_sub, n_items=n_items),
        grid_spec=grid_spec,
        out_shape=jax.ShapeDtypeStruct(xs.shape, F32),
        compiler_params=_cparams(("arbitrary",)),
        name="moe_experts",
    )(*items, xs.reshape(-1, n_sub, LANES), w_in, b_in.reshape(b_in.shape[0], n_exp, 1, f2), w_out, b_out.reshape(b_out.shape[0], n_exp, 1, d))


def _combine_kernel(pos_hbm, x_ref, g_ref, mod_ref, lg_ref, lb_ref, ys_hbm, ys4_hbm, o_ref,
                    idx_smem, ybuf, idx_sem, row_sem, *, alpha, tq, n_tiles):
    i = pl.program_id(0)
    slot = i % 2
    d = x_ref.shape[1]

    def idx_copy(tile, sl):
        return pltpu.make_async_copy(pos_hbm.at[tile], idx_smem.at[sl], idx_sem.at[sl])

    n_sub = d // LANES

    def gather_start(sl):
        for r in range(TOP_K * tq):
            src = pl.multiple_of(idx_smem[sl, 0, r] * n_sub, n_sub)
            pltpu.make_async_copy(ys_hbm.at[pl.ds(src, n_sub), :], ybuf.at[sl, r // SUBLANES, :, r % SUBLANES, :],
                                  row_sem.at[sl]).start(priority=r % 2)

    def rows_wait(sl):
        pltpu.make_async_copy(ys4_hbm.at[pl.ds(0, TOP_K * tq // SUBLANES)], ybuf.at[sl], row_sem.at[sl]).wait()

    @pl.when(i == 0)
    def _():
        idx_copy(0, 0).start()
        idx_copy(0, 0).wait()
        gather_start(0)

        @pl.when(n_tiles > 1)
        def _():
            idx_copy(1, 1).start()

    @pl.when(i + 1 < n_tiles)
    def _():
        idx_copy(i + 1, 1 - slot).wait()
        gather_start(1 - slot)

    rows_wait(slot)

    @pl.when(i + 2 < n_tiles)
    def _():
        idx_copy(i + 2, slot).start()

    yb = ybuf.at[slot]
    gates = g_ref[...]
    y = None
    rt = tq // SUBLANES
    for k in range(TOP_K):
        rows_k = jnp.concatenate([yb[k * rt:(k + 1) * rt, c].reshape(tq, LANES) for c in range(n_sub)], axis=-1)
        term = gates[:, k:k + 1] * rows_k
        y = term if y is None else y + term
    o_ref[...] = _layer_norm(alpha * x_ref[...] + mod_ref[5:6, :] * y, lg_ref[...], lb_ref[...])


def moe_combine(x1, pos_tiles, gates_tok, modtab, ln_g, ln_b, ys, n_ctx, alpha, lat_only=False):
    b, t, d = x1.shape
    tq = ROW_TILE
    nt = t // tq
    n_tiles = b * nt
    nct = n_ctx // tq
    seg = lambda i: jnp.minimum((i % nt) // nct, 1) if nct > 0 else 1
    kern = functools.partial(_combine_kernel, alpha=alpha, tq=tq, n_tiles=n_tiles)
    if lat_only:
        nlt = nt - nct
        out_map = lambda i: ((i // nt) * nlt + jnp.maximum(i % nt - nct, 0), 0)
        out_rows = b * nlt * tq
    else:
        out_map = lambda i: (i, 0)
        out_rows = b * t
    out = pl.pallas_call(
        kern,
        grid=(n_tiles,),
        in_specs=[
            pl.BlockSpec(memory_space=pl.ANY),
            pl.BlockSpec((tq, d), lambda i: (i, 0)),
            pl.BlockSpec((tq, TOP_K), lambda i: (i, 0)),
            pl.BlockSpec((None, None, 6, d), lambda i: (i // nt, seg(i), 0, 0)),
            pl.BlockSpec((1, d), lambda i: (0, 0)),
            pl.BlockSpec((1, d), lambda i: (0, 0)),
            pl.BlockSpec(memory_space=pl.ANY),
            pl.BlockSpec(memory_space=pl.ANY),
        ],
        out_specs=pl.BlockSpec((tq, d), out_map),
        out_shape=jax.ShapeDtypeStruct((out_rows, d), F32),
        scratch_shapes=[
            pltpu.SMEM((2, 1, TOP_K * tq), I32),
            pltpu.VMEM((2, TOP_K * tq // SUBLANES, d // LANES, SUBLANES, LANES), F32),
            pltpu.SemaphoreType.DMA((2,)),
            pltpu.SemaphoreType.DMA((2,)),
        ],
        compiler_params=_cparams(("arbitrary",)),
        name="moe_combine",
    )(pos_tiles, x1.reshape(b * t, d), gates_tok, modtab, ln_g.reshape(1, d), ln_b.reshape(1, d), ys,
      ys.reshape(-1, d // LANES, SUBLANES, LANES))
    return out.reshape(b, out_rows // b, d)


def _group_items(counts, n_slots, tm):
    n_exp = counts.shape[0]
    le = (jnp.arange(n_exp)[:, None] <= jnp.arange(n_exp)[None, :]).astype(I32)
    g_end = counts @ le
    g_start = g_end - counts
    first_blk = g_start // tm
    n_blk = jnp.where(counts > 0, (g_end - 1) // tm - first_blk + 1, 0)
    i_end = n_blk @ le
    i_start = i_end - n_blk
    n_items = n_slots // tm + n_exp - 1
    it = jnp.arange(n_items, dtype=I32)
    valid = it < i_end[-1]
    e = jnp.minimum(jnp.sum((i_end[None, :] <= it[:, None]).astype(I32), axis=1), n_exp - 1)
    onehot = (e[:, None] == jnp.arange(n_exp)[None, :]).astype(I32)
    pick = lambda tbl: onehot @ tbl
    blk = pick(first_blk) + it - pick(i_start)
    lo = jnp.clip(pick(g_start) - blk * tm, 0, tm)
    hi = jnp.clip(pick(g_end) - blk * tm, 0, tm)
    last = jnp.maximum(i_end[-1] - 1, 0)
    e_last = jnp.sum(jnp.where(it == last, e, 0))
    e = jnp.where(valid, e, e_last)
    blk = jnp.where(valid, blk, n_slots // tm - 1)
    lo = jnp.where(valid, lo, 0)
    hi = jnp.where(valid, hi, 0)
    prev = lambda a: jnp.concatenate([jnp.full((1,), -1, I32), a[:-1]])
    first = (blk != prev(blk)).astype(I32)
    new_e = (e != prev(e)).astype(I32)
    return g_start, tuple(a.astype(I32) for a in (e, blk, lo, hi, first, new_e))


def kernel(x, c, ctx, c_ctx, w_mod, b_mod, ln_g, ln_b, w_mix_out, w_router, b_router, w_exp_in, b_exp_in, w_exp_out, b_exp_out, w_in_ab, a_q_norm, a_k_norm, s5_lam_re, s5_lam_im, s5_log_dt, s5_b_re, s5_b_im, s5_c_re, s5_c_im, s5_d, w_glu, b_glu, w_in_cd, c_q_norm, c_kv_norm, w_uq, w_ukv, d_sink):
    batch, n_lat, d = x.shape
    n_ctx = ctx.shape[1]
    depth = w_mod.shape[0]
    alpha = (2.0 * depth) ** 0.25
    t = n_ctx + n_lat
    n_tok = batch * t

    xs = jnp.concatenate([ctx, x], axis=1)
    n_rows = -(-(batch + 1) // SUBLANES) * SUBLANES
    cvec = jnp.zeros((n_rows, d), F32).at[:batch].set(c).at[batch].set(c_ctx)
    mods = modulation(cvec, w_mod, b_mod).reshape(depth, n_rows, 6, d)
    modtab = jnp.stack([jnp.broadcast_to(mods[:, batch:batch + 1], (depth, batch, 6, d)), mods[:, :batch]],
                       axis=2)

    cos_h, sin_h = _rope_tables(n_ctx, n_lat, HEAD_DIM)
    cos_m, sin_m = _rope_tables(n_ctx, n_lat, C_ROPE)

    for layer in range(depth):
        i = layer // 2
        mt = modtab[layer]
        if layer % 2 == 0:
            q, k, v, u = inproj_ab(xs, mt, w_in_ab[i].astype(BF16), a_q_norm[i], a_k_norm[i], cos_h, sin_h, n_ctx)
            att = attention(q, k, v, n_ctx)
            tables = _s5_tables(s5_lam_re[i], s5_lam_im[i], s5_log_dt[i], s5_b_re[i], s5_b_im[i],
                                s5_c_re[i], s5_c_im[i], s5_d[i])
            second = s5_apply(u, tables, n_ctx)
            glu_w, glu_b = w_glu[i], b_glu[i]
        else:
            w_in_r, uq_r, ukv_r = _prep_cd_weights(w_in_cd[i], w_uq[i], w_ukv[i])
            qm, km, vm, qd, kd, vd = inproj_cd(xs, mt, w_in_r, c_q_norm[i], c_kv_norm[i], uq_r, ukv_r,
                                               cos_h, sin_h, cos_m, sin_m, n_ctx)
            att = attention(qm, km, vm, n_ctx, hps=N_HEADS)
            second = attention(qd, kd, vd, n_ctx, windowed=True, sink=d_sink[i])
            glu_w = glu_b = None
        x1, h2, idx, gates, rank, counts = post_mixer(
            xs, att, second, mt, w_mix_out[layer], ln_g[layer, 0], ln_b[layer, 0],
            w_router[layer], b_router[layer], n_ctx, alpha, glu_w, glu_b)
        g_start, items = _group_items(counts[:, 0], TOP_K * n_tok, MOE_BLOCK)
        pos_tiles = route_positions(g_start, idx, rank)
        sorted_rows = moe_dispatch(pos_tiles, h2, d // LANES)
        ys = moe_experts(sorted_rows, items, w_exp_in, b_exp_in, w_exp_out, b_exp_out, layer)
        xs = moe_combine(x1, pos_tiles, gates.T, mt, ln_g[layer, 1], ln_b[layer, 1], ys, n_ctx, alpha,
                         lat_only=layer == depth - 1)
    return xs
```

```python
import functools
import math

import jax
import jax.numpy as jnp
from jax import lax
from jax.experimental import pallas as pl
from jax.experimental.pallas import tpu as pltpu

F32 = jnp.float32
BF16 = jnp.bfloat16
I32 = jnp.int32

GRID_W = 64
ROPE_THETA = 10000.0
HEAD_DIM = 128
N_HEADS = 4
N_KV_HEADS = 2
C_ROPE = 64
MLA_QK = 256
D_WINDOW = 128
S5_GROUP = 16
S5_STATE = 64
S5_CHUNK = 8
S5_PACK = 8
TOP_K = 4
SWIGLU_LIMIT = 7.0
SWIGLU_ALPHA = 1.702
NEG_INF = -1e30
LOG2E = 1.4426950408889634
LN_EPS = 1e-5
RMS_EPS = 1e-6

LANES = 128
SUBLANES = 8
VMEM_LIMIT_BYTES = 56 * 1024 * 1024

ROW_TILE = 256
MOE_BLOCK = 512
MOE_CHAINS = 2
KEY_CHUNK = 768
ATTN_HEADS_PER_STEP = 2


def _cparams(sem):
    return pltpu.CompilerParams(dimension_semantics=sem, vmem_limit_bytes=VMEM_LIMIT_BYTES)


def _split_bf16(a):
    hi = a.astype(BF16)
    lo = (a - hi.astype(F32)).astype(BF16)
    return hi, lo


def _dot(a, b):
    return jnp.dot(a, b, preferred_element_type=F32)


def _dot_nt(a, b):
    return lax.dot_general(a, b, (((1,), (1,)), ((), ())), preferred_element_type=F32)


def _store_token_tiles(ref, val):
    rows, d = val.shape
    n = d // LANES
    for s in range(n):
        ref[pl.ds(s, rows, stride=n), :] = val[:, s * LANES:(s + 1) * LANES]


def _load_token_tiles(ref, first_row, rows, n):
    return jnp.concatenate([ref[pl.ds(first_row + s, rows, stride=n), :] for s in range(n)], axis=-1)


def _layer_norm(x, g, b):
    mu = jnp.mean(x, axis=-1, keepdims=True)
    xc = x - mu
    var = jnp.mean(xc * xc, axis=-1, keepdims=True)
    return xc * lax.rsqrt(var + LN_EPS) * g + b


def _rms(x, g):
    return x * lax.rsqrt(jnp.mean(x * x, axis=-1, keepdims=True) + RMS_EPS) * g


def _mod_kernel(c_ref, w_ref, b_ref, o_ref):
    c = c_ref[...]
    s = c * jax.nn.sigmoid(c)
    s_hi, s_lo = _split_bf16(s)
    w_hi, w_lo = _split_bf16(w_ref[...])
    o_ref[...] = _dot(s_hi, w_hi) + _dot(s_lo, w_hi) + _dot(s_hi, w_lo) + b_ref[...]


def modulation(cvec, w_mod, b_mod):
    n_layers, d, d6 = w_mod.shape
    r = cvec.shape[0]
    tn = 1536
    return pl.pallas_call(
        _mod_kernel,
        grid=(n_layers, d6 // tn),
        in_specs=[
            pl.BlockSpec((r, d), lambda l, j: (0, 0)),
            pl.BlockSpec((None, d, tn), lambda l, j: (l, 0, j)),
            pl.BlockSpec((None, 1, tn), lambda l, j: (l, 0, j)),
        ],
        out_specs=pl.BlockSpec((None, r, tn), lambda l, j: (l, 0, j)),
        out_shape=jax.ShapeDtypeStruct((n_layers, r, d6), F32),
        compiler_params=_cparams(("arbitrary", "arbitrary")),
        name="modulation",
    )(cvec, w_mod, b_mod.reshape(n_layers, 1, d6))


def _rope_lanes(t, cos, sin, quarter):
    lane = lax.broadcasted_iota(I32, t.shape, 1)
    first = (lane % (2 * quarter)) < quarter
    partner = jnp.where(first, pltpu.roll(t, LANES - quarter, 1), pltpu.roll(t, quarter, 1))
    return t * cos + partner * sin


def _rope_tables(n_ctx, n_lat, rot_dim):
    quarter = rot_dim // 4
    pos = jnp.arange(n_lat, dtype=F32)
    row = jnp.floor(pos / GRID_W)
    col = pos - row * GRID_W
    inv_freq = ROPE_THETA ** (-jnp.arange(quarter, dtype=F32) / quarter)
    ang_r = row[:, None] * inv_freq
    ang_c = col[:, None] * inv_freq
    cos = jnp.concatenate([jnp.cos(ang_r)] * 2 + [jnp.cos(ang_c)] * 2, axis=-1)
    sin = jnp.concatenate([-jnp.sin(ang_r), jnp.sin(ang_r), -jnp.sin(ang_c), jnp.sin(ang_c)], axis=-1)
    pad = LANES - rot_dim
    cos = jnp.pad(cos, ((n_ctx, 0), (0, pad)), constant_values=1.0)
    sin = jnp.pad(sin, ((n_ctx, 0), (0, pad)))
    return cos, sin


def _inproj_ab_kernel(x_ref, mod_ref, w_ref, qn_ref, kn_ref, cos_ref, sin_ref,
                      q_ref, k_ref, v_ref, u_ref, *, scale):
    x = x_ref[...]
    h = (x * (1.0 + mod_ref[1:2, :]) + mod_ref[0:1, :]).astype(BF16)
    y = _dot(h, w_ref[...])
    cos = cos_ref[...]
    sin = sin_ref[...]
    rep = N_HEADS // N_KV_HEADS
    for hh in range(N_HEADS):
        qh = _rms(y[:, hh * HEAD_DIM:(hh + 1) * HEAD_DIM], qn_ref[...])
        qh = _rope_lanes(qh, cos, sin, HEAD_DIM // 4) * scale
        q_ref[hh // rep, hh % rep] = qh.astype(BF16)
    k0 = N_HEADS * HEAD_DIM
    v0 = k0 + N_KV_HEADS * HEAD_DIM
    for g in range(N_KV_HEADS):
        kh = _rms(y[:, k0 + g * HEAD_DIM:k0 + (g + 1) * HEAD_DIM], kn_ref[...])
        k_ref[g] = _rope_lanes(kh, cos, sin, HEAD_DIM // 4).astype(BF16)
        v_ref[g] = y[:, v0 + g * HEAD_DIM:v0 + (g + 1) * HEAD_DIM].astype(BF16)
    u_ref[...] = y[:, v0 + N_KV_HEADS * HEAD_DIM:].astype(BF16)


def inproj_ab(x, modtab, w_in, q_norm, k_norm, cos, sin, n_ctx):
    b, t, d = x.shape
    n_in = w_in.shape[1]
    s5_ch = n_in - (N_HEADS + 2 * N_KV_HEADS) * HEAD_DIM
    rep = N_HEADS // N_KV_HEADS
    tq = ROW_TILE
    nct = n_ctx // tq
    seg = lambda i: jnp.minimum(i // nct, 1) if nct > 0 else 1
    kern = functools.partial(_inproj_ab_kernel, scale=HEAD_DIM ** -0.5 * LOG2E)
    return pl.pallas_call(
        kern,
        grid=(b, t // tq),
        in_specs=[
            pl.BlockSpec((None, tq, d), lambda bi, i: (bi, i, 0)),
            pl.BlockSpec((None, None, 6, d), lambda bi, i: (bi, seg(i), 0, 0)),
            pl.BlockSpec((d, n_in), lambda bi, i: (0, 0)),
            pl.BlockSpec((1, HEAD_DIM), lambda bi, i: (0, 0)),
            pl.BlockSpec((1, HEAD_DIM), lambda bi, i: (0, 0)),
            pl.BlockSpec((tq, LANES), lambda bi, i: (i, 0)),
            pl.BlockSpec((tq, LANES), lambda bi, i: (i, 0)),
        ],
        out_specs=[
            pl.BlockSpec((None, N_KV_HEADS, rep, tq, HEAD_DIM), lambda bi, i: (bi, 0, 0, i, 0)),
            pl.BlockSpec((None, N_KV_HEADS, tq, HEAD_DIM), lambda bi, i: (bi, 0, i, 0)),
            pl.BlockSpec((None, N_KV_HEADS, tq, HEAD_DIM), lambda bi, i: (bi, 0, i, 0)),
            pl.BlockSpec((None, tq, s5_ch), lambda bi, i: (bi, i, 0)),
        ],
        out_shape=[
            jax.ShapeDtypeStruct((b, N_KV_HEADS, rep, t, HEAD_DIM), BF16),
            jax.ShapeDtypeStruct((b, N_KV_HEADS, t, HEAD_DIM), BF16),
            jax.ShapeDtypeStruct((b, N_KV_HEADS, t, HEAD_DIM), BF16),
            jax.ShapeDtypeStruct((b, t, s5_ch), BF16),
        ],
        compiler_params=_cparams(("parallel", "parallel")),
        name="inproj_ab",
    )(x, modtab, w_in, q_norm.reshape(1, -1), k_norm.reshape(1, -1), cos, sin)


def _inproj_cd_kernel(x_ref, mod_ref, w_ref, qln_ref, kvln_ref, wuq_ref, wukv_ref,
                      cos_ref, sin_ref, cosm_ref, sinm_ref,
                      qm_ref, km_ref, vm_ref, qd_ref, kd_ref, vd_ref, *, scale_c, scale_d, q_lora, kv_lora):
    x = x_ref[...]
    h = (x * (1.0 + mod_ref[1:2, :]) + mod_ref[0:1, :]).astype(BF16)
    y = _dot(h, w_ref[...])
    cos = cos_ref[...]
    sin = sin_ref[...]
    cosm = cosm_ref[...]
    sinm = sinm_ref[...]
    rep = N_HEADS // N_KV_HEADS
    cq = _rms(y[:, :q_lora], qln_ref[...]).astype(BF16)
    ckv = _rms(y[:, q_lora:q_lora + kv_lora], kvln_ref[...]).astype(BF16)
    q = _dot(cq, wuq_ref[...])
    kv = _dot(ckv, wukv_ref[...])
    o = q_lora + kv_lora
    qd0, kd0 = o, o + N_HEADS * HEAD_DIM
    vd0 = kd0 + N_KV_HEADS * HEAD_DIM
    kr0 = vd0 + N_KV_HEADS * HEAD_DIM
    k_rope = _rope_lanes(y[:, kr0:kr0 + LANES], cosm, sinm, C_ROPE // 4)
    for hh in range(N_HEADS):
        qn = q[:, hh * MLA_QK:hh * MLA_QK + HEAD_DIM]
        qr = _rope_lanes(q[:, hh * MLA_QK + HEAD_DIM:(hh + 1) * MLA_QK], cosm, sinm, C_ROPE // 4)
        qm_ref[hh, 0] = (jnp.concatenate([qn, qr], axis=-1) * scale_c).astype(BF16)
        kn = kv[:, hh * HEAD_DIM:(hh + 1) * HEAD_DIM]
        km_ref[hh] = jnp.concatenate([kn, k_rope], axis=-1).astype(BF16)
        vm_ref[hh] = kv[:, (N_HEADS + hh) * HEAD_DIM:(N_HEADS + hh + 1) * HEAD_DIM].astype(BF16)
        qdh = _rope_lanes(y[:, qd0 + hh * HEAD_DIM:qd0 + (hh + 1) * HEAD_DIM], cos, sin, HEAD_DIM // 4)
        qd_ref[hh // rep, hh % rep] = (qdh * scale_d).astype(BF16)
    for g in range(N_KV_HEADS):
        kdh = _rope_lanes(y[:, kd0 + g * HEAD_DIM:kd0 + (g + 1) * HEAD_DIM], cos, sin, HEAD_DIM // 4)
        kd_ref[g] = kdh.astype(BF16)
        vd_ref[g] = y[:, vd0 + g * HEAD_DIM:vd0 + (g + 1) * HEAD_DIM].astype(BF16)


def inproj_cd(x, modtab, w_in, q_ln, kv_ln, w_uq, w_ukv, cos, sin, cosm, sinm, n_ctx):
    b, t, d = x.shape
    n_in = w_in.shape[1]
    q_lora, kv_lora = q_ln.shape[0], kv_ln.shape[0]
    rep = N_HEADS // N_KV_HEADS
    tq = ROW_TILE
    nct = n_ctx // tq
    seg = lambda i: jnp.minimum(i // nct, 1) if nct > 0 else 1
    kern = functools.partial(_inproj_cd_kernel, scale_c=(HEAD_DIM + C_ROPE) ** -0.5 * LOG2E,
                             scale_d=HEAD_DIM ** -0.5 * LOG2E, q_lora=q_lora, kv_lora=kv_lora)
    full = lambda shape: pl.BlockSpec(shape, lambda bi, i: (0,) * len(shape))
    tab = pl.BlockSpec((tq, LANES), lambda bi, i: (i, 0))
    return pl.pallas_call(
        kern,
        grid=(b, t // tq),
        in_specs=[
            pl.BlockSpec((None, tq, d), lambda bi, i: (bi, i, 0)),
            pl.BlockSpec((None, None, 6, d), lambda bi, i: (bi, seg(i), 0, 0)),
            full((d, n_in)), full((1, q_lora)), full((1, kv_lora)),
            full(w_uq.shape), full(w_ukv.shape), tab, tab, tab, tab,
        ],
        out_specs=[
            pl.BlockSpec((None, N_HEADS, 1, tq, MLA_QK), lambda bi, i: (bi, 0, 0, i, 0)),
            pl.BlockSpec((None, N_HEADS, tq, MLA_QK), lambda bi, i: (bi, 0, i, 0)),
            pl.BlockSpec((None, N_HEADS, tq, HEAD_DIM), lambda bi, i: (bi, 0, i, 0)),
            pl.BlockSpec((None, N_KV_HEADS, rep, tq, HEAD_DIM), lambda bi, i: (bi, 0, 0, i, 0)),
            pl.BlockSpec((None, N_KV_HEADS, tq, HEAD_DIM), lambda bi, i: (bi, 0, i, 0)),
            pl.BlockSpec((None, N_KV_HEADS, tq, HEAD_DIM), lambda bi, i: (bi, 0, i, 0)),
        ],
        out_shape=[
            jax.ShapeDtypeStruct((b, N_HEADS, 1, t, MLA_QK), BF16),
            jax.ShapeDtypeStruct((b, N_HEADS, t, MLA_QK), BF16),
            jax.ShapeDtypeStruct((b, N_HEADS, t, HEAD_DIM), BF16),
            jax.ShapeDtypeStruct((b, N_KV_HEADS, rep, t, HEAD_DIM), BF16),
            jax.ShapeDtypeStruct((b, N_KV_HEADS, t, HEAD_DIM), BF16),
            jax.ShapeDtypeStruct((b, N_KV_HEADS, t, HEAD_DIM), BF16),
        ],
        compiler_params=_cparams(("parallel", "parallel")),
        name="inproj_cd",
    )(x, modtab, w_in, q_ln.reshape(1, -1), kv_ln.reshape(1, -1), w_uq, w_ukv, cos, sin, cosm, sinm)


def _prep_cd_weights(w_in, w_uq, w_ukv):
    d = w_in.shape[0]
    q_lora = w_uq.shape[0]
    kv_lora = w_ukv.shape[0]
    o = q_lora + kv_lora
    k_rope = w_in[:, o:o + C_ROPE]
    rest = w_in[:, o + C_ROPE:]
    w_in_r = jnp.concatenate([w_in[:, :o], rest, k_rope, jnp.zeros((d, LANES - C_ROPE), w_in.dtype)], axis=1)
    uq = w_uq.reshape(q_lora, N_HEADS, HEAD_DIM + C_ROPE)
    uq = jnp.pad(uq, ((0, 0), (0, 0), (0, MLA_QK - HEAD_DIM - C_ROPE))).reshape(q_lora, N_HEADS * MLA_QK)
    ukv = w_ukv.reshape(kv_lora, N_HEADS, 2, HEAD_DIM).transpose(0, 2, 1, 3).reshape(kv_lora, 2 * N_HEADS * HEAD_DIM)
    return w_in_r.astype(BF16), uq.astype(BF16), ukv.astype(BF16)


def _softmax_pv(blocks, sink_col):
    mx = None
    for s, _ in blocks:
        bm = jnp.max(s, axis=-1, keepdims=True)
        mx = bm if mx is None else jnp.maximum(mx, bm)
    if sink_col is not None:
        mx = jnp.maximum(mx, sink_col)
    den = None
    acc = None
    for s, v in blocks:
        p = jnp.exp2(s - mx)
        ps = jnp.sum(p, axis=-1, keepdims=True)
        den = ps if den is None else den + ps
        pv = _dot(p.astype(BF16), v)
        acc = pv if acc is None else acc + pv
    if sink_col is not None:
        den = den + jnp.exp2(sink_col - mx)
    return acc / den


def _online_softmax_pv(q, k_ref, v_ref, sink_col):
    n_keys = k_ref.shape[0]
    dv = v_ref.shape[1]
    m = sink_col if sink_col is not None else jnp.full((q.shape[0], 1), -jnp.inf, F32)
    acc = None
    for c0 in range(0, n_keys, KEY_CHUNK):
        c1 = min(c0 + KEY_CHUNK, n_keys)
        s = _dot_nt(q, k_ref[c0:c1, :])
        m_new = jnp.maximum(m, jnp.max(s, axis=-1, keepdims=True))
        p = jnp.exp2((s - m_new).astype(BF16))
        v_ones = jnp.concatenate([v_ref[c0:c1, :], jnp.ones((c1 - c0, LANES), BF16)], axis=-1)
        pv = _dot(p, v_ones)
        acc = pv if acc is None else acc * jnp.exp2(m - m_new) + pv
        m = m_new
    den = acc[:, dv:dv + 1]
    if sink_col is not None:
        den = den + jnp.exp2(sink_col - m)
    return acc[:, :dv] / den


def _attn_kernel(sink_ref, q_ref, k_ref, v_ref, o_ref, *, n_ctx, tq, rep, hps, windowed, use_sink):
    gi = pl.program_id(1)
    qi = pl.program_id(2)
    t_all = k_ref.shape[1]
    dk = q_ref.shape[-1]
    dv = v_ref.shape[-1]
    nct = n_ctx // tq

    def head(h, ctx_tile):
        g = gi * hps + h
        q = q_ref[h].reshape(rep * tq, dk)
        kh = k_ref.at[h]
        vh = v_ref.at[h]
        if use_sink:
            row = lax.broadcasted_iota(I32, (rep * tq, 1), 0)
            sink_col = jnp.full((rep * tq, 1), sink_ref[g * rep], F32)
            for r in range(1, rep):
                sink_col = jnp.where(row >= r * tq, sink_ref[g * rep + r], sink_col)
            sink_col = sink_col * LOG2E
        else:
            sink_col = None
        if ctx_tile:
            o = _softmax_pv([(_dot_nt(q, kh[0:n_ctx, :]), vh[0:n_ctx, :])], sink_col)
        elif not windowed:
            o = _online_softmax_pv(q, kh, vh, sink_col)
        else:
            band = tq + 2 * D_WINDOW
            s0 = (qi - nct) * tq
            kstart = jnp.clip(n_ctx + s0 - D_WINDOW, n_ctx, t_all - band)
            kstart = pl.multiple_of(kstart, LANES)
            sb = _dot_nt(q, kh[pl.ds(kstart, band), :])
            rowq = lax.broadcasted_iota(I32, (rep * tq, band), 0) % tq + s0
            colk = lax.broadcasted_iota(I32, (rep * tq, band), 1) + (kstart - n_ctx)
            sb = jnp.where(jnp.abs(colk - rowq) <= D_WINDOW, sb, NEG_INF)
            o = _softmax_pv([(_dot_nt(q, kh[0:n_ctx, :]), vh[0:n_ctx, :]), (sb, vh[pl.ds(kstart, band), :])],
                            sink_col)
        for r in range(rep):
            c0 = (h * rep + r) * dv
            o_ref[:, c0:c0 + dv] = o[r * tq:(r + 1) * tq].astype(o_ref.dtype)

    @pl.when(qi < nct)
    def _():
        for h in range(hps):
            head(h, True)

    @pl.when(qi >= nct)
    def _():
        for h in range(hps):
            head(h, False)


def attention(q, k, v, n_ctx, *, windowed=False, sink=None, hps=ATTN_HEADS_PER_STEP):
    b, g, rep, t, dk = q.shape
    dv = v.shape[-1]
    tq = ROW_TILE
    use_sink = sink is not None
    if sink is None:
        sink = jnp.zeros((g * rep,), F32)
    kern = functools.partial(_attn_kernel, n_ctx=n_ctx, tq=tq, rep=rep, hps=hps, windowed=windowed,
                             use_sink=use_sink)
    return pl.pallas_call(
        kern,
        grid=(b, g // hps, t // tq),
        in_specs=[
            pl.BlockSpec(memory_space=pltpu.SMEM),
            pl.BlockSpec((None, hps, rep, tq, dk), lambda bi, gi, i: (bi, gi, 0, i, 0)),
            pl.BlockSpec((None, hps, t, dk), lambda bi, gi, i: (bi, gi, 0, 0)),
            pl.BlockSpec((None, hps, t, dv), lambda bi, gi, i: (bi, gi, 0, 0)),
        ],
        out_specs=pl.BlockSpec((None, tq, hps * rep * dv), lambda bi, gi, i: (bi, i, gi)),
        out_shape=jax.ShapeDtypeStruct((b, t, g * rep * dv), BF16),
        compiler_params=_cparams(("parallel", "parallel", "arbitrary")),
        name="attention_win" if windowed else "attention",
    )(sink.astype(F32), q, k, v)


def _s5_tables(lam_re, lam_im, log_dt, b_re, b_im, c_re, c_im, d_skip):
    n_groups, n_state = lam_re.shape[1:]
    L, gs = S5_CHUNK, S5_GROUP
    lam = lax.complex(lam_re.astype(F32), lam_im.astype(F32))
    dt = jnp.exp(log_dt.astype(F32))[..., None]
    lam_dt = lam * dt
    lam_bar = jnp.exp(lam_dt)
    b_bar = ((lam_bar - 1.0) / lam)[..., None] * lax.complex(b_re.astype(F32), b_im.astype(F32))
    c_mat = lax.complex(c_re.astype(F32), c_im.astype(F32))
    pw = jnp.exp(lam_dt[None] * jnp.arange(L + 1, dtype=F32)[:, None, None, None])
    kker = jnp.einsum("dgop,tdgp,dgpi->dtgoi", c_mat, pw[:L], b_bar).real
    s_in = jnp.arange(L)[:, None]
    s_out = jnp.arange(L)[None, :]
    tau_f = s_out - s_in
    tau_r = s_in - s_out
    kf = jnp.where((tau_f >= 0)[:, :, None, None, None], kker[0][jnp.clip(tau_f, 0, L - 1)], 0.0)
    kr = jnp.where((tau_r >= 0)[:, :, None, None, None], kker[1][jnp.clip(tau_r, 0, L - 1)], 0.0)
    kt = kf + kr
    gp = S5_PACK
    n_packs = n_groups // gp
    eye = jnp.eye(gp, dtype=F32)
    t6 = kt.transpose(2, 0, 4, 1, 3).reshape(n_packs, gp, L, gs, L, gs)
    d6 = d_skip.astype(F32).reshape(n_packs, gp, 1, gs, 1, 1) * (
        jnp.eye(L, dtype=F32)[None, None, :, None, :, None] * jnp.eye(gs, dtype=F32)[None, None, None, :, None, :])
    tmat = jnp.einsum("pgsctd,gh->psgcthd", t6 + d6, eye).reshape(n_packs, L * LANES, L * LANES)
    steps = jnp.arange(L, dtype=F32)[:, None, None]
    wf = jnp.exp(lam_dt[0][None] * (L - 1 - steps))[:, :, :, None] * b_bar[0][None]
    wr = jnp.exp(lam_dt[1][None] * steps)[:, :, :, None] * b_bar[1][None]
    def w_pack(w):
        w2 = jnp.stack([w.real, w.imag], axis=0).reshape(2, L, n_packs, gp, n_state, gs)
        return jnp.einsum("bspgqc,gh->psgcbhq", w2, eye).reshape(n_packs, L * LANES, 2 * gp * n_state)
    vf = c_mat[0][None] * jnp.exp(lam_dt[0][None] * (steps + 1))[:, :, None, :]
    vr = c_mat[1][None] * jnp.exp(lam_dt[1][None] * (L - steps))[:, :, None, :]
    def v_pack(vv):
        v2 = jnp.stack([vv.real, -vv.imag], axis=0).reshape(2, L, n_packs, gp, gs, n_state)
        return jnp.einsum("bspgcq,gh->pbgqshc", v2, eye).reshape(n_packs, 2 * gp * n_state, L * LANES)
    def lam_pack(l):
        return jnp.stack([l.real, l.imag], axis=0).reshape(2, n_packs, gp * n_state).transpose(1, 0, 2)
    lam_l = pw[L]
    return (tmat.astype(BF16),
            (w_pack(wf).astype(BF16), v_pack(vf).astype(BF16), lam_pack(lam_l[0])),
            (w_pack(wr).astype(BF16), v_pack(vr).astype(BF16), lam_pack(lam_l[1])))


def _s5_pass_kernel(*refs, reverse):
    if reverse:
        u_ref, yin_ref, w_ref, v_ref, lam_ref, y_ref, s_scr, h_scr = refs
    else:
        u_ref, t_ref, w_ref, v_ref, lam_ref, y_ref, s_scr, h_scr = refs

    @pl.when(pl.program_id(1) == 0)
    def _():
        h_scr[...] = jnp.zeros_like(h_scr)

    nchs, L, b, _ = u_ref.shape
    rows = nchs * b
    half = h_scr.shape[1] // 2
    xg = jnp.concatenate([u_ref[:, s].reshape(rows, LANES) for s in range(L)], axis=-1)
    s_scr[...] = _dot(xg, w_ref[...])
    lr = jnp.broadcast_to(lam_ref[0:1, :], (b, half))
    li = jnp.broadcast_to(lam_ref[1:2, :], (b, half))

    def step(k, carry):
        hr, hi = carry
        j = nchs - 1 - k if reverse else k
        r0 = pl.multiple_of(j * b, b)
        sr = s_scr[pl.ds(r0, b), 0:half]
        si = s_scr[pl.ds(r0, b), half:]
        s_scr[pl.ds(r0, b), 0:half] = hr
        s_scr[pl.ds(r0, b), half:] = hi
        return lr * hr - li * hi + sr, lr * hi + li * hr + si

    hr, hi = lax.fori_loop(0, nchs, step, (h_scr[:, 0:half], h_scr[:, half:]))
    h_scr[:, 0:half] = hr
    h_scr[:, half:] = hi
    y = _dot(s_scr[...].astype(BF16), v_ref[...])
    if not reverse:
        y = y + _dot(xg, t_ref[...])
    for s in range(L):
        blk = y[:, s * LANES:(s + 1) * LANES].reshape(nchs, b, LANES)
        if reverse:
            blk = blk + yin_ref[:, s]
        y_ref[:, s] = blk


def s5_apply(u, tables, n_ctx):
    tmat, fwd, rev = tables
    b, t, ch = u.shape
    L = S5_CHUNK
    n_packs = ch // LANES
    seg = ROW_TILE
    nseg, nct, nchs = t // seg, n_ctx // seg, seg // L
    width = L * LANES
    n_state2 = fwd[0].shape[-1]
    ut = u.transpose(1, 0, 2).reshape(t // L, L, b, ch)
    blk = pl.BlockSpec((nchs, L, b, LANES), lambda p, i: (i, 0, 0, p))
    rseg = lambda i: jnp.where(i < nct, nct - 1 - i, nseg - 1 - (i - nct))
    rblk = pl.BlockSpec((nchs, L, b, LANES), lambda p, i: (rseg(i), 0, 0, p))
    tab = lambda shape: pl.BlockSpec((None,) + shape, lambda p, i: (p, 0, 0))
    scratch = [pltpu.VMEM((nchs * b, n_state2), F32), pltpu.VMEM((b, n_state2), F32)]
    y_shape = jax.ShapeDtypeStruct((t // L, L, b, ch), F32)
    y_f = pl.pallas_call(
        functools.partial(_s5_pass_kernel, reverse=False),
        grid=(n_packs, nseg),
        in_specs=[blk, tab((width, width)), tab((width, n_state2)), tab((n_state2, width)), tab((2, n_state2 // 2))],
        out_specs=blk,
        out_shape=y_shape,
        scratch_shapes=scratch,
        compiler_params=_cparams(("parallel", "arbitrary")),
        name="s5_forward",
    )(ut, tmat, *fwd)
    y = pl.pallas_call(
        functools.partial(_s5_pass_kernel, reverse=True),
        grid=(n_packs, nseg),
        in_specs=[rblk, rblk, tab((width, n_state2)), tab((n_state2, width)), tab((2, n_state2 // 2))],
        out_specs=rblk,
        out_shape=y_shape,
        scratch_shapes=scratch,
        input_output_aliases={1: 0},
        compiler_params=_cparams(("parallel", "arbitrary")),
        name="s5_reverse",
    )(ut, y_f, *rev)
    return y.reshape(t, b, ch).transpose(1, 0, 2)


def _gelu_tanh(x):
    return 0.5 * x * (1.0 + jnp.tanh(math.sqrt(2.0 / math.pi) * (x + 0.044715 * (x * x * x))))


def _post_kernel(*refs, alpha, glu, n_exp):
    if glu:
        (x_ref, a_ref, s_ref, wg_ref, bg_ref, wm_ref, mod_ref, g_ref, b_ref, wrh_ref, wrl_ref, br_ref,
         x1_ref, h2_ref, idx_ref, gate_ref, rank_ref, cnt_ref, cnt_scr) = refs
    else:
        (x_ref, a_ref, s_ref, wm_ref, mod_ref, g_ref, b_ref, wrh_ref, wrl_ref, br_ref,
         x1_ref, h2_ref, idx_ref, gate_ref, rank_ref, cnt_ref, cnt_scr) = refs
    first = (pl.program_id(0) == 0) & (pl.program_id(1) == 0)

    @pl.when(first)
    def _():
        cnt_scr[...] = jnp.zeros_like(cnt_scr)

    tq, d = x_ref.shape
    half = a_ref.shape[1]
    if glu:
        z = _gelu_tanh(s_ref[...])
        gate = jax.nn.sigmoid(_dot(z.astype(BF16), wg_ref[...]) + bg_ref[...])
        second = (z * gate).astype(BF16)
    else:
        second = s_ref[...]
    mix = _dot(a_ref[...], wm_ref[0:half, :]) + _dot(second, wm_ref[half:, :])
    x1 = _layer_norm(alpha * x_ref[...] + mod_ref[2:3, :] * mix, g_ref[...], b_ref[...])
    x1_ref[...] = x1
    h2 = x1 * (1.0 + mod_ref[4:5, :]) + mod_ref[3:4, :]
    _store_token_tiles(h2_ref, h2)

    h_hi, h_lo = _split_bf16(h2)
    logits = _dot_nt(wrh_ref[...], h_hi) + _dot_nt(wrh_ref[...], h_lo) + _dot_nt(wrl_ref[...], h_hi) + br_ref[...]
    eidx = lax.broadcasted_iota(I32, (n_exp, tq), 0)
    work = logits
    tops, sels = [], []
    for k in range(TOP_K):
        m = jnp.max(work, axis=0, keepdims=True)
        ik = jnp.min(jnp.where(work == m, eidx, n_exp), axis=0, keepdims=True)
        sel = eidx == ik
        work = jnp.where(sel, -jnp.inf, work)
        tops.append(m)
        sels.append(sel)
        idx_ref[k:k + 1, :] = ik
    exps = [jnp.exp(tk - tops[0]) for tk in tops]
    den = exps[0] + exps[1] + exps[2] + exps[3]
    for k in range(TOP_K):
        gate_ref[k:k + 1, :] = exps[k] / den
    onehot = jnp.zeros((n_exp, tq), F32)
    for sel in sels:
        onehot = onehot + sel.astype(F32)
    tri = (lax.broadcasted_iota(I32, (tq, tq), 0) < lax.broadcasted_iota(I32, (tq, tq), 1)).astype(BF16)
    before = _dot(onehot.astype(BF16), tri) + cnt_scr[...]
    for k in range(TOP_K):
        rk = jnp.sum(jnp.where(sels[k], before, 0.0), axis=0, keepdims=True)
        rank_ref[k:k + 1, :] = rk.astype(I32)
    cnt_scr[...] += jnp.sum(onehot, axis=1, keepdims=True)
    cnt_ref[...] = jnp.broadcast_to(cnt_scr[...], cnt_ref.shape).astype(I32)


def post_mixer(x, a, s, modtab, w_mix, ln_g, ln_b, w_router, b_router, n_ctx, alpha, glu_w=None, glu_b=None):
    b, t, d = x.shape
    half = a.shape[-1]
    n_exp = w_router.shape[1]
    tq = ROW_TILE
    nct = n_ctx // tq
    seg = lambda i: jnp.minimum(i // nct, 1) if nct > 0 else 1
    glu = glu_w is not None
    wr_hi, wr_lo = _split_bf16(w_router.T.astype(F32))
    full = lambda shape: pl.BlockSpec(shape, lambda bi, i: (0,) * len(shape))
    tok = lambda width: pl.BlockSpec((None, tq, width), lambda bi, i: (bi, i, 0))
    in_specs = [tok(d), tok(half), tok(half)]
    args = [x, a, s]
    if glu:
        in_specs += [full((half, half)), full((1, half))]
        args += [glu_w.astype(BF16), glu_b.reshape(1, half)]
    in_specs += [full((d, d)), pl.BlockSpec((None, None, 6, d), lambda bi, i: (bi, seg(i), 0, 0)),
                 full((1, d)), full((1, d)), full((n_exp, d)), full((n_exp, d)), full((n_exp, 1))]
    args += [w_mix.astype(BF16), modtab, ln_g.reshape(1, d), ln_b.reshape(1, d), wr_hi, wr_lo,
             b_router.reshape(n_exp, 1)]
    nt = t // tq
    lane_out = lambda dt: (pl.BlockSpec((TOP_K, tq), lambda bi, i: (0, bi * nt + i)),
                           jax.ShapeDtypeStruct((TOP_K, b * t), dt))
    outs = [
        (tok(d), jax.ShapeDtypeStruct((b, t, d), F32)),
        (pl.BlockSpec((tq * SUBLANES, LANES), lambda bi, i: (bi * nt + i, 0)),
         jax.ShapeDtypeStruct((b * t * SUBLANES, LANES), F32)),
        lane_out(I32), lane_out(F32), lane_out(I32),
        (pl.BlockSpec((n_exp, LANES), lambda bi, i: (0, 0)), jax.ShapeDtypeStruct((n_exp, LANES), I32)),
    ]
    kern = functools.partial(_post_kernel, alpha=alpha, glu=glu, n_exp=n_exp)
    return pl.pallas_call(
        kern,
        grid=(b, nt),
        in_specs=in_specs,
        out_specs=[o[0] for o in outs],
        out_shape=[o[1] for o in outs],
        scratch_shapes=[pltpu.VMEM((n_exp, 1), F32)],
        compiler_params=_cparams(("arbitrary", "arbitrary")),
        name="post_mixer_glu" if glu else "post_mixer",
    )(*args)


def _route_kernel(start_ref, idx_ref, rank_ref, pos_ref, *, n_exp, tq):
    n = idx_ref.shape[1]
    eidx = lax.broadcasted_iota(I32, (n_exp, n), 0)
    for k in range(TOP_K):
        sel = eidx == idx_ref[k:k + 1, :]
        base = jnp.sum(jnp.where(sel, start_ref[...], 0.0), axis=0, keepdims=True)
        pos = base.astype(I32) + rank_ref[k:k + 1, :]
        for ti in range(n // tq):
            pos_ref[ti, :, k * tq:(k + 1) * tq] = pos[:, ti * tq:(ti + 1) * tq]


def route_positions(group_start, idx, rank):
    n_exp = group_start.shape[0]
    n_tok = idx.shape[1]
    tq = ROW_TILE
    n_tiles = n_tok // tq
    per_step = math.gcd(n_tiles, 8)
    return pl.pallas_call(
        functools.partial(_route_kernel, n_exp=n_exp, tq=tq),
        grid=(n_tiles // per_step,),
        in_specs=[
            pl.BlockSpec((n_exp, 1), lambda i: (0, 0)),
            pl.BlockSpec((TOP_K, per_step * tq), lambda i: (0, i)),
            pl.BlockSpec((TOP_K, per_step * tq), lambda i: (0, i)),
        ],
        out_specs=pl.BlockSpec((per_step, 1, TOP_K * tq), lambda i: (i, 0, 0)),
        out_shape=jax.ShapeDtypeStruct((n_tiles, 1, TOP_K * tq), I32),
        compiler_params=_cparams(("parallel",)),
        name="route_positions",
    )(group_start.astype(F32).reshape(n_exp, 1), idx, rank)


def _dispatch_kernel(pos_hbm, h_ref, xs_hbm, idx_smem, idx_sem, row_sem, *, tq, n_tiles, tile_rows):
    i = pl.program_id(0)
    slot = i % 2

    def idx_copy(tile, sl):
        return pltpu.make_async_copy(pos_hbm.at[tile], idx_smem.at[sl], idx_sem.at[sl])

    @pl.when(i == 0)
    def _():
        idx_copy(0, 0).start()

    idx_copy(i, slot).wait()

    @pl.when(i + 1 < n_tiles)
    def _():
        idx_copy(i + 1, 1 - slot).start()

    for k in range(TOP_K):
        for r in range(tq):
            dst = pl.multiple_of(idx_smem[slot, 0, k * tq + r] * tile_rows, tile_rows)
            pltpu.make_async_copy(h_ref.at[pl.ds(r * tile_rows, tile_rows), :],
                                  xs_hbm.at[pl.ds(dst, tile_rows), :], row_sem.at[0]).start(priority=r % 2)
    for k in range(TOP_K):
        pltpu.make_async_copy(h_ref, xs_hbm.at[pl.ds(0, tq * tile_rows), :], row_sem.at[0]).wait()


def moe_dispatch(pos_tiles, h_tiles, tile_rows):
    n_tiles = pos_tiles.shape[0]
    tq = h_tiles.shape[0] // tile_rows // n_tiles
    return pl.pallas_call(
        functools.partial(_dispatch_kernel, tq=tq, n_tiles=n_tiles, tile_rows=tile_rows),
        grid=(n_tiles,),
        in_specs=[pl.BlockSpec(memory_space=pl.ANY),
                  pl.BlockSpec((tq * tile_rows, LANES), lambda i: (i, 0))],
        out_specs=pl.BlockSpec(memory_space=pl.ANY),
        out_shape=jax.ShapeDtypeStruct((TOP_K * h_tiles.shape[0], LANES), F32),
        scratch_shapes=[
            pltpu.SMEM((2, 1, TOP_K * tq), I32),
            pltpu.SemaphoreType.DMA((2,)),
            pltpu.SemaphoreType.DMA((1,)),
        ],
        compiler_params=_cparams(("arbitrary",)),
        name="moe_dispatch",
    )(pos_tiles, h_tiles)


def _moe_kernel(e_ref, j_ref, lo_ref, hi_ref, first_ref, new_ref, x_ref, wi_ref, bi_ref, wo_ref, bo_ref,
                y_ref, wi_b, wo_b, *, n_sub):
    i = pl.program_id(0)
    lo = lo_ref[i]
    hi = hi_ref[i]

    @pl.when(new_ref[i] == 1)
    def _():
        wi_b[...] = wi_ref[...].astype(BF16)
        wo_b[...] = wo_ref[...].astype(BF16)

    tm = x_ref.shape[0] // n_sub
    sub = tm // MOE_CHAINS

    def expert_rows(r0):
        x = _load_token_tiles(x_ref, r0 * n_sub, sub, n_sub).astype(BF16)
        z = _dot(x, wi_b[...]) + bi_ref[...]
        f = z.shape[1] // 2
        glu = jnp.minimum(z[:, :f], SWIGLU_LIMIT)
        lin = jnp.clip(z[:, f:], -SWIGLU_LIMIT, SWIGLU_LIMIT)
        act = glu * jax.nn.sigmoid(SWIGLU_ALPHA * glu) * (lin + 1.0)
        y = _dot(act.astype(BF16), wo_b[...]) + bo_ref[...]
        row = lax.broadcasted_iota(I32, (sub, 1), 0) + r0
        return y, (row >= lo) & (row < hi)

    def out_rows(r0):
        return y_ref.at[pl.ds(r0 * n_sub, sub * n_sub), :]

    @pl.when((hi > lo) & (first_ref[i] == 1))
    def _():
        for r0 in range(0, tm, sub):
            y, mine = expert_rows(r0)
            _store_token_tiles(out_rows(r0), jnp.where(mine, y, 0.0))

    @pl.when((hi > lo) & (first_ref[i] == 0))
    def _():
        for r0 in range(0, tm, sub):
            y, mine = expert_rows(r0)
            _store_token_tiles(out_rows(r0), jnp.where(mine, y, _load_token_tiles(out_rows(r0), 0, sub, n_sub)))


def moe_experts(xs, items, w_in, b_in, w_out, b_out, layer):
    _, n_exp, d, f2 = w_in.shape
    n_sub = d // LANES
    n_items = items[0].shape[0]
    tm = MOE_BLOCK
    wmap = lambda i, e, j, lo, hi, fi, nw: (layer, e[i], 0, 0)
    xmap = lambda i, e, j, lo, hi, fi, nw: (j[i], 0)
    grid_spec = pltpu.PrefetchScalarGridSpec(
        num_scalar_prefetch=6,
        grid=(n_items,),
        in_specs=[
            pl.BlockSpec((tm * n_sub, LANES), xmap),
            pl.BlockSpec((None, None, d, f2), wmap),
            pl.BlockSpec((None, None, 1, f2), wmap),
            pl.BlockSpec((None, None, f2 // 2, d), wmap),
            pl.BlockSpec((None, None, 1, d), wmap),
        ],
        out_specs=pl.BlockSpec((tm * n_sub, LANES), xmap),
        scratch_shapes=[pltpu.VMEM((d, f2), BF16), pltpu.VMEM((f2 // 2, d), BF16)],
    )
    return pl.pallas_call(
        functools.partial(_moe_kernel, n_sub=n_sub),
        grid_spec=grid_spec,
        out_shape=jax.ShapeDtypeStruct(xs.shape, F32),
        compiler_params=_cparams(("arbitrary",)),
        name="moe_experts",
    )(*items, xs, w_in, b_in.reshape(b_in.shape[0], n_exp, 1, f2), w_out, b_out.reshape(b_out.shape[0], n_exp, 1, d))


def _combine_kernel(pos_hbm, x_ref, g_ref, mod_ref, lg_ref, lb_ref, ys_hbm, o_ref,
                    idx_smem, ybuf0, ybuf1, idx_sem, row_sem, *, alpha, tq, n_tiles):
    i = pl.program_id(0)
    slot = i % 2
    d = x_ref.shape[1]
    n_sub = d // LANES
    bufs = (ybuf0, ybuf1)

    def idx_copy(tile, sl):
        return pltpu.make_async_copy(pos_hbm.at[tile], idx_smem.at[sl], idx_sem.at[sl])

    def gather_start(sl, rows):
        for r in rows:
            src = pl.multiple_of(idx_smem[sl, 0, r] * n_sub, n_sub)
            pltpu.make_async_copy(ys_hbm.at[pl.ds(src, n_sub), :], bufs[sl].at[pl.ds(r * n_sub, n_sub), :],
                                  row_sem.at[sl]).start(priority=r % 2)

    def rows_wait(sl):
        pltpu.make_async_copy(ys_hbm.at[pl.ds(0, TOP_K * tq * n_sub), :], bufs[sl], row_sem.at[sl]).wait()

    @pl.when(i == 0)
    def _():
        idx_copy(0, 0).start()
        idx_copy(0, 0).wait()
        gather_start(0, range(TOP_K * tq))
        idx_copy(1, 1).start()

    @pl.when(i + 1 < n_tiles)
    def _():
        idx_copy(i + 1, 1 - slot).wait()

    def step(cur):
        nxt = 1 - cur
        rows_wait(cur)

        @pl.when(i + 2 < n_tiles)
        def _():
            idx_copy(i + 2, cur).start()

        gates = g_ref[...]
        per_chunk = TOP_K * tq // n_sub
        parts = []
        for c in range(n_sub):
            gather_start(nxt, range(c * per_chunk, (c + 1) * per_chunk))
            acc = None
            for k in range(TOP_K):
                term = gates[:, k:k + 1] * bufs[cur][pl.ds(k * tq * n_sub + c, tq, stride=n_sub), :]
                acc = term if acc is None else acc + term
            parts.append(acc)
        y = jnp.concatenate(parts, axis=-1)
        o_ref[...] = _layer_norm(alpha * x_ref[...] + mod_ref[5:6, :] * y, lg_ref[...], lb_ref[...])

        @pl.when(i == n_tiles - 1)
        def _():
            rows_wait(nxt)

    @pl.when(slot == 0)
    def _():
        step(0)

    @pl.when(slot == 1)
    def _():
        step(1)


def moe_combine(x1, pos_tiles, gates_tok, modtab, ln_g, ln_b, ys, n_ctx, alpha, lat_only=False):
    b, t, d = x1.shape
    tq = ROW_TILE
    nt = t // tq
    n_tiles = b * nt
    nct = n_ctx // tq
    seg = lambda i: jnp.minimum((i % nt) // nct, 1) if nct > 0 else 1
    assert n_tiles >= 2, "the combine pipeline prefetches one tile ahead"
    kern = functools.partial(_combine_kernel, alpha=alpha, tq=tq, n_tiles=n_tiles)
    if lat_only:
        nlt = nt - nct
        out_map = lambda i: ((i // nt) * nlt + jnp.maximum(i % nt - nct, 0), 0)
        out_rows = b * nlt * tq
    else:
        out_map = lambda i: (i, 0)
        out_rows = b * t
    out = pl.pallas_call(
        kern,
        grid=(n_tiles,),
        in_specs=[
            pl.BlockSpec(memory_space=pl.ANY),
            pl.BlockSpec((tq, d), lambda i: (i, 0)),
            pl.BlockSpec((tq, TOP_K), lambda i: (i, 0)),
            pl.BlockSpec((None, None, 6, d), lambda i: (i // nt, seg(i), 0, 0)),
            pl.BlockSpec((1, d), lambda i: (0, 0)),
            pl.BlockSpec((1, d), lambda i: (0, 0)),
            pl.BlockSpec(memory_space=pl.ANY),
        ],
        out_specs=pl.BlockSpec((tq, d), out_map),
        out_shape=jax.ShapeDtypeStruct((out_rows, d), F32),
        scratch_shapes=[
            pltpu.SMEM((2, 1, TOP_K * tq), I32),
            pltpu.VMEM((TOP_K * tq * (d // LANES), LANES), F32),
            pltpu.VMEM((TOP_K * tq * (d // LANES), LANES), F32),
            pltpu.SemaphoreType.DMA((2,)),
            pltpu.SemaphoreType.DMA((2,)),
        ],
        compiler_params=_cparams(("arbitrary",)),
        name="moe_combine",
    )(pos_tiles, x1.reshape(b * t, d), gates_tok, modtab, ln_g.reshape(1, d), ln_b.reshape(1, d), ys)
    return out.reshape(b, out_rows // b, d)


def _group_items(counts, n_slots, tm):
    n_exp = counts.shape[0]
    le = (jnp.arange(n_exp)[:, None] <= jnp.arange(n_exp)[None, :]).astype(I32)
    g_end = counts @ le
    g_start = g_end - counts
    first_blk = g_start // tm
    n_blk = jnp.where(counts > 0, (g_end - 1) // tm - first_blk + 1, 0)
    i_end = n_blk @ le
    i_start = i_end - n_blk
    n_items = n_slots // tm + n_exp - 1
    it = jnp.arange(n_items, dtype=I32)
    valid = it < i_end[-1]
    e = jnp.minimum(jnp.sum((i_end[None, :] <= it[:, None]).astype(I32), axis=1), n_exp - 1)
    onehot = (e[:, None] == jnp.arange(n_exp)[None, :]).astype(I32)
    pick = lambda tbl: onehot @ tbl
    blk = pick(first_blk) + it - pick(i_start)
    lo = jnp.clip(pick(g_start) - blk * tm, 0, tm)
    hi = jnp.clip(pick(g_end) - blk * tm, 0, tm)
    last = jnp.maximum(i_end[-1] - 1, 0)
    e_last = jnp.sum(jnp.where(it == last, e, 0))
    e = jnp.where(valid, e, e_last)
    blk = jnp.where(valid, blk, n_slots // tm - 1)
    lo = jnp.where(valid, lo, 0)
    hi = jnp.where(valid, hi, 0)
    prev = lambda a: jnp.concatenate([jnp.full((1,), -1, I32), a[:-1]])
    first = (blk != prev(blk)).astype(I32)
    new_e = (e != prev(e)).astype(I32)
    return g_start, tuple(a.astype(I32) for a in (e, blk, lo, hi, first, new_e))


def kernel(x, c, ctx, c_ctx, w_mod, b_mod, ln_g, ln_b, w_mix_out, w_router, b_router, w_exp_in, b_exp_in, w_exp_out, b_exp_out, w_in_ab, a_q_norm, a_k_norm, s5_lam_re, s5_lam_im, s5_log_dt, s5_b_re, s5_b_im, s5_c_re, s5_c_im, s5_d, w_glu, b_glu, w_in_cd, c_q_norm, c_kv_norm, w_uq, w_ukv, d_sink):
    batch, n_lat, d = x.shape
    n_ctx = ctx.shape[1]
    depth = w_mod.shape[0]
    alpha = (2.0 * depth) ** 0.25
    t = n_ctx + n_lat
    n_tok = batch * t

    xs = jnp.concatenate([ctx, x], axis=1)
    n_rows = -(-(batch + 1) // SUBLANES) * SUBLANES
    cvec = jnp.zeros((n_rows, d), F32).at[:batch].set(c).at[batch].set(c_ctx)
    mods = modulation(cvec, w_mod, b_mod).reshape(depth, n_rows, 6, d)
    modtab = jnp.stack([jnp.broadcast_to(mods[:, batch:batch + 1], (depth, batch, 6, d)), mods[:, :batch]],
                       axis=2)

    cos_h, sin_h = _rope_tables(n_ctx, n_lat, HEAD_DIM)
    cos_m, sin_m = _rope_tables(n_ctx, n_lat, C_ROPE)

    for layer in range(depth):
        i = layer // 2
        mt = modtab[layer]
        if layer % 2 == 0:
            q, k, v, u = inproj_ab(xs, mt, w_in_ab[i].astype(BF16), a_q_norm[i], a_k_norm[i], cos_h, sin_h, n_ctx)
            att = attention(q, k, v, n_ctx)
            tables = _s5_tables(s5_lam_re[i], s5_lam_im[i], s5_log_dt[i], s5_b_re[i], s5_b_im[i],
                                s5_c_re[i], s5_c_im[i], s5_d[i])
            second = s5_apply(u, tables, n_ctx)
            glu_w, glu_b = w_glu[i], b_glu[i]
        else:
            w_in_r, uq_r, ukv_r = _prep_cd_weights(w_in_cd[i], w_uq[i], w_ukv[i])
            qm, km, vm, qd, kd, vd = inproj_cd(xs, mt, w_in_r, c_q_norm[i], c_kv_norm[i], uq_r, ukv_r,
                                               cos_h, sin_h, cos_m, sin_m, n_ctx)
            att = attention(qm, km, vm, n_ctx, hps=N_HEADS)
            second = attention(qd, kd, vd, n_ctx, windowed=True, sink=d_sink[i])
            glu_w = glu_b = None
        x1, h2, idx, gates, rank, counts = post_mixer(
            xs, att, second, mt, w_mix_out[layer], ln_g[layer, 0], ln_b[layer, 0],
            w_router[layer], b_router[layer], n_ctx, alpha, glu_w, glu_b)
        g_start, items = _group_items(counts[:, 0], TOP_K * n_tok, MOE_BLOCK)
        pos_tiles = route_positions(g_start, idx, rank)
        sorted_rows = moe_dispatch(pos_tiles, h2, d // LANES)
        ys = moe_experts(sorted_rows, items, w_exp_in, b_exp_in, w_exp_out, b_exp_out, layer)
        xs = moe_combine(x1, pos_tiles, gates.T, mt, ln_g[layer, 1], ln_b[layer, 1], ys, n_ctx, alpha,
                         lat_only=layer == depth - 1)
    return xs
```

```python
import functools
import math

import jax
import jax.numpy as jnp
from jax import lax
from jax.experimental import pallas as pl
from jax.experimental.pallas import tpu as pltpu

F32 = jnp.float32
BF16 = jnp.bfloat16
I32 = jnp.int32

GRID_W = 64
ROPE_THETA = 10000.0
HEAD_DIM = 128
N_HEADS = 4
N_KV_HEADS = 2
C_ROPE = 64
MLA_QK = 256
D_WINDOW = 128
S5_GROUP = 16
S5_STATE = 64
S5_CHUNK = 8
S5_PACK = 8
TOP_K = 4
SWIGLU_LIMIT = 7.0
SWIGLU_ALPHA = 1.702
NEG_INF = -1e30
LOG2E = 1.4426950408889634
LN_EPS = 1e-5
RMS_EPS = 1e-6

LANES = 128
SUBLANES = 8
VMEM_LIMIT_BYTES = 56 * 1024 * 1024

ROW_TILE = 256
MOE_BLOCK = 512
MOE_CHAINS = 2
KEY_CHUNK = 768
ATTN_HEADS_PER_STEP = 2


def _cparams(sem):
    return pltpu.CompilerParams(dimension_semantics=sem, vmem_limit_bytes=VMEM_LIMIT_BYTES)


def _split_bf16(a):
    hi = a.astype(BF16)
    lo = (a - hi.astype(F32)).astype(BF16)
    return hi, lo


def _dot(a, b):
    return jnp.dot(a, b, preferred_element_type=F32)


def _dot_nt(a, b):
    return lax.dot_general(a, b, (((1,), (1,)), ((), ())), preferred_element_type=F32)


def _store_token_tiles(ref, val):
    rows, d = val.shape
    n = d // LANES
    for s in range(n):
        ref[pl.ds(s, rows, stride=n), :] = val[:, s * LANES:(s + 1) * LANES]


def _load_token_tiles(ref, first_row, rows, n):
    return jnp.concatenate([ref[pl.ds(first_row + s, rows, stride=n), :] for s in range(n)], axis=-1)


def _layer_norm(x, g, b):
    mu = jnp.mean(x, axis=-1, keepdims=True)
    xc = x - mu
    var = jnp.mean(xc * xc, axis=-1, keepdims=True)
    return xc * lax.rsqrt(var + LN_EPS) * g + b


def _rms(x, g):
    return x * lax.rsqrt(jnp.mean(x * x, axis=-1, keepdims=True) + RMS_EPS) * g


def _mod_kernel(c_ref, w_ref, b_ref, o_ref):
    c = c_ref[...]
    s = c * jax.nn.sigmoid(c)
    s_hi, s_lo = _split_bf16(s)
    w_hi, w_lo = _split_bf16(w_ref[...])
    o_ref[...] = _dot(s_hi, w_hi) + _dot(s_lo, w_hi) + _dot(s_hi, w_lo) + b_ref[...]


def modulation(cvec, w_mod, b_mod):
    n_layers, d, d6 = w_mod.shape
    r = cvec.shape[0]
    tn = 1536
    return pl.pallas_call(
        _mod_kernel,
        grid=(n_layers, d6 // tn),
        in_specs=[
            pl.BlockSpec((r, d), lambda l, j: (0, 0)),
            pl.BlockSpec((None, d, tn), lambda l, j: (l, 0, j)),
            pl.BlockSpec((None, 1, tn), lambda l, j: (l, 0, j)),
        ],
        out_specs=pl.BlockSpec((None, r, tn), lambda l, j: (l, 0, j)),
        out_shape=jax.ShapeDtypeStruct((n_layers, r, d6), F32),
        compiler_params=_cparams(("arbitrary", "arbitrary")),
        name="modulation",
    )(cvec, w_mod, b_mod.reshape(n_layers, 1, d6))


def _rope_lanes(t, cos, sin, quarter):
    lane = lax.broadcasted_iota(I32, t.shape, 1)
    first = (lane % (2 * quarter)) < quarter
    partner = jnp.where(first, pltpu.roll(t, LANES - quarter, 1), pltpu.roll(t, quarter, 1))
    return t * cos + partner * sin


def _rope_tables(n_ctx, n_lat, rot_dim):
    quarter = rot_dim // 4
    pos = jnp.arange(n_lat, dtype=F32)
    row = jnp.floor(pos / GRID_W)
    col = pos - row * GRID_W
    inv_freq = ROPE_THETA ** (-jnp.arange(quarter, dtype=F32) / quarter)
    ang_r = row[:, None] * inv_freq
    ang_c = col[:, None] * inv_freq
    cos = jnp.concatenate([jnp.cos(ang_r)] * 2 + [jnp.cos(ang_c)] * 2, axis=-1)
    sin = jnp.concatenate([-jnp.sin(ang_r), jnp.sin(ang_r), -jnp.sin(ang_c), jnp.sin(ang_c)], axis=-1)
    pad = LANES - rot_dim
    cos = jnp.pad(cos, ((n_ctx, 0), (0, pad)), constant_values=1.0)
    sin = jnp.pad(sin, ((n_ctx, 0), (0, pad)))
    return cos, sin


def _inproj_ab_kernel(x_ref, mod_ref, w_ref, qn_ref, kn_ref, cos_ref, sin_ref,
                      q_ref, k_ref, v_ref, u_ref, *, scale):
    x = x_ref[...]
    h = (x * (1.0 + mod_ref[1:2, :]) + mod_ref[0:1, :]).astype(BF16)
    y = _dot(h, w_ref[...])
    cos = cos_ref[...]
    sin = sin_ref[...]
    rep = N_HEADS // N_KV_HEADS
    for hh in range(N_HEADS):
        qh = _rms(y[:, hh * HEAD_DIM:(hh + 1) * HEAD_DIM], qn_ref[...])
        qh = _rope_lanes(qh, cos, sin, HEAD_DIM // 4) * scale
        q_ref[hh // rep, hh % rep] = qh.astype(BF16)
    k0 = N_HEADS * HEAD_DIM
    v0 = k0 + N_KV_HEADS * HEAD_DIM
    for g in range(N_KV_HEADS):
        kh = _rms(y[:, k0 + g * HEAD_DIM:k0 + (g + 1) * HEAD_DIM], kn_ref[...])
        k_ref[g] = _rope_lanes(kh, cos, sin, HEAD_DIM // 4).astype(BF16)
        v_ref[g] = y[:, v0 + g * HEAD_DIM:v0 + (g + 1) * HEAD_DIM].astype(BF16)
    u_ref[...] = y[:, v0 + N_KV_HEADS * HEAD_DIM:].astype(BF16)


def inproj_ab(x, modtab, w_in, q_norm, k_norm, cos, sin, n_ctx):
    b, t, d = x.shape
    n_in = w_in.shape[1]
    s5_ch = n_in - (N_HEADS + 2 * N_KV_HEADS) * HEAD_DIM
    rep = N_HEADS // N_KV_HEADS
    tq = ROW_TILE
    nct = n_ctx // tq
    seg = lambda i: jnp.minimum(i // nct, 1) if nct > 0 else 1
    kern = functools.partial(_inproj_ab_kernel, scale=HEAD_DIM ** -0.5 * LOG2E)
    return pl.pallas_call(
        kern,
        grid=(b, t // tq),
        in_specs=[
            pl.BlockSpec((None, tq, d), lambda bi, i: (bi, i, 0)),
            pl.BlockSpec((None, None, 6, d), lambda bi, i: (bi, seg(i), 0, 0)),
            pl.BlockSpec((d, n_in), lambda bi, i: (0, 0)),
            pl.BlockSpec((1, HEAD_DIM), lambda bi, i: (0, 0)),
            pl.BlockSpec((1, HEAD_DIM), lambda bi, i: (0, 0)),
            pl.BlockSpec((tq, LANES), lambda bi, i: (i, 0)),
            pl.BlockSpec((tq, LANES), lambda bi, i: (i, 0)),
        ],
        out_specs=[
            pl.BlockSpec((None, N_KV_HEADS, rep, tq, HEAD_DIM), lambda bi, i: (bi, 0, 0, i, 0)),
            pl.BlockSpec((None, N_KV_HEADS, tq, HEAD_DIM), lambda bi, i: (bi, 0, i, 0)),
            pl.BlockSpec((None, N_KV_HEADS, tq, HEAD_DIM), lambda bi, i: (bi, 0, i, 0)),
            pl.BlockSpec((None, tq, s5_ch), lambda bi, i: (bi, i, 0)),
        ],
        out_shape=[
            jax.ShapeDtypeStruct((b, N_KV_HEADS, rep, t, HEAD_DIM), BF16),
            jax.ShapeDtypeStruct((b, N_KV_HEADS, t, HEAD_DIM), BF16),
            jax.ShapeDtypeStruct((b, N_KV_HEADS, t, HEAD_DIM), BF16),
            jax.ShapeDtypeStruct((b, t, s5_ch), BF16),
        ],
        compiler_params=_cparams(("parallel", "parallel")),
        name="inproj_ab",
    )(x, modtab, w_in, q_norm.reshape(1, -1), k_norm.reshape(1, -1), cos, sin)


def _inproj_cd_kernel(x_ref, mod_ref, w_ref, qln_ref, kvln_ref, wuq_ref, wukv_ref,
                      cos_ref, sin_ref, cosm_ref, sinm_ref,
                      qm_ref, km_ref, vm_ref, qd_ref, kd_ref, vd_ref, *, scale_c, scale_d, q_lora, kv_lora):
    x = x_ref[...]
    h = (x * (1.0 + mod_ref[1:2, :]) + mod_ref[0:1, :]).astype(BF16)
    y = _dot(h, w_ref[...])
    cos = cos_ref[...]
    sin = sin_ref[...]
    cosm = cosm_ref[...]
    sinm = sinm_ref[...]
    rep = N_HEADS // N_KV_HEADS
    cq = _rms(y[:, :q_lora], qln_ref[...]).astype(BF16)
    ckv = _rms(y[:, q_lora:q_lora + kv_lora], kvln_ref[...]).astype(BF16)
    q = _dot(cq, wuq_ref[...])
    kv = _dot(ckv, wukv_ref[...])
    o = q_lora + kv_lora
    qd0, kd0 = o, o + N_HEADS * HEAD_DIM
    vd0 = kd0 + N_KV_HEADS * HEAD_DIM
    kr0 = vd0 + N_KV_HEADS * HEAD_DIM
    k_rope = _rope_lanes(y[:, kr0:kr0 + LANES], cosm, sinm, C_ROPE // 4)
    for hh in range(N_HEADS):
        qn = q[:, hh * MLA_QK:hh * MLA_QK + HEAD_DIM]
        qr = _rope_lanes(q[:, hh * MLA_QK + HEAD_DIM:(hh + 1) * MLA_QK], cosm, sinm, C_ROPE // 4)
        qm_ref[hh, 0] = (jnp.concatenate([qn, qr], axis=-1) * scale_c).astype(BF16)
        kn = kv[:, hh * HEAD_DIM:(hh + 1) * HEAD_DIM]
        km_ref[hh] = jnp.concatenate([kn, k_rope], axis=-1).astype(BF16)
        vm_ref[hh] = kv[:, (N_HEADS + hh) * HEAD_DIM:(N_HEADS + hh + 1) * HEAD_DIM].astype(BF16)
        qdh = _rope_lanes(y[:, qd0 + hh * HEAD_DIM:qd0 + (hh + 1) * HEAD_DIM], cos, sin, HEAD_DIM // 4)
        qd_ref[hh // rep, hh % rep] = (qdh * scale_d).astype(BF16)
    for g in range(N_KV_HEADS):
        kdh = _rope_lanes(y[:, kd0 + g * HEAD_DIM:kd0 + (g + 1) * HEAD_DIM], cos, sin, HEAD_DIM // 4)
        kd_ref[g] = kdh.astype(BF16)
        vd_ref[g] = y[:, vd0 + g * HEAD_DIM:vd0 + (g + 1) * HEAD_DIM].astype(BF16)


def inproj_cd(x, modtab, w_in, q_ln, kv_ln, w_uq, w_ukv, cos, sin, cosm, sinm, n_ctx):
    b, t, d = x.shape
    n_in = w_in.shape[1]
    q_lora, kv_lora = q_ln.shape[0], kv_ln.shape[0]
    rep = N_HEADS // N_KV_HEADS
    tq = ROW_TILE
    nct = n_ctx // tq
    seg = lambda i: jnp.minimum(i // nct, 1) if nct > 0 else 1
    kern = functools.partial(_inproj_cd_kernel, scale_c=(HEAD_DIM + C_ROPE) ** -0.5 * LOG2E,
                             scale_d=HEAD_DIM ** -0.5 * LOG2E, q_lora=q_lora, kv_lora=kv_lora)
    full = lambda shape: pl.BlockSpec(shape, lambda bi, i: (0,) * len(shape))
    tab = pl.BlockSpec((tq, LANES), lambda bi, i: (i, 0))
    return pl.pallas_call(
        kern,
        grid=(b, t // tq),
        in_specs=[
            pl.BlockSpec((None, tq, d), lambda bi, i: (bi, i, 0)),
            pl.BlockSpec((None, None, 6, d), lambda bi, i: (bi, seg(i), 0, 0)),
            full((d, n_in)), full((1, q_lora)), full((1, kv_lora)),
            full(w_uq.shape), full(w_ukv.shape), tab, tab, tab, tab,
        ],
        out_specs=[
            pl.BlockSpec((None, N_HEADS, 1, tq, MLA_QK), lambda bi, i: (bi, 0, 0, i, 0)),
            pl.BlockSpec((None, N_HEADS, tq, MLA_QK), lambda bi, i: (bi, 0, i, 0)),
            pl.BlockSpec((None, N_HEADS, tq, HEAD_DIM), lambda bi, i: (bi, 0, i, 0)),
            pl.BlockSpec((None, N_KV_HEADS, rep, tq, HEAD_DIM), lambda bi, i: (bi, 0, 0, i, 0)),
            pl.BlockSpec((None, N_KV_HEADS, tq, HEAD_DIM), lambda bi, i: (bi, 0, i, 0)),
            pl.BlockSpec((None, N_KV_HEADS, tq, HEAD_DIM), lambda bi, i: (bi, 0, i, 0)),
        ],
        out_shape=[
            jax.ShapeDtypeStruct((b, N_HEADS, 1, t, MLA_QK), BF16),
            jax.ShapeDtypeStruct((b, N_HEADS, t, MLA_QK), BF16),
            jax.ShapeDtypeStruct((b, N_HEADS, t, HEAD_DIM), BF16),
            jax.ShapeDtypeStruct((b, N_KV_HEADS, rep, t, HEAD_DIM), BF16),
            jax.ShapeDtypeStruct((b, N_KV_HEADS, t, HEAD_DIM), BF16),
            jax.ShapeDtypeStruct((b, N_KV_HEADS, t, HEAD_DIM), BF16),
        ],
        compiler_params=_cparams(("parallel", "parallel")),
        name="inproj_cd",
    )(x, modtab, w_in, q_ln.reshape(1, -1), kv_ln.reshape(1, -1), w_uq, w_ukv, cos, sin, cosm, sinm)


def _prep_cd_weights(w_in, w_uq, w_ukv):
    d = w_in.shape[0]
    q_lora = w_uq.shape[0]
    kv_lora = w_ukv.shape[0]
    o = q_lora + kv_lora
    k_rope = w_in[:, o:o + C_ROPE]
    rest = w_in[:, o + C_ROPE:]
    w_in_r = jnp.concatenate([w_in[:, :o], rest, k_rope, jnp.zeros((d, LANES - C_ROPE), w_in.dtype)], axis=1)
    uq = w_uq.reshape(q_lora, N_HEADS, HEAD_DIM + C_ROPE)
    uq = jnp.pad(uq, ((0, 0), (0, 0), (0, MLA_QK - HEAD_DIM - C_ROPE))).reshape(q_lora, N_HEADS * MLA_QK)
    ukv = w_ukv.reshape(kv_lora, N_HEADS, 2, HEAD_DIM).transpose(0, 2, 1, 3).reshape(kv_lora, 2 * N_HEADS * HEAD_DIM)
    return w_in_r.astype(BF16), uq.astype(BF16), ukv.astype(BF16)


def _softmax_pv(blocks, sink_col):
    mx = None
    for s, _ in blocks:
        bm = jnp.max(s, axis=-1, keepdims=True)
        mx = bm if mx is None else jnp.maximum(mx, bm)
    if sink_col is not None:
        mx = jnp.maximum(mx, sink_col)
    den = None
    acc = None
    for s, v in blocks:
        p = jnp.exp2(s - mx)
        ps = jnp.sum(p, axis=-1, keepdims=True)
        den = ps if den is None else den + ps
        pv = _dot(p.astype(BF16), v)
        acc = pv if acc is None else acc + pv
    if sink_col is not None:
        den = den + jnp.exp2(sink_col - mx)
    return acc / den


def _exp2_pv(x, v):
    p = jnp.exp2(x.astype(BF16))
    return _dot(p, jnp.concatenate([v, jnp.ones((v.shape[0], LANES), BF16)], axis=-1))


def _online_softmax_pv(q, k_ref, v_ref, sink_col):
    n_keys = k_ref.shape[0]
    dv = v_ref.shape[1]
    m = sink_col if sink_col is not None else jnp.full((q.shape[0], 1), -jnp.inf, F32)
    acc = None
    for c0 in range(0, n_keys, KEY_CHUNK):
        c1 = min(c0 + KEY_CHUNK, n_keys)
        s = _dot_nt(q, k_ref[c0:c1, :])
        m_new = jnp.maximum(m, jnp.max(s, axis=-1, keepdims=True))
        pv = _exp2_pv(s - m_new, v_ref[c0:c1, :])
        acc = pv if acc is None else acc * jnp.exp2(m - m_new) + pv
        m = m_new
    den = acc[:, dv:dv + 1]
    if sink_col is not None:
        den = den + jnp.exp2(sink_col - m)
    return acc[:, :dv] / den


def _attn_kernel(sink_ref, q_ref, k_ref, v_ref, o_ref, *, n_ctx, tq, rep, hps, windowed, use_sink):
    gi = pl.program_id(1)
    qi = pl.program_id(2)
    t_all = k_ref.shape[1]
    dk = q_ref.shape[-1]
    dv = v_ref.shape[-1]
    nct = n_ctx // tq

    def head(h, ctx_tile):
        g = gi * hps + h
        q = q_ref[h].reshape(rep * tq, dk)
        kh = k_ref.at[h]
        vh = v_ref.at[h]
        if use_sink:
            row = lax.broadcasted_iota(I32, (rep * tq, 1), 0)
            sink_col = jnp.full((rep * tq, 1), sink_ref[g * rep], F32)
            for r in range(1, rep):
                sink_col = jnp.where(row >= r * tq, sink_ref[g * rep + r], sink_col)
            sink_col = sink_col * LOG2E
        else:
            sink_col = None
        if ctx_tile:
            o = _softmax_pv([(_dot_nt(q, kh[0:n_ctx, :]), vh[0:n_ctx, :])], sink_col)
        elif not windowed:
            o = _online_softmax_pv(q, kh, vh, sink_col)
        else:
            band = tq + 2 * D_WINDOW
            s0 = (qi - nct) * tq
            kstart = jnp.clip(n_ctx + s0 - D_WINDOW, n_ctx, t_all - band)
            kstart = pl.multiple_of(kstart, LANES)
            sb = _dot_nt(q, kh[pl.ds(kstart, band), :])
            rowq = lax.broadcasted_iota(I32, (rep * tq, band), 0) % tq + s0
            colk = lax.broadcasted_iota(I32, (rep * tq, band), 1) + (kstart - n_ctx)
            sb = jnp.where(jnp.abs(colk - rowq) <= D_WINDOW, sb, NEG_INF)
            o = _softmax_pv([(_dot_nt(q, kh[0:n_ctx, :]), vh[0:n_ctx, :]), (sb, vh[pl.ds(kstart, band), :])],
                            sink_col)
        for r in range(rep):
            c0 = (h * rep + r) * dv
            o_ref[:, c0:c0 + dv] = o[r * tq:(r + 1) * tq].astype(o_ref.dtype)

    @pl.when(qi < nct)
    def _():
        for h in range(hps):
            head(h, True)

    @pl.when(qi >= nct)
    def _():
        for h in range(hps):
            head(h, False)


def attention(q, k, v, n_ctx, *, windowed=False, sink=None, hps=ATTN_HEADS_PER_STEP):
    b, g, rep, t, dk = q.shape
    dv = v.shape[-1]
    tq = ROW_TILE
    use_sink = sink is not None
    if sink is None:
        sink = jnp.zeros((g * rep,), F32)
    kern = functools.partial(_attn_kernel, n_ctx=n_ctx, tq=tq, rep=rep, hps=hps, windowed=windowed,
                             use_sink=use_sink)
    return pl.pallas_call(
        kern,
        grid=(b, g // hps, t // tq),
        in_specs=[
            pl.BlockSpec(memory_space=pltpu.SMEM),
            pl.BlockSpec((None, hps, rep, tq, dk), lambda bi, gi, i: (bi, gi, 0, i, 0)),
            pl.BlockSpec((None, hps, t, dk), lambda bi, gi, i: (bi, gi, 0, 0)),
            pl.BlockSpec((None, hps, t, dv), lambda bi, gi, i: (bi, gi, 0, 0)),
        ],
        out_specs=pl.BlockSpec((None, tq, hps * rep * dv), lambda bi, gi, i: (bi, i, gi)),
        out_shape=jax.ShapeDtypeStruct((b, t, g * rep * dv), BF16),
        compiler_params=_cparams(("parallel", "parallel", "arbitrary")),
        name="attention_win" if windowed else "attention",
    )(sink.astype(F32), q, k, v)


def _s5_tables(lam_re, lam_im, log_dt, b_re, b_im, c_re, c_im, d_skip):
    n_groups, n_state = lam_re.shape[1:]
    L, gs = S5_CHUNK, S5_GROUP
    lam = lax.complex(lam_re.astype(F32), lam_im.astype(F32))
    dt = jnp.exp(log_dt.astype(F32))[..., None]
    lam_dt = lam * dt
    lam_bar = jnp.exp(lam_dt)
    b_bar = ((lam_bar - 1.0) / lam)[..., None] * lax.complex(b_re.astype(F32), b_im.astype(F32))
    c_mat = lax.complex(c_re.astype(F32), c_im.astype(F32))
    pw = jnp.exp(lam_dt[None] * jnp.arange(L + 1, dtype=F32)[:, None, None, None])
    kker = jnp.einsum("dgop,tdgp,dgpi->dtgoi", c_mat, pw[:L], b_bar).real
    s_in = jnp.arange(L)[:, None]
    s_out = jnp.arange(L)[None, :]
    tau_f = s_out - s_in
    tau_r = s_in - s_out
    kf = jnp.where((tau_f >= 0)[:, :, None, None, None], kker[0][jnp.clip(tau_f, 0, L - 1)], 0.0)
    kr = jnp.where((tau_r >= 0)[:, :, None, None, None], kker[1][jnp.clip(tau_r, 0, L - 1)], 0.0)
    kt = kf + kr
    gp = S5_PACK
    n_packs = n_groups // gp
    eye = jnp.eye(gp, dtype=F32)
    t6 = kt.transpose(2, 0, 4, 1, 3).reshape(n_packs, gp, L, gs, L, gs)
    d6 = d_skip.astype(F32).reshape(n_packs, gp, 1, gs, 1, 1) * (
        jnp.eye(L, dtype=F32)[None, None, :, None, :, None] * jnp.eye(gs, dtype=F32)[None, None, None, :, None, :])
    tmat = jnp.einsum("pgsctd,gh->psgcthd", t6 + d6, eye).reshape(n_packs, L * LANES, L * LANES)
    steps = jnp.arange(L, dtype=F32)[:, None, None]
    wf = jnp.exp(lam_dt[0][None] * (L - 1 - steps))[:, :, :, None] * b_bar[0][None]
    wr = jnp.exp(lam_dt[1][None] * steps)[:, :, :, None] * b_bar[1][None]
    def w_pack(w):
        w2 = jnp.stack([w.real, w.imag], axis=0).reshape(2, L, n_packs, gp, n_state, gs)
        return jnp.einsum("bspgqc,gh->psgcbhq", w2, eye).reshape(n_packs, L * LANES, 2 * gp * n_state)
    vf = c_mat[0][None] * jnp.exp(lam_dt[0][None] * (steps + 1))[:, :, None, :]
    vr = c_mat[1][None] * jnp.exp(lam_dt[1][None] * (L - steps))[:, :, None, :]
    def v_pack(vv):
        v2 = jnp.stack([vv.real, -vv.imag], axis=0).reshape(2, L, n_packs, gp, gs, n_state)
        return jnp.einsum("bspgcq,gh->pbgqshc", v2, eye).reshape(n_packs, 2 * gp * n_state, L * LANES)
    def lam_pack(l):
        return jnp.stack([l.real, l.imag], axis=0).reshape(2, n_packs, gp * n_state).transpose(1, 0, 2)
    lam_l = pw[L]
    return (tmat.astype(BF16),
            (w_pack(wf).astype(BF16), v_pack(vf).astype(BF16), lam_pack(lam_l[0])),
            (w_pack(wr).astype(BF16), v_pack(vr).astype(BF16), lam_pack(lam_l[1])))


def _s5_pass_kernel(*refs, reverse):
    if reverse:
        u_ref, yin_ref, w_ref, v_ref, lam_ref, y_ref, s_scr, h_scr = refs
    else:
        u_ref, t_ref, w_ref, v_ref, lam_ref, y_ref, s_scr, h_scr = refs

    @pl.when(pl.program_id(1) == 0)
    def _():
        h_scr[...] = jnp.zeros_like(h_scr)

    nchs, L, b, _ = u_ref.shape
    rows = nchs * b
    half = h_scr.shape[1] // 2
    xg = jnp.concatenate([u_ref[:, s].reshape(rows, LANES) for s in range(L)], axis=-1)
    s_scr[...] = _dot(xg, w_ref[...])
    lr = jnp.broadcast_to(lam_ref[0:1, :], (b, half))
    li = jnp.broadcast_to(lam_ref[1:2, :], (b, half))

    def step(k, carry):
        hr, hi = carry
        j = nchs - 1 - k if reverse else k
        r0 = pl.multiple_of(j * b, b)
        sr = s_scr[pl.ds(r0, b), 0:half]
        si = s_scr[pl.ds(r0, b), half:]
        s_scr[pl.ds(r0, b), 0:half] = hr
        s_scr[pl.ds(r0, b), half:] = hi
        return lr * hr - li * hi + sr, lr * hi + li * hr + si

    hr, hi = lax.fori_loop(0, nchs, step, (h_scr[:, 0:half], h_scr[:, half:]))
    h_scr[:, 0:half] = hr
    h_scr[:, half:] = hi
    y = _dot(s_scr[...].astype(BF16), v_ref[...])
    if not reverse:
        y = y + _dot(xg, t_ref[...])
    for s in range(L):
        blk = y[:, s * LANES:(s + 1) * LANES].reshape(nchs, b, LANES)
        if reverse:
            blk = blk + yin_ref[:, s]
        y_ref[:, s] = blk


def s5_apply(u, tables, n_ctx):
    tmat, fwd, rev = tables
    b, t, ch = u.shape
    L = S5_CHUNK
    n_packs = ch // LANES
    seg = ROW_TILE
    nseg, nct, nchs = t // seg, n_ctx // seg, seg // L
    width = L * LANES
    n_state2 = fwd[0].shape[-1]
    ut = u.transpose(1, 0, 2).reshape(t // L, L, b, ch)
    blk = pl.BlockSpec((nchs, L, b, LANES), lambda p, i: (i, 0, 0, p))
    rseg = lambda i: jnp.where(i < nct, nct - 1 - i, nseg - 1 - (i - nct))
    rblk = pl.BlockSpec((nchs, L, b, LANES), lambda p, i: (rseg(i), 0, 0, p))
    tab = lambda shape: pl.BlockSpec((None,) + shape, lambda p, i: (p, 0, 0))
    scratch = [pltpu.VMEM((nchs * b, n_state2), F32), pltpu.VMEM((b, n_state2), F32)]
    y_shape = jax.ShapeDtypeStruct((t // L, L, b, ch), F32)
    y_f = pl.pallas_call(
        functools.partial(_s5_pass_kernel, reverse=False),
        grid=(n_packs, nseg),
        in_specs=[blk, tab((width, width)), tab((width, n_state2)), tab((n_state2, width)), tab((2, n_state2 // 2))],
        out_specs=blk,
        out_shape=y_shape,
        scratch_shapes=scratch,
        compiler_params=_cparams(("parallel", "arbitrary")),
        name="s5_forward",
    )(ut, tmat, *fwd)
    y = pl.pallas_call(
        functools.partial(_s5_pass_kernel, reverse=True),
        grid=(n_packs, nseg),
        in_specs=[rblk, rblk, tab((width, n_state2)), tab((n_state2, width)), tab((2, n_state2 // 2))],
        out_specs=rblk,
        out_shape=y_shape,
        scratch_shapes=scratch,
        input_output_aliases={1: 0},
        compiler_params=_cparams(("parallel", "arbitrary")),
        name="s5_reverse",
    )(ut, y_f, *rev)
    return y.reshape(t, b, ch).transpose(1, 0, 2)


def _gelu_tanh(x):
    return 0.5 * x * (1.0 + jnp.tanh(math.sqrt(2.0 / math.pi) * (x + 0.044715 * (x * x * x))))


def _post_kernel(*refs, alpha, glu, n_exp):
    if glu:
        (x_ref, a_ref, s_ref, wg_ref, bg_ref, wm_ref, mod_ref, g_ref, b_ref, wrh_ref, wrl_ref, br_ref,
         x1_ref, idx_ref, gate_ref, rank_ref, cnt_ref, cnt_scr) = refs
    else:
        (x_ref, a_ref, s_ref, wm_ref, mod_ref, g_ref, b_ref, wrh_ref, wrl_ref, br_ref,
         x1_ref, idx_ref, gate_ref, rank_ref, cnt_ref, cnt_scr) = refs
    first = (pl.program_id(0) == 0) & (pl.program_id(1) == 0)

    @pl.when(first)
    def _():
        cnt_scr[...] = jnp.zeros_like(cnt_scr)

    tq, d = x_ref.shape
    half = a_ref.shape[1]
    if glu:
        z = _gelu_tanh(s_ref[...])
        gate = jax.nn.sigmoid(_dot(z.astype(BF16), wg_ref[...]) + bg_ref[...])
        second = (z * gate).astype(BF16)
    else:
        second = s_ref[...]
    mix = _dot(a_ref[...], wm_ref[0:half, :]) + _dot(second, wm_ref[half:, :])
    x1 = _layer_norm(alpha * x_ref[...] + mod_ref[2:3, :] * mix, g_ref[...], b_ref[...])
    x1_ref[...] = x1
    h2 = x1 * (1.0 + mod_ref[4:5, :]) + mod_ref[3:4, :]

    h_hi, h_lo = _split_bf16(h2)
    logits = _dot_nt(wrh_ref[...], h_hi) + _dot_nt(wrh_ref[...], h_lo) + _dot_nt(wrl_ref[...], h_hi) + br_ref[...]
    eidx = lax.broadcasted_iota(I32, (n_exp, tq), 0)
    work = logits
    tops, sels = [], []
    for k in range(TOP_K):
        m = jnp.max(work, axis=0, keepdims=True)
        ik = jnp.min(jnp.where(work == m, eidx, n_exp), axis=0, keepdims=True)
        sel = eidx == ik
        work = jnp.where(sel, -jnp.inf, work)
        tops.append(m)
        sels.append(sel)
        idx_ref[k:k + 1, :] = ik
    exps = [jnp.exp(tk - tops[0]) for tk in tops]
    den = exps[0] + exps[1] + exps[2] + exps[3]
    for k in range(TOP_K):
        gate_ref[k:k + 1, :] = exps[k] / den
    onehot = jnp.zeros((n_exp, tq), F32)
    for sel in sels:
        onehot = onehot + sel.astype(F32)
    tri = (lax.broadcasted_iota(I32, (tq, tq), 0) < lax.broadcasted_iota(I32, (tq, tq), 1)).astype(BF16)
    before = _dot(onehot.astype(BF16), tri) + cnt_scr[...]
    for k in range(TOP_K):
        rk = jnp.sum(jnp.where(sels[k], before, 0.0), axis=0, keepdims=True)
        rank_ref[k:k + 1, :] = rk.astype(I32)
    cnt_scr[...] += jnp.sum(onehot, axis=1, keepdims=True)
    cnt_ref[...] = jnp.broadcast_to(cnt_scr[...], cnt_ref.shape).astype(I32)


def post_mixer(x, a, s, modtab, w_mix, ln_g, ln_b, w_router, b_router, n_ctx, alpha, glu_w=None, glu_b=None):
    b, t, d = x.shape
    half = a.shape[-1]
    n_exp = w_router.shape[1]
    tq = ROW_TILE
    nct = n_ctx // tq
    seg = lambda i: jnp.minimum(i // nct, 1) if nct > 0 else 1
    glu = glu_w is not None
    wr_hi, wr_lo = _split_bf16(w_router.T.astype(F32))
    full = lambda shape: pl.BlockSpec(shape, lambda bi, i: (0,) * len(shape))
    tok = lambda width: pl.BlockSpec((None, tq, width), lambda bi, i: (bi, i, 0))
    in_specs = [tok(d), tok(half), tok(half)]
    args = [x, a, s]
    if glu:
        in_specs += [full((half, half)), full((1, half))]
        args += [glu_w.astype(BF16), glu_b.reshape(1, half)]
    in_specs += [full((d, d)), pl.BlockSpec((None, None, 6, d), lambda bi, i: (bi, seg(i), 0, 0)),
                 full((1, d)), full((1, d)), full((n_exp, d)), full((n_exp, d)), full((n_exp, 1))]
    args += [w_mix.astype(BF16), modtab, ln_g.reshape(1, d), ln_b.reshape(1, d), wr_hi, wr_lo,
             b_router.reshape(n_exp, 1)]
    nt = t // tq
    lane_out = lambda dt: (pl.BlockSpec((TOP_K, tq), lambda bi, i: (0, bi * nt + i)),
                           jax.ShapeDtypeStruct((TOP_K, b * t), dt))
    outs = [
        (tok(d), jax.ShapeDtypeStruct((b, t, d), F32)),
        lane_out(I32), lane_out(F32), lane_out(I32),
        (pl.BlockSpec((n_exp, LANES), lambda bi, i: (0, 0)), jax.ShapeDtypeStruct((n_exp, LANES), I32)),
    ]
    kern = functools.partial(_post_kernel, alpha=alpha, glu=glu, n_exp=n_exp)
    return pl.pallas_call(
        kern,
        grid=(b, nt),
        in_specs=in_specs,
        out_specs=[o[0] for o in outs],
        out_shape=[o[1] for o in outs],
        scratch_shapes=[pltpu.VMEM((n_exp, 1), F32)],
        compiler_params=_cparams(("arbitrary", "arbitrary")),
        name="post_mixer_glu" if glu else "post_mixer",
    )(*args)


def _route_kernel(start_ref, idx_ref, rank_ref, pos_ref, *, n_exp, tq):
    n = idx_ref.shape[1]
    eidx = lax.broadcasted_iota(I32, (n_exp, n), 0)
    for k in range(TOP_K):
        sel = eidx == idx_ref[k:k + 1, :]
        base = jnp.sum(jnp.where(sel, start_ref[...], 0.0), axis=0, keepdims=True)
        pos = base.astype(I32) + rank_ref[k:k + 1, :]
        for ti in range(n // tq):
            pos_ref[ti, :, k * tq:(k + 1) * tq] = pos[:, ti * tq:(ti + 1) * tq]


def route_positions(group_start, idx, rank):
    n_exp = group_start.shape[0]
    n_tok = idx.shape[1]
    tq = ROW_TILE
    n_tiles = n_tok // tq
    per_step = math.gcd(n_tiles, 8)
    return pl.pallas_call(
        functools.partial(_route_kernel, n_exp=n_exp, tq=tq),
        grid=(n_tiles // per_step,),
        in_specs=[
            pl.BlockSpec((n_exp, 1), lambda i: (0, 0)),
            pl.BlockSpec((TOP_K, per_step * tq), lambda i: (0, i)),
            pl.BlockSpec((TOP_K, per_step * tq), lambda i: (0, i)),
        ],
        out_specs=pl.BlockSpec((per_step, 1, TOP_K * tq), lambda i: (i, 0, 0)),
        out_shape=jax.ShapeDtypeStruct((n_tiles, 1, TOP_K * tq), I32),
        compiler_params=_cparams(("parallel",)),
        name="route_positions",
    )(group_start.astype(F32).reshape(n_exp, 1), idx, rank)


def _dispatch_kernel(pos_hbm, x_ref, mod_ref, xs_hbm, idx_smem, hbuf0, hbuf1, idx_sem, row_sem, *, tq, n_tiles):
    i = pl.program_id(0)
    slot = i % 2
    n_sub = x_ref.shape[1] // LANES
    bufs = (hbuf0, hbuf1)

    def idx_copy(tile, sl):
        return pltpu.make_async_copy(pos_hbm.at[tile], idx_smem.at[sl], idx_sem.at[sl])

    def rows_wait(sl):
        for _ in range(TOP_K):
            pltpu.make_async_copy(bufs[sl], xs_hbm.at[pl.ds(0, tq * n_sub), :], row_sem.at[sl]).wait()

    @pl.when(i == 0)
    def _():
        idx_copy(0, 0).start()

    idx_copy(i, slot).wait()

    @pl.when(i + 1 < n_tiles)
    def _():
        idx_copy(i + 1, 1 - slot).start()

    def step(cur):
        @pl.when(i >= 2)
        def _():
            rows_wait(cur)

        h2 = x_ref[...] * (1.0 + mod_ref[4:5, :]) + mod_ref[3:4, :]
        _store_token_tiles(bufs[cur], h2)
        for k in range(TOP_K):
            for r in range(tq):
                dst = pl.multiple_of(idx_smem[cur, 0, k * tq + r] * n_sub, n_sub)
                pltpu.make_async_copy(bufs[cur].at[pl.ds(r * n_sub, n_sub), :],
                                      xs_hbm.at[pl.ds(dst, n_sub), :], row_sem.at[cur]).start(priority=r % 2)

        @pl.when(i == n_tiles - 1)
        def _():
            rows_wait(1 - cur)
            rows_wait(cur)

    @pl.when(slot == 0)
    def _():
        step(0)

    @pl.when(slot == 1)
    def _():
        step(1)


def moe_dispatch(pos_tiles, x1, modtab, n_ctx):
    b, t, d = x1.shape
    n_tiles = pos_tiles.shape[0]
    tq = ROW_TILE
    nt = t // tq
    nct = n_ctx // tq
    seg = lambda i: jnp.minimum((i % nt) // nct, 1) if nct > 0 else 1
    n_sub = d // LANES
    assert n_tiles == b * nt and n_tiles >= 2
    return pl.pallas_call(
        functools.partial(_dispatch_kernel, tq=tq, n_tiles=n_tiles),
        grid=(n_tiles,),
        in_specs=[pl.BlockSpec(memory_space=pl.ANY),
                  pl.BlockSpec((tq, d), lambda i: (i, 0)),
                  pl.BlockSpec((None, None, 6, d), lambda i: (i // nt, seg(i), 0, 0))],
        out_specs=pl.BlockSpec(memory_space=pl.ANY),
        out_shape=jax.ShapeDtypeStruct((TOP_K * b * t * n_sub, LANES), F32),
        scratch_shapes=[
            pltpu.SMEM((2, 1, TOP_K * tq), I32),
            pltpu.VMEM((tq * n_sub, LANES), F32),
            pltpu.VMEM((tq * n_sub, LANES), F32),
            pltpu.SemaphoreType.DMA((2,)),
            pltpu.SemaphoreType.DMA((2,)),
        ],
        compiler_params=_cparams(("arbitrary",)),
        name="moe_dispatch",
    )(pos_tiles, x1.reshape(b * t, d), modtab)


def _moe_kernel(e_ref, j_ref, lo_ref, hi_ref, first_ref, new_ref, x_ref, wi_ref, bi_ref, wo_ref, bo_ref,
                y_ref, wi_b, wo_b, *, n_sub):
    i = pl.program_id(0)
    lo = lo_ref[i]
    hi = hi_ref[i]

    @pl.when(new_ref[i] == 1)
    def _():
        wi_b[...] = wi_ref[...].astype(BF16)
        wo_b[...] = wo_ref[...].astype(BF16)

    tm = x_ref.shape[0] // n_sub
    sub = tm // MOE_CHAINS

    def expert_rows(r0):
        x = _load_token_tiles(x_ref, r0 * n_sub, sub, n_sub).astype(BF16)
        z = _dot(x, wi_b[...]) + bi_ref[...]
        f = z.shape[1] // 2
        glu = jnp.minimum(z[:, :f], SWIGLU_LIMIT)
        lin = jnp.clip(z[:, f:], -SWIGLU_LIMIT, SWIGLU_LIMIT)
        act = glu * jax.nn.sigmoid(SWIGLU_ALPHA * glu) * (lin + 1.0)
        y = _dot(act.astype(BF16), wo_b[...]) + bo_ref[...]
        row = lax.broadcasted_iota(I32, (sub, 1), 0) + r0
        return y, (row >= lo) & (row < hi)

    def out_rows(r0):
        return y_ref.at[pl.ds(r0 * n_sub, sub * n_sub), :]

    @pl.when((hi > lo) & (first_ref[i] == 1))
    def _():
        for r0 in range(0, tm, sub):
            y, mine = expert_rows(r0)
            _store_token_tiles(out_rows(r0), jnp.where(mine, y, 0.0))

    @pl.when((hi > lo) & (first_ref[i] == 0))
    def _():
        for r0 in range(0, tm, sub):
            y, mine = expert_rows(r0)
            _store_token_tiles(out_rows(r0), jnp.where(mine, y, _load_token_tiles(out_rows(r0), 0, sub, n_sub)))


def moe_experts(xs, items, w_in, b_in, w_out, b_out, layer):
    _, n_exp, d, f2 = w_in.shape
    n_sub = d // LANES
    n_items = items[0].shape[0]
    tm = MOE_BLOCK
    wmap = lambda i, e, j, lo, hi, fi, nw: (layer, e[i], 0, 0)
    xmap = lambda i, e, j, lo, hi, fi, nw: (j[i], 0)
    grid_spec = pltpu.PrefetchScalarGridSpec(
        num_scalar_prefetch=6,
        grid=(n_items,),
        in_specs=[
            pl.BlockSpec((tm * n_sub, LANES), xmap),
            pl.BlockSpec((None, None, d, f2), wmap),
            pl.BlockSpec((None, None, 1, f2), wmap),
            pl.BlockSpec((None, None, f2 // 2, d), wmap),
            pl.BlockSpec((None, None, 1, d), wmap),
        ],
        out_specs=pl.BlockSpec((tm * n_sub, LANES), xmap),
        scratch_shapes=[pltpu.VMEM((d, f2), BF16), pltpu.VMEM((f2 // 2, d), BF16)],
    )
    return pl.pallas_call(
        functools.partial(_moe_kernel, n_sub=n_sub),
        grid_spec=grid_spec,
        out_shape=jax.ShapeDtypeStruct(xs.shape, F32),
        compiler_params=_cparams(("arbitrary",)),
        name="moe_experts",
    )(*items, xs, w_in, b_in.reshape(b_in.shape[0], n_exp, 1, f2), w_out, b_out.reshape(b_out.shape[0], n_exp, 1, d))


def _combine_kernel(pos_hbm, x_ref, g_ref, mod_ref, lg_ref, lb_ref, ys_hbm, o_ref,
                    idx_smem, ybuf0, ybuf1, idx_sem, row_sem, *, alpha, tq, n_tiles):
    i = pl.program_id(0)
    slot = i % 2
    d = x_ref.shape[1]
    n_sub = d // LANES
    bufs = (ybuf0, ybuf1)

    def idx_copy(tile, sl):
        return pltpu.make_async_copy(pos_hbm.at[tile], idx_smem.at[sl], idx_sem.at[sl])

    def gather_start(sl, rows):
        for r in rows:
            src = pl.multiple_of(idx_smem[sl, 0, r] * n_sub, n_sub)
            pltpu.make_async_copy(ys_hbm.at[pl.ds(src, n_sub), :], bufs[sl].at[pl.ds(r * n_sub, n_sub), :],
                                  row_sem.at[sl]).start(priority=r % 2)

    def rows_wait(sl):
        pltpu.make_async_copy(ys_hbm.at[pl.ds(0, TOP_K * tq * n_sub), :], bufs[sl], row_sem.at[sl]).wait()

    @pl.when(i == 0)
    def _():
        idx_copy(0, 0).start()
        idx_copy(0, 0).wait()
        gather_start(0, range(TOP_K * tq))
        idx_copy(1, 1).start()

    @pl.when(i + 1 < n_tiles)
    def _():
        idx_copy(i + 1, 1 - slot).wait()

    def step(cur):
        nxt = 1 - cur
        rows_wait(cur)

        @pl.when(i + 2 < n_tiles)
        def _():
            idx_copy(i + 2, cur).start()

        gates = g_ref[...]
        per_chunk = TOP_K * tq // n_sub
        parts = []
        for c in range(n_sub):
            gather_start(nxt, range(c * per_chunk, (c + 1) * per_chunk))
            acc = None
            for k in range(TOP_K):
                term = gates[:, k:k + 1] * bufs[cur][pl.ds(k * tq * n_sub + c, tq, stride=n_sub), :]
                acc = term if acc is None else acc + term
            parts.append(acc)
        y = jnp.concatenate(parts, axis=-1)
        o_ref[...] = _layer_norm(alpha * x_ref[...] + mod_ref[5:6, :] * y, lg_ref[...], lb_ref[...])

        @pl.when(i == n_tiles - 1)
        def _():
            rows_wait(nxt)

    @pl.when(slot == 0)
    def _():
        step(0)

    @pl.when(slot == 1)
    def _():
        step(1)


def moe_combine(x1, pos_tiles, gates_tok, modtab, ln_g, ln_b, ys, n_ctx, alpha, lat_only=False):
    b, t, d = x1.shape
    tq = ROW_TILE
    nt = t // tq
    n_tiles = b * nt
    nct = n_ctx // tq
    seg = lambda i: jnp.minimum((i % nt) // nct, 1) if nct > 0 else 1
    assert n_tiles >= 2, "the combine pipeline prefetches one tile ahead"
    kern = functools.partial(_combine_kernel, alpha=alpha, tq=tq, n_tiles=n_tiles)
    if lat_only:
        nlt = nt - nct
        out_map = lambda i: ((i // nt) * nlt + jnp.maximum(i % nt - nct, 0), 0)
        out_rows = b * nlt * tq
    else:
        out_map = lambda i: (i, 0)
        out_rows = b * t
    out = pl.pallas_call(
        kern,
        grid=(n_tiles,),
        in_specs=[
            pl.BlockSpec(memory_space=pl.ANY),
            pl.BlockSpec((tq, d), lambda i: (i, 0)),
            pl.BlockSpec((tq, TOP_K), lambda i: (i, 0)),
            pl.BlockSpec((None, None, 6, d), lambda i: (i // nt, seg(i), 0, 0)),
            pl.BlockSpec((1, d), lambda i: (0, 0)),
            pl.BlockSpec((1, d), lambda i: (0, 0)),
            pl.BlockSpec(memory_space=pl.ANY),
        ],
        out_specs=pl.BlockSpec((tq, d), out_map),
        out_shape=jax.ShapeDtypeStruct((out_rows, d), F32),
        scratch_shapes=[
            pltpu.SMEM((2, 1, TOP_K * tq), I32),
            pltpu.VMEM((TOP_K * tq * (d // LANES), LANES), F32),
            pltpu.VMEM((TOP_K * tq * (d // LANES), LANES), F32),
            pltpu.SemaphoreType.DMA((2,)),
            pltpu.SemaphoreType.DMA((2,)),
        ],
        compiler_params=_cparams(("arbitrary",)),
        name="moe_combine",
    )(pos_tiles, x1.reshape(b * t, d), gates_tok, modtab, ln_g.reshape(1, d), ln_b.reshape(1, d), ys)
    return out.reshape(b, out_rows // b, d)


def _group_items(counts, n_slots, tm):
    n_exp = counts.shape[0]
    le = (jnp.arange(n_exp)[:, None] <= jnp.arange(n_exp)[None, :]).astype(I32)
    g_end = counts @ le
    g_start = g_end - counts
    first_blk = g_start // tm
    n_blk = jnp.where(counts > 0, (g_end - 1) // tm - first_blk + 1, 0)
    i_end = n_blk @ le
    i_start = i_end - n_blk
    n_items = n_slots // tm + n_exp - 1
    it = jnp.arange(n_items, dtype=I32)
    valid = it < i_end[-1]
    e = jnp.minimum(jnp.sum((i_end[None, :] <= it[:, None]).astype(I32), axis=1), n_exp - 1)
    onehot = (e[:, None] == jnp.arange(n_exp)[None, :]).astype(I32)
    pick = lambda tbl: onehot @ tbl
    blk = pick(first_blk) + it - pick(i_start)
    lo = jnp.clip(pick(g_start) - blk * tm, 0, tm)
    hi = jnp.clip(pick(g_end) - blk * tm, 0, tm)
    last = jnp.maximum(i_end[-1] - 1, 0)
    e_last = jnp.sum(jnp.where(it == last, e, 0))
    e = jnp.where(valid, e, e_last)
    blk = jnp.where(valid, blk, n_slots // tm - 1)
    lo = jnp.where(valid, lo, 0)
    hi = jnp.where(valid, hi, 0)
    prev = lambda a: jnp.concatenate([jnp.full((1,), -1, I32), a[:-1]])
    first = (blk != prev(blk)).astype(I32)
    new_e = (e != prev(e)).astype(I32)
    return g_start, tuple(a.astype(I32) for a in (e, blk, lo, hi, first, new_e))


def kernel(x, c, ctx, c_ctx, w_mod, b_mod, ln_g, ln_b, w_mix_out, w_router, b_router, w_exp_in, b_exp_in, w_exp_out, b_exp_out, w_in_ab, a_q_norm, a_k_norm, s5_lam_re, s5_lam_im, s5_log_dt, s5_b_re, s5_b_im, s5_c_re, s5_c_im, s5_d, w_glu, b_glu, w_in_cd, c_q_norm, c_kv_norm, w_uq, w_ukv, d_sink):
    batch, n_lat, d = x.shape
    n_ctx = ctx.shape[1]
    depth = w_mod.shape[0]
    alpha = (2.0 * depth) ** 0.25
    t = n_ctx + n_lat
    n_tok = batch * t

    xs = jnp.concatenate([ctx, x], axis=1)
    n_rows = -(-(batch + 1) // SUBLANES) * SUBLANES
    cvec = jnp.zeros((n_rows, d), F32).at[:batch].set(c).at[batch].set(c_ctx)
    mods = modulation(cvec, w_mod, b_mod).reshape(depth, n_rows, 6, d)
    modtab = jnp.stack([jnp.broadcast_to(mods[:, batch:batch + 1], (depth, batch, 6, d)), mods[:, :batch]],
                       axis=2)

    cos_h, sin_h = _rope_tables(n_ctx, n_lat, HEAD_DIM)
    cos_m, sin_m = _rope_tables(n_ctx, n_lat, C_ROPE)

    for layer in range(depth):
        i = layer // 2
        mt = modtab[layer]
        if layer % 2 == 0:
            q, k, v, u = inproj_ab(xs, mt, w_in_ab[i].astype(BF16), a_q_norm[i], a_k_norm[i], cos_h, sin_h, n_ctx)
            att = attention(q, k, v, n_ctx)
            tables = _s5_tables(s5_lam_re[i], s5_lam_im[i], s5_log_dt[i], s5_b_re[i], s5_b_im[i],
                                s5_c_re[i], s5_c_im[i], s5_d[i])
            second = s5_apply(u, tables, n_ctx)
            glu_w, glu_b = w_glu[i], b_glu[i]
        else:
            w_in_r, uq_r, ukv_r = _prep_cd_weights(w_in_cd[i], w_uq[i], w_ukv[i])
            qm, km, vm, qd, kd, vd = inproj_cd(xs, mt, w_in_r, c_q_norm[i], c_kv_norm[i], uq_r, ukv_r,
                                               cos_h, sin_h, cos_m, sin_m, n_ctx)
            att = attention(qm, km, vm, n_ctx, hps=N_HEADS)
            second = attention(qd, kd, vd, n_ctx, windowed=True, sink=d_sink[i])
            glu_w = glu_b = None
        x1, idx, gates, rank, counts = post_mixer(
            xs, att, second, mt, w_mix_out[layer], ln_g[layer, 0], ln_b[layer, 0],
            w_router[layer], b_router[layer], n_ctx, alpha, glu_w, glu_b)
        g_start, items = _group_items(counts[:, 0], TOP_K * n_tok, MOE_BLOCK)
        pos_tiles = route_positions(g_start, idx, rank)
        sorted_rows = moe_dispatch(pos_tiles, x1, mt, n_ctx)
        ys = moe_experts(sorted_rows, items, w_exp_in, b_exp_in, w_exp_out, b_exp_out, layer)
        xs = moe_combine(x1, pos_tiles, gates.T, mt, ln_g[layer, 1], ln_b[layer, 1], ys, n_ctx, alpha,
                         lat_only=layer == depth - 1)
    return xs
```

```python
import functools
import math

import jax
import jax.numpy as jnp
from jax import lax
from jax.experimental import pallas as pl
from jax.experimental.pallas import tpu as pltpu

F32 = jnp.float32
BF16 = jnp.bfloat16
I32 = jnp.int32

GRID_W = 64
ROPE_THETA = 10000.0
HEAD_DIM = 128
N_HEADS = 4
N_KV_HEADS = 2
C_ROPE = 64
MLA_QK = 256
D_WINDOW = 128
S5_GROUP = 16
S5_STATE = 64
S5_CHUNK = 8
S5_PACK = 8
TOP_K = 4
SWIGLU_LIMIT = 7.0
SWIGLU_ALPHA = 1.702
NEG_INF = -1e30
LOG2E = 1.4426950408889634
LN_EPS = 1e-5
RMS_EPS = 1e-6

LANES = 128
SUBLANES = 8
VMEM_LIMIT_BYTES = 56 * 1024 * 1024

ROW_TILE = 256
MOE_BLOCK = 512
MOE_CHAINS = 2
KEY_CHUNK = 768
ATTN_HEADS_PER_STEP = 2


def _cparams(sem):
    return pltpu.CompilerParams(dimension_semantics=sem, vmem_limit_bytes=VMEM_LIMIT_BYTES)


def _split_bf16(a):
    hi = a.astype(BF16)
    lo = (a - hi.astype(F32)).astype(BF16)
    return hi, lo


def _dot(a, b):
    return jnp.dot(a, b, preferred_element_type=F32)


def _dot_nt(a, b):
    return lax.dot_general(a, b, (((1,), (1,)), ((), ())), preferred_element_type=F32)


def _store_token_tiles(ref, val):
    rows, d = val.shape
    n = d // LANES
    for s in range(n):
        ref[pl.ds(s, rows, stride=n), :] = val[:, s * LANES:(s + 1) * LANES]


def _load_token_tiles(ref, first_row, rows, n):
    return jnp.concatenate([ref[pl.ds(first_row + s, rows, stride=n), :] for s in range(n)], axis=-1)


def _layer_norm(x, g, b):
    mu = jnp.mean(x, axis=-1, keepdims=True)
    xc = x - mu
    var = jnp.mean(xc * xc, axis=-1, keepdims=True)
    return xc * lax.rsqrt(var + LN_EPS) * g + b


def _rms(x, g):
    return x * lax.rsqrt(jnp.mean(x * x, axis=-1, keepdims=True) + RMS_EPS) * g


def _mod_kernel(c_ref, w_ref, b_ref, o_ref):
    c = c_ref[...]
    s = c * jax.nn.sigmoid(c)
    s_hi, s_lo = _split_bf16(s)
    w_hi, w_lo = _split_bf16(w_ref[...])
    o_ref[...] = _dot(s_hi, w_hi) + _dot(s_lo, w_hi) + _dot(s_hi, w_lo) + b_ref[...]


def modulation(cvec, w_mod, b_mod):
    n_layers, d, d6 = w_mod.shape
    r = cvec.shape[0]
    tn = 1536
    return pl.pallas_call(
        _mod_kernel,
        grid=(n_layers, d6 // tn),
        in_specs=[
            pl.BlockSpec((r, d), lambda l, j: (0, 0)),
            pl.BlockSpec((None, d, tn), lambda l, j: (l, 0, j)),
            pl.BlockSpec((None, 1, tn), lambda l, j: (l, 0, j)),
        ],
        out_specs=pl.BlockSpec((None, r, tn), lambda l, j: (l, 0, j)),
        out_shape=jax.ShapeDtypeStruct((n_layers, r, d6), F32),
        compiler_params=_cparams(("arbitrary", "arbitrary")),
        name="modulation",
    )(cvec, w_mod, b_mod.reshape(n_layers, 1, d6))


def _rope_lanes(t, cos, sin, quarter):
    lane = lax.broadcasted_iota(I32, t.shape, 1)
    first = (lane % (2 * quarter)) < quarter
    partner = jnp.where(first, pltpu.roll(t, LANES - quarter, 1), pltpu.roll(t, quarter, 1))
    return t * cos + partner * sin


def _rope_tables(n_ctx, n_lat, rot_dim):
    quarter = rot_dim // 4
    pos = jnp.arange(n_lat, dtype=F32)
    row = jnp.floor(pos / GRID_W)
    col = pos - row * GRID_W
    inv_freq = ROPE_THETA ** (-jnp.arange(quarter, dtype=F32) / quarter)
    ang_r = row[:, None] * inv_freq
    ang_c = col[:, None] * inv_freq
    cos = jnp.concatenate([jnp.cos(ang_r)] * 2 + [jnp.cos(ang_c)] * 2, axis=-1)
    sin = jnp.concatenate([-jnp.sin(ang_r), jnp.sin(ang_r), -jnp.sin(ang_c), jnp.sin(ang_c)], axis=-1)
    pad = LANES - rot_dim
    cos = jnp.pad(cos, ((n_ctx, 0), (0, pad)), constant_values=1.0)
    sin = jnp.pad(sin, ((n_ctx, 0), (0, pad)))
    return cos, sin


def _inproj_ab_kernel(x_ref, mod_ref, w_ref, qn_ref, kn_ref, cos_ref, sin_ref,
                      q_ref, k_ref, v_ref, u_ref, *, scale):
    x = x_ref[...]
    h = (x * (1.0 + mod_ref[1:2, :]) + mod_ref[0:1, :]).astype(BF16)
    y = _dot(h, w_ref[...])
    cos = cos_ref[...]
    sin = sin_ref[...]
    rep = N_HEADS // N_KV_HEADS
    for hh in range(N_HEADS):
        qh = _rms(y[:, hh * HEAD_DIM:(hh + 1) * HEAD_DIM], qn_ref[...])
        qh = _rope_lanes(qh, cos, sin, HEAD_DIM // 4) * scale
        q_ref[hh // rep, hh % rep] = qh.astype(BF16)
    k0 = N_HEADS * HEAD_DIM
    v0 = k0 + N_KV_HEADS * HEAD_DIM
    for g in range(N_KV_HEADS):
        kh = _rms(y[:, k0 + g * HEAD_DIM:k0 + (g + 1) * HEAD_DIM], kn_ref[...])
        k_ref[g] = _rope_lanes(kh, cos, sin, HEAD_DIM // 4).astype(BF16)
        v_ref[g] = y[:, v0 + g * HEAD_DIM:v0 + (g + 1) * HEAD_DIM].astype(BF16)
    u_ref[...] = y[:, v0 + N_KV_HEADS * HEAD_DIM:].astype(BF16)


def inproj_ab(x, modtab, w_in, q_norm, k_norm, cos, sin, n_ctx):
    b, t, d = x.shape
    n_in = w_in.shape[1]
    s5_ch = n_in - (N_HEADS + 2 * N_KV_HEADS) * HEAD_DIM
    rep = N_HEADS // N_KV_HEADS
    tq = ROW_TILE
    nct = n_ctx // tq
    seg = lambda i: jnp.minimum(i // nct, 1) if nct > 0 else 1
    kern = functools.partial(_inproj_ab_kernel, scale=HEAD_DIM ** -0.5 * LOG2E)
    return pl.pallas_call(
        kern,
        grid=(b, t // tq),
        in_specs=[
            pl.BlockSpec((None, tq, d), lambda bi, i: (bi, i, 0)),
            pl.BlockSpec((None, None, 6, d), lambda bi, i: (bi, seg(i), 0, 0)),
            pl.BlockSpec((d, n_in), lambda bi, i: (0, 0)),
            pl.BlockSpec((1, HEAD_DIM), lambda bi, i: (0, 0)),
            pl.BlockSpec((1, HEAD_DIM), lambda bi, i: (0, 0)),
            pl.BlockSpec((tq, LANES), lambda bi, i: (i, 0)),
            pl.BlockSpec((tq, LANES), lambda bi, i: (i, 0)),
        ],
        out_specs=[
            pl.BlockSpec((None, N_KV_HEADS, rep, tq, HEAD_DIM), lambda bi, i: (bi, 0, 0, i, 0)),
            pl.BlockSpec((None, N_KV_HEADS, tq, HEAD_DIM), lambda bi, i: (bi, 0, i, 0)),
            pl.BlockSpec((None, N_KV_HEADS, tq, HEAD_DIM), lambda bi, i: (bi, 0, i, 0)),
            pl.BlockSpec((None, tq, s5_ch), lambda bi, i: (bi, i, 0)),
        ],
        out_shape=[
            jax.ShapeDtypeStruct((b, N_KV_HEADS, rep, t, HEAD_DIM), BF16),
            jax.ShapeDtypeStruct((b, N_KV_HEADS, t, HEAD_DIM), BF16),
            jax.ShapeDtypeStruct((b, N_KV_HEADS, t, HEAD_DIM), BF16),
            jax.ShapeDtypeStruct((b, t, s5_ch), BF16),
        ],
        compiler_params=_cparams(("parallel", "parallel")),
        name="inproj_ab",
    )(x, modtab, w_in, q_norm.reshape(1, -1), k_norm.reshape(1, -1), cos, sin)


def _inproj_cd_kernel(x_ref, mod_ref, w_ref, qln_ref, kvln_ref, wuq_ref, wukv_ref,
                      cos_ref, sin_ref, cosm_ref, sinm_ref,
                      qm_ref, km_ref, vm_ref, qd_ref, kd_ref, vd_ref, *, scale_c, scale_d, q_lora, kv_lora):
    x = x_ref[...]
    h = (x * (1.0 + mod_ref[1:2, :]) + mod_ref[0:1, :]).astype(BF16)
    y = _dot(h, w_ref[...])
    cos = cos_ref[...]
    sin = sin_ref[...]
    cosm = cosm_ref[...]
    sinm = sinm_ref[...]
    rep = N_HEADS // N_KV_HEADS
    cq = _rms(y[:, :q_lora], qln_ref[...]).astype(BF16)
    ckv = _rms(y[:, q_lora:q_lora + kv_lora], kvln_ref[...]).astype(BF16)
    q = _dot(cq, wuq_ref[...])
    kv = _dot(ckv, wukv_ref[...])
    o = q_lora + kv_lora
    qd0, kd0 = o, o + N_HEADS * HEAD_DIM
    vd0 = kd0 + N_KV_HEADS * HEAD_DIM
    kr0 = vd0 + N_KV_HEADS * HEAD_DIM
    k_rope = _rope_lanes(y[:, kr0:kr0 + LANES], cosm, sinm, C_ROPE // 4)
    for hh in range(N_HEADS):
        qn = q[:, hh * MLA_QK:hh * MLA_QK + HEAD_DIM]
        qr = _rope_lanes(q[:, hh * MLA_QK + HEAD_DIM:(hh + 1) * MLA_QK], cosm, sinm, C_ROPE // 4)
        qm_ref[hh, 0] = (jnp.concatenate([qn, qr], axis=-1) * scale_c).astype(BF16)
        kn = kv[:, hh * HEAD_DIM:(hh + 1) * HEAD_DIM]
        km_ref[hh] = jnp.concatenate([kn, k_rope], axis=-1).astype(BF16)
        vm_ref[hh] = kv[:, (N_HEADS + hh) * HEAD_DIM:(N_HEADS + hh + 1) * HEAD_DIM].astype(BF16)
        qdh = _rope_lanes(y[:, qd0 + hh * HEAD_DIM:qd0 + (hh + 1) * HEAD_DIM], cos, sin, HEAD_DIM // 4)
        qd_ref[hh // rep, hh % rep] = (qdh * scale_d).astype(BF16)
    for g in range(N_KV_HEADS):
        kdh = _rope_lanes(y[:, kd0 + g * HEAD_DIM:kd0 + (g + 1) * HEAD_DIM], cos, sin, HEAD_DIM // 4)
        kd_ref[g] = kdh.astype(BF16)
        vd_ref[g] = y[:, vd0 + g * HEAD_DIM:vd0 + (g + 1) * HEAD_DIM].astype(BF16)


def inproj_cd(x, modtab, w_in, q_ln, kv_ln, w_uq, w_ukv, cos, sin, cosm, sinm, n_ctx):
    b, t, d = x.shape
    n_in = w_in.shape[1]
    q_lora, kv_lora = q_ln.shape[0], kv_ln.shape[0]
    rep = N_HEADS // N_KV_HEADS
    tq = ROW_TILE
    nct = n_ctx // tq
    seg = lambda i: jnp.minimum(i // nct, 1) if nct > 0 else 1
    kern = functools.partial(_inproj_cd_kernel, scale_c=(HEAD_DIM + C_ROPE) ** -0.5 * LOG2E,
                             scale_d=HEAD_DIM ** -0.5 * LOG2E, q_lora=q_lora, kv_lora=kv_lora)
    full = lambda shape: pl.BlockSpec(shape, lambda bi, i: (0,) * len(shape))
    tab = pl.BlockSpec((tq, LANES), lambda bi, i: (i, 0))
    return pl.pallas_call(
        kern,
        grid=(b, t // tq),
        in_specs=[
            pl.BlockSpec((None, tq, d), lambda bi, i: (bi, i, 0)),
            pl.BlockSpec((None, None, 6, d), lambda bi, i: (bi, seg(i), 0, 0)),
            full((d, n_in)), full((1, q_lora)), full((1, kv_lora)),
            full(w_uq.shape), full(w_ukv.shape), tab, tab, tab, tab,
        ],
        out_specs=[
            pl.BlockSpec((None, N_HEADS, 1, tq, MLA_QK), lambda bi, i: (bi, 0, 0, i, 0)),
            pl.BlockSpec((None, N_HEADS, tq, MLA_QK), lambda bi, i: (bi, 0, i, 0)),
            pl.BlockSpec((None, N_HEADS, tq, HEAD_DIM), lambda bi, i: (bi, 0, i, 0)),
            pl.BlockSpec((None, N_KV_HEADS, rep, tq, HEAD_DIM), lambda bi, i: (bi, 0, 0, i, 0)),
            pl.BlockSpec((None, N_KV_HEADS, tq, HEAD_DIM), lambda bi, i: (bi, 0, i, 0)),
            pl.BlockSpec((None, N_KV_HEADS, tq, HEAD_DIM), lambda bi, i: (bi, 0, i, 0)),
        ],
        out_shape=[
            jax.ShapeDtypeStruct((b, N_HEADS, 1, t, MLA_QK), BF16),
            jax.ShapeDtypeStruct((b, N_HEADS, t, MLA_QK), BF16),
            jax.ShapeDtypeStruct((b, N_HEADS, t, HEAD_DIM), BF16),
            jax.ShapeDtypeStruct((b, N_KV_HEADS, rep, t, HEAD_DIM), BF16),
            jax.ShapeDtypeStruct((b, N_KV_HEADS, t, HEAD_DIM), BF16),
            jax.ShapeDtypeStruct((b, N_KV_HEADS, t, HEAD_DIM), BF16),
        ],
        compiler_params=_cparams(("parallel", "parallel")),
        name="inproj_cd",
    )(x, modtab, w_in, q_ln.reshape(1, -1), kv_ln.reshape(1, -1), w_uq, w_ukv, cos, sin, cosm, sinm)


def _prep_cd_weights(w_in, w_uq, w_ukv):
    d = w_in.shape[0]
    q_lora = w_uq.shape[0]
    kv_lora = w_ukv.shape[0]
    o = q_lora + kv_lora
    k_rope = w_in[:, o:o + C_ROPE]
    rest = w_in[:, o + C_ROPE:]
    w_in_r = jnp.concatenate([w_in[:, :o], rest, k_rope, jnp.zeros((d, LANES - C_ROPE), w_in.dtype)], axis=1)
    uq = w_uq.reshape(q_lora, N_HEADS, HEAD_DIM + C_ROPE)
    uq = jnp.pad(uq, ((0, 0), (0, 0), (0, MLA_QK - HEAD_DIM - C_ROPE))).reshape(q_lora, N_HEADS * MLA_QK)
    ukv = w_ukv.reshape(kv_lora, N_HEADS, 2, HEAD_DIM).transpose(0, 2, 1, 3).reshape(kv_lora, 2 * N_HEADS * HEAD_DIM)
    return w_in_r.astype(BF16), uq.astype(BF16), ukv.astype(BF16)


def _softmax_pv(blocks, sink_col):
    mx = None
    for s, _ in blocks:
        bm = jnp.max(s, axis=-1, keepdims=True)
        mx = bm if mx is None else jnp.maximum(mx, bm)
    if sink_col is not None:
        mx = jnp.maximum(mx, sink_col)
    den = None
    acc = None
    for s, v in blocks:
        p = jnp.exp2(s - mx)
        ps = jnp.sum(p, axis=-1, keepdims=True)
        den = ps if den is None else den + ps
        pv = _dot(p.astype(BF16), v)
        acc = pv if acc is None else acc + pv
    if sink_col is not None:
        den = den + jnp.exp2(sink_col - mx)
    return acc / den


def _exp2_pv(x, v):
    p = jnp.exp2(x.astype(BF16))
    return _dot(p, jnp.concatenate([v, jnp.ones((v.shape[0], LANES), BF16)], axis=-1))


def _online_softmax_pv(q, k_ref, v_ref, sink_col):
    n_keys = k_ref.shape[0]
    dv = v_ref.shape[1]
    m = sink_col if sink_col is not None else jnp.full((q.shape[0], 1), -jnp.inf, F32)
    acc = None
    for c0 in range(0, n_keys, KEY_CHUNK):
        c1 = min(c0 + KEY_CHUNK, n_keys)
        s = _dot_nt(q, k_ref[c0:c1, :])
        m_new = jnp.maximum(m, jnp.max(s, axis=-1, keepdims=True))
        pv = _exp2_pv(s - m_new, v_ref[c0:c1, :])
        acc = pv if acc is None else acc * jnp.exp2(m - m_new) + pv
        m = m_new
    den = acc[:, dv:dv + 1]
    if sink_col is not None:
        den = den + jnp.exp2(sink_col - m)
    return acc[:, :dv] / den


def _attn_kernel(sink_ref, q_ref, k_ref, v_ref, o_ref, *, n_ctx, tq, rep, hps, windowed, use_sink):
    gi = pl.program_id(1)
    qi = pl.program_id(2)
    t_all = k_ref.shape[1]
    dk = q_ref.shape[-1]
    dv = v_ref.shape[-1]
    nct = n_ctx // tq

    def head(h, ctx_tile):
        g = gi * hps + h
        q = q_ref[h].reshape(rep * tq, dk)
        kh = k_ref.at[h]
        vh = v_ref.at[h]
        if use_sink:
            row = lax.broadcasted_iota(I32, (rep * tq, 1), 0)
            sink_col = jnp.full((rep * tq, 1), sink_ref[g * rep], F32)
            for r in range(1, rep):
                sink_col = jnp.where(row >= r * tq, sink_ref[g * rep + r], sink_col)
            sink_col = sink_col * LOG2E
        else:
            sink_col = None
        if ctx_tile:
            o = _softmax_pv([(_dot_nt(q, kh[0:n_ctx, :]), vh[0:n_ctx, :])], sink_col)
        elif not windowed:
            o = _online_softmax_pv(q, kh, vh, sink_col)
        else:
            band = tq + 2 * D_WINDOW
            s0 = (qi - nct) * tq
            kstart = jnp.clip(n_ctx + s0 - D_WINDOW, n_ctx, t_all - band)
            kstart = pl.multiple_of(kstart, LANES)
            sb = _dot_nt(q, kh[pl.ds(kstart, band), :])
            rowq = lax.broadcasted_iota(I32, (rep * tq, band), 0) % tq + s0
            colk = lax.broadcasted_iota(I32, (rep * tq, band), 1) + (kstart - n_ctx)
            sb = jnp.where(jnp.abs(colk - rowq) <= D_WINDOW, sb, NEG_INF)
            o = _softmax_pv([(_dot_nt(q, kh[0:n_ctx, :]), vh[0:n_ctx, :]), (sb, vh[pl.ds(kstart, band), :])],
                            sink_col)
        for r in range(rep):
            c0 = (h * rep + r) * dv
            o_ref[:, c0:c0 + dv] = o[r * tq:(r + 1) * tq].astype(o_ref.dtype)

    @pl.when(qi < nct)
    def _():
        for h in range(hps):
            head(h, True)

    @pl.when(qi >= nct)
    def _():
        for h in range(hps):
            head(h, False)


def attention(q, k, v, n_ctx, *, windowed=False, sink=None, hps=ATTN_HEADS_PER_STEP):
    b, g, rep, t, dk = q.shape
    dv = v.shape[-1]
    tq = ROW_TILE
    use_sink = sink is not None
    if sink is None:
        sink = jnp.zeros((g * rep,), F32)
    kern = functools.partial(_attn_kernel, n_ctx=n_ctx, tq=tq, rep=rep, hps=hps, windowed=windowed,
                             use_sink=use_sink)
    return pl.pallas_call(
        kern,
        grid=(b, g // hps, t // tq),
        in_specs=[
            pl.BlockSpec(memory_space=pltpu.SMEM),
            pl.BlockSpec((None, hps, rep, tq, dk), lambda bi, gi, i: (bi, gi, 0, i, 0)),
            pl.BlockSpec((None, hps, t, dk), lambda bi, gi, i: (bi, gi, 0, 0)),
            pl.BlockSpec((None, hps, t, dv), lambda bi, gi, i: (bi, gi, 0, 0)),
        ],
        out_specs=pl.BlockSpec((None, tq, hps * rep * dv), lambda bi, gi, i: (bi, i, gi)),
        out_shape=jax.ShapeDtypeStruct((b, t, g * rep * dv), BF16),
        compiler_params=_cparams(("parallel", "parallel", "arbitrary")),
        name="attention_win" if windowed else "attention",
    )(sink.astype(F32), q, k, v)


def _s5_tables(lam_re, lam_im, log_dt, b_re, b_im, c_re, c_im, d_skip):
    n_groups, n_state = lam_re.shape[1:]
    L, gs = S5_CHUNK, S5_GROUP
    lam = lax.complex(lam_re.astype(F32), lam_im.astype(F32))
    dt = jnp.exp(log_dt.astype(F32))[..., None]
    lam_dt = lam * dt
    lam_bar = jnp.exp(lam_dt)
    b_bar = ((lam_bar - 1.0) / lam)[..., None] * lax.complex(b_re.astype(F32), b_im.astype(F32))
    c_mat = lax.complex(c_re.astype(F32), c_im.astype(F32))
    pw = jnp.exp(lam_dt[None] * jnp.arange(L + 1, dtype=F32)[:, None, None, None])
    kker = jnp.einsum("dgop,tdgp,dgpi->dtgoi", c_mat, pw[:L], b_bar).real
    s_in = jnp.arange(L)[:, None]
    s_out = jnp.arange(L)[None, :]
    tau_f = s_out - s_in
    tau_r = s_in - s_out
    kf = jnp.where((tau_f >= 0)[:, :, None, None, None], kker[0][jnp.clip(tau_f, 0, L - 1)], 0.0)
    kr = jnp.where((tau_r >= 0)[:, :, None, None, None], kker[1][jnp.clip(tau_r, 0, L - 1)], 0.0)
    kt = kf + kr
    gp = S5_PACK
    n_packs = n_groups // gp
    eye = jnp.eye(gp, dtype=F32)
    t6 = kt.transpose(2, 0, 4, 1, 3).reshape(n_packs, gp, L, gs, L, gs)
    d6 = d_skip.astype(F32).reshape(n_packs, gp, 1, gs, 1, 1) * (
        jnp.eye(L, dtype=F32)[None, None, :, None, :, None] * jnp.eye(gs, dtype=F32)[None, None, None, :, None, :])
    tmat = jnp.einsum("pgsctd,gh->psgcthd", t6 + d6, eye).reshape(n_packs, L * LANES, L * LANES)
    steps = jnp.arange(L, dtype=F32)[:, None, None]
    wf = jnp.exp(lam_dt[0][None] * (L - 1 - steps))[:, :, :, None] * b_bar[0][None]
    wr = jnp.exp(lam_dt[1][None] * steps)[:, :, :, None] * b_bar[1][None]
    def w_pack(w):
        w2 = jnp.stack([w.real, w.imag], axis=0).reshape(2, L, n_packs, gp, n_state, gs)
        return jnp.einsum("bspgqc,gh->psgcbhq", w2, eye).reshape(n_packs, L * LANES, 2 * gp * n_state)
    vf = c_mat[0][None] * jnp.exp(lam_dt[0][None] * (steps + 1))[:, :, None, :]
    vr = c_mat[1][None] * jnp.exp(lam_dt[1][None] * (L - steps))[:, :, None, :]
    def v_pack(vv):
        v2 = jnp.stack([vv.real, -vv.imag], axis=0).reshape(2, L, n_packs, gp, gs, n_state)
        return jnp.einsum("bspgcq,gh->pbgqshc", v2, eye).reshape(n_packs, 2 * gp * n_state, L * LANES)
    def lam_pack(l):
        return jnp.stack([l.real, l.imag], axis=0).reshape(2, n_packs, gp * n_state).transpose(1, 0, 2)
    lam_l = pw[L]
    return (tmat.astype(BF16),
            (w_pack(wf).astype(BF16), v_pack(vf).astype(BF16), lam_pack(lam_l[0])),
            (w_pack(wr).astype(BF16), v_pack(vr).astype(BF16), lam_pack(lam_l[1])))


def _s5_pass_kernel(*refs, reverse):
    if reverse:
        u_ref, yin_ref, w_ref, v_ref, lam_ref, y_ref, s_scr, h_scr = refs
    else:
        u_ref, t_ref, w_ref, v_ref, lam_ref, y_ref, s_scr, h_scr = refs

    @pl.when(pl.program_id(1) == 0)
    def _():
        h_scr[...] = jnp.zeros_like(h_scr)

    nchs, L, b, _ = u_ref.shape
    rows = nchs * b
    half = h_scr.shape[1] // 2
    xg = jnp.concatenate([u_ref[:, s].reshape(rows, LANES) for s in range(L)], axis=-1)
    s_scr[...] = _dot(xg, w_ref[...])
    lr = jnp.broadcast_to(lam_ref[0:1, :], (b, half))
    li = jnp.broadcast_to(lam_ref[1:2, :], (b, half))

    def step(k, carry):
        hr, hi = carry
        j = nchs - 1 - k if reverse else k
        r0 = pl.multiple_of(j * b, b)
        sr = s_scr[pl.ds(r0, b), 0:half]
        si = s_scr[pl.ds(r0, b), half:]
        s_scr[pl.ds(r0, b), 0:half] = hr
        s_scr[pl.ds(r0, b), half:] = hi
        return lr * hr - li * hi + sr, lr * hi + li * hr + si

    hr, hi = lax.fori_loop(0, nchs, step, (h_scr[:, 0:half], h_scr[:, half:]))
    h_scr[:, 0:half] = hr
    h_scr[:, half:] = hi
    y = _dot(s_scr[...].astype(BF16), v_ref[...])
    if not reverse:
        y = y + _dot(xg, t_ref[...])
    for s in range(L):
        blk = y[:, s * LANES:(s + 1) * LANES].reshape(nchs, b, LANES)
        if reverse:
            blk = blk + yin_ref[:, s]
        y_ref[:, s] = blk


def s5_apply(u, tables, n_ctx):
    tmat, fwd, rev = tables
    b, t, ch = u.shape
    L = S5_CHUNK
    n_packs = ch // LANES
    seg = ROW_TILE
    nseg, nct, nchs = t // seg, n_ctx // seg, seg // L
    width = L * LANES
    n_state2 = fwd[0].shape[-1]
    ut = u.transpose(1, 0, 2).reshape(t // L, L, b, ch)
    blk = pl.BlockSpec((nchs, L, b, LANES), lambda p, i: (i, 0, 0, p))
    rseg = lambda i: jnp.where(i < nct, nct - 1 - i, nseg - 1 - (i - nct))
    rblk = pl.BlockSpec((nchs, L, b, LANES), lambda p, i: (rseg(i), 0, 0, p))
    tab = lambda shape: pl.BlockSpec((None,) + shape, lambda p, i: (p, 0, 0))
    scratch = [pltpu.VMEM((nchs * b, n_state2), F32), pltpu.VMEM((b, n_state2), F32)]
    y_shape = jax.ShapeDtypeStruct((t // L, L, b, ch), F32)
    y_f = pl.pallas_call(
        functools.partial(_s5_pass_kernel, reverse=False),
        grid=(n_packs, nseg),
        in_specs=[blk, tab((width, width)), tab((width, n_state2)), tab((n_state2, width)), tab((2, n_state2 // 2))],
        out_specs=blk,
        out_shape=y_shape,
        scratch_shapes=scratch,
        compiler_params=_cparams(("parallel", "arbitrary")),
        name="s5_forward",
    )(ut, tmat, *fwd)
    y = pl.pallas_call(
        functools.partial(_s5_pass_kernel, reverse=True),
        grid=(n_packs, nseg),
        in_specs=[rblk, rblk, tab((width, n_state2)), tab((n_state2, width)), tab((2, n_state2 // 2))],
        out_specs=rblk,
        out_shape=y_shape,
        scratch_shapes=scratch,
        input_output_aliases={1: 0},
        compiler_params=_cparams(("parallel", "arbitrary")),
        name="s5_reverse",
    )(ut, y_f, *rev)
    return y.reshape(t, b, ch).transpose(1, 0, 2)


def _gelu_tanh(x):
    return 0.5 * x * (1.0 + jnp.tanh(math.sqrt(2.0 / math.pi) * (x + 0.044715 * (x * x * x))))


def _post_kernel(*refs, alpha, glu, n_exp):
    if glu:
        (x_ref, a_ref, s_ref, wg_ref, bg_ref, wm_ref, mod_ref, g_ref, b_ref, wrh_ref, wrl_ref, br_ref,
         x1_ref, idx_ref, gate_ref, rank_ref, cnt_ref, cnt_scr) = refs
    else:
        (x_ref, a_ref, s_ref, wm_ref, mod_ref, g_ref, b_ref, wrh_ref, wrl_ref, br_ref,
         x1_ref, idx_ref, gate_ref, rank_ref, cnt_ref, cnt_scr) = refs
    first = (pl.program_id(0) == 0) & (pl.program_id(1) == 0)

    @pl.when(first)
    def _():
        cnt_scr[...] = jnp.zeros_like(cnt_scr)

    tq, d = x_ref.shape
    half = a_ref.shape[1]
    if glu:
        z = _gelu_tanh(s_ref[...])
        gate = jax.nn.sigmoid(_dot(z.astype(BF16), wg_ref[...]) + bg_ref[...])
        second = (z * gate).astype(BF16)
    else:
        second = s_ref[...]
    mix = _dot(a_ref[...], wm_ref[0:half, :]) + _dot(second, wm_ref[half:, :])
    x1 = _layer_norm(alpha * x_ref[...] + mod_ref[2:3, :] * mix, g_ref[...], b_ref[...])
    x1_ref[...] = x1
    h2 = x1 * (1.0 + mod_ref[4:5, :]) + mod_ref[3:4, :]

    h_hi, h_lo = _split_bf16(h2)
    lg = _dot(h_hi, wrh_ref[...]) + _dot(h_lo, wrh_ref[...]) + _dot(h_hi, wrl_ref[...])
    logits = lg.T[0:n_exp, :] + br_ref[...]
    eidx = lax.broadcasted_iota(I32, (n_exp, tq), 0)
    work = logits
    tops, sels = [], []
    for k in range(TOP_K):
        m = jnp.max(work, axis=0, keepdims=True)
        ik = jnp.min(jnp.where(work == m, eidx, n_exp), axis=0, keepdims=True)
        sel = eidx == ik
        work = jnp.where(sel, -jnp.inf, work)
        tops.append(m)
        sels.append(sel)
        idx_ref[k:k + 1, :] = ik
    exps = [jnp.exp(tk - tops[0]) for tk in tops]
    den = exps[0] + exps[1] + exps[2] + exps[3]
    for k in range(TOP_K):
        gate_ref[k:k + 1, :] = exps[k] / den
    onehot = jnp.zeros((n_exp, tq), F32)
    for sel in sels:
        onehot = onehot + sel.astype(F32)
    tri = (lax.broadcasted_iota(I32, (tq, tq), 0) < lax.broadcasted_iota(I32, (tq, tq), 1)).astype(BF16)
    before = _dot(onehot.astype(BF16), tri) + cnt_scr[...]
    for k in range(TOP_K):
        rk = jnp.sum(jnp.where(sels[k], before, 0.0), axis=0, keepdims=True)
        rank_ref[k:k + 1, :] = rk.astype(I32)
    cnt_scr[...] += jnp.sum(onehot, axis=1, keepdims=True)
    cnt_ref[...] = jnp.broadcast_to(cnt_scr[...], cnt_ref.shape).astype(I32)


def post_mixer(x, a, s, modtab, w_mix, ln_g, ln_b, w_router, b_router, n_ctx, alpha, glu_w=None, glu_b=None):
    b, t, d = x.shape
    half = a.shape[-1]
    n_exp = w_router.shape[1]
    tq = ROW_TILE
    nct = n_ctx // tq
    seg = lambda i: jnp.minimum(i // nct, 1) if nct > 0 else 1
    glu = glu_w is not None
    wr_hi, wr_lo = _split_bf16(jnp.pad(w_router.astype(F32), ((0, 0), (0, LANES - n_exp))))
    full = lambda shape: pl.BlockSpec(shape, lambda bi, i: (0,) * len(shape))
    tok = lambda width: pl.BlockSpec((None, tq, width), lambda bi, i: (bi, i, 0))
    in_specs = [tok(d), tok(half), tok(half)]
    args = [x, a, s]
    if glu:
        in_specs += [full((half, half)), full((1, half))]
        args += [glu_w.astype(BF16), glu_b.reshape(1, half)]
    in_specs += [full((d, d)), pl.BlockSpec((None, None, 6, d), lambda bi, i: (bi, seg(i), 0, 0)),
                 full((1, d)), full((1, d)), full((d, LANES)), full((d, LANES)), full((n_exp, 1))]
    args += [w_mix.astype(BF16), modtab, ln_g.reshape(1, d), ln_b.reshape(1, d), wr_hi, wr_lo,
             b_router.reshape(n_exp, 1)]
    nt = t // tq
    lane_out = lambda dt: (pl.BlockSpec((TOP_K, tq), lambda bi, i: (0, bi * nt + i)),
                           jax.ShapeDtypeStruct((TOP_K, b * t), dt))
    outs = [
        (tok(d), jax.ShapeDtypeStruct((b, t, d), F32)),
        lane_out(I32), lane_out(F32), lane_out(I32),
        (pl.BlockSpec((n_exp, LANES), lambda bi, i: (0, 0)), jax.ShapeDtypeStruct((n_exp, LANES), I32)),
    ]
    kern = functools.partial(_post_kernel, alpha=alpha, glu=glu, n_exp=n_exp)
    return pl.pallas_call(
        kern,
        grid=(b, nt),
        in_specs=in_specs,
        out_specs=[o[0] for o in outs],
        out_shape=[o[1] for o in outs],
        scratch_shapes=[pltpu.VMEM((n_exp, 1), F32)],
        compiler_params=_cparams(("arbitrary", "arbitrary")),
        name="post_mixer_glu" if glu else "post_mixer",
    )(*args)


def _route_kernel(start_ref, idx_ref, rank_ref, pos_ref, *, n_exp, tq):
    n = idx_ref.shape[1]
    eidx = lax.broadcasted_iota(I32, (n_exp, n), 0)
    for k in range(TOP_K):
        sel = eidx == idx_ref[k:k + 1, :]
        base = jnp.sum(jnp.where(sel, start_ref[...], 0.0), axis=0, keepdims=True)
        pos = base.astype(I32) + rank_ref[k:k + 1, :]
        for ti in range(n // tq):
            pos_ref[ti, :, k * tq:(k + 1) * tq] = pos[:, ti * tq:(ti + 1) * tq]


def route_positions(group_start, idx, rank):
    n_exp = group_start.shape[0]
    n_tok = idx.shape[1]
    tq = ROW_TILE
    n_tiles = n_tok // tq
    per_step = math.gcd(n_tiles, 8)
    return pl.pallas_call(
        functools.partial(_route_kernel, n_exp=n_exp, tq=tq),
        grid=(n_tiles // per_step,),
        in_specs=[
            pl.BlockSpec((n_exp, 1), lambda i: (0, 0)),
            pl.BlockSpec((TOP_K, per_step * tq), lambda i: (0, i)),
            pl.BlockSpec((TOP_K, per_step * tq), lambda i: (0, i)),
        ],
        out_specs=pl.BlockSpec((per_step, 1, TOP_K * tq), lambda i: (i, 0, 0)),
        out_shape=jax.ShapeDtypeStruct((n_tiles, 1, TOP_K * tq), I32),
        compiler_params=_cparams(("parallel",)),
        name="route_positions",
    )(group_start.astype(F32).reshape(n_exp, 1), idx, rank)


def _dispatch_kernel(pos_hbm, x_ref, mod_ref, xs_hbm, idx_smem, hbuf0, hbuf1, idx_sem, row_sem, *, tq, n_tiles):
    i = pl.program_id(0)
    slot = i % 2
    n_sub = x_ref.shape[1] // LANES
    bufs = (hbuf0, hbuf1)

    def idx_copy(tile, sl):
        return pltpu.make_async_copy(pos_hbm.at[tile], idx_smem.at[sl], idx_sem.at[sl])

    def rows_wait(sl):
        for _ in range(TOP_K):
            pltpu.make_async_copy(bufs[sl], xs_hbm.at[pl.ds(0, tq * n_sub), :], row_sem.at[sl]).wait()

    @pl.when(i == 0)
    def _():
        idx_copy(0, 0).start()

    idx_copy(i, slot).wait()

    @pl.when(i + 1 < n_tiles)
    def _():
        idx_copy(i + 1, 1 - slot).start()

    def step(cur):
        @pl.when(i >= 2)
        def _():
            rows_wait(cur)

        h2 = x_ref[...] * (1.0 + mod_ref[4:5, :]) + mod_ref[3:4, :]
        _store_token_tiles(bufs[cur], h2)
        for k in range(TOP_K):
            for r in range(tq):
                dst = pl.multiple_of(idx_smem[cur, 0, k * tq + r] * n_sub, n_sub)
                pltpu.make_async_copy(bufs[cur].at[pl.ds(r * n_sub, n_sub), :],
                                      xs_hbm.at[pl.ds(dst, n_sub), :], row_sem.at[cur]).start(priority=r % 2)

        @pl.when(i == n_tiles - 1)
        def _():
            rows_wait(1 - cur)
            rows_wait(cur)

    @pl.when(slot == 0)
    def _():
        step(0)

    @pl.when(slot == 1)
    def _():
        step(1)


def moe_dispatch(pos_tiles, x1, modtab, n_ctx):
    b, t, d = x1.shape
    n_tiles = pos_tiles.shape[0]
    tq = ROW_TILE
    nt = t // tq
    nct = n_ctx // tq
    seg = lambda i: jnp.minimum((i % nt) // nct, 1) if nct > 0 else 1
    n_sub = d // LANES
    assert n_tiles == b * nt and n_tiles >= 2
    return pl.pallas_call(
        functools.partial(_dispatch_kernel, tq=tq, n_tiles=n_tiles),
        grid=(n_tiles,),
        in_specs=[pl.BlockSpec(memory_space=pl.ANY),
                  pl.BlockSpec((tq, d), lambda i: (i, 0)),
                  pl.BlockSpec((None, None, 6, d), lambda i: (i // nt, seg(i), 0, 0))],
        out_specs=pl.BlockSpec(memory_space=pl.ANY),
        out_shape=jax.ShapeDtypeStruct((TOP_K * b * t * n_sub, LANES), F32),
        scratch_shapes=[
            pltpu.SMEM((2, 1, TOP_K * tq), I32),
            pltpu.VMEM((tq * n_sub, LANES), F32),
            pltpu.VMEM((tq * n_sub, LANES), F32),
            pltpu.SemaphoreType.DMA((2,)),
            pltpu.SemaphoreType.DMA((2,)),
        ],
        compiler_params=_cparams(("arbitrary",)),
        name="moe_dispatch",
    )(pos_tiles, x1.reshape(b * t, d), modtab)


def _moe_kernel(e_ref, j_ref, lo_ref, hi_ref, first_ref, new_ref, x_ref, wi_ref, bi_ref, wo_ref, bo_ref,
                y_ref, wi_b, wo_b, *, n_sub):
    i = pl.program_id(0)
    lo = lo_ref[i]
    hi = hi_ref[i]

    @pl.when(new_ref[i] == 1)
    def _():
        wi_b[...] = wi_ref[...].astype(BF16)
        wo_b[...] = wo_ref[...].astype(BF16)

    tm = x_ref.shape[0] // n_sub
    sub = tm // MOE_CHAINS

    def expert_rows(r0):
        x = _load_token_tiles(x_ref, r0 * n_sub, sub, n_sub).astype(BF16)
        z = _dot(x, wi_b[...]) + bi_ref[...]
        f = z.shape[1] // 2
        glu = jnp.minimum(z[:, :f], SWIGLU_LIMIT)
        lin = jnp.clip(z[:, f:], -SWIGLU_LIMIT, SWIGLU_LIMIT)
        act = glu * jax.nn.sigmoid(SWIGLU_ALPHA * glu) * (lin + 1.0)
        y = _dot(act.astype(BF16), wo_b[...]) + bo_ref[...]
        row = lax.broadcasted_iota(I32, (sub, 1), 0) + r0
        return y, (row >= lo) & (row < hi)

    def out_rows(r0):
        return y_ref.at[pl.ds(r0 * n_sub, sub * n_sub), :]

    @pl.when((hi > lo) & (first_ref[i] == 1))
    def _():
        for r0 in range(0, tm, sub):
            y, mine = expert_rows(r0)
            _store_token_tiles(out_rows(r0), jnp.where(mine, y, 0.0))

    @pl.when((hi > lo) & (first_ref[i] == 0))
    def _():
        for r0 in range(0, tm, sub):
            y, mine = expert_rows(r0)
            _store_token_tiles(out_rows(r0), jnp.where(mine, y, _load_token_tiles(out_rows(r0), 0, sub, n_sub)))


def moe_experts(xs, items, w_in, b_in, w_out, b_out, layer):
    _, n_exp, d, f2 = w_in.shape
    n_sub = d // LANES
    n_items = items[0].shape[0]
    tm = MOE_BLOCK
    wmap = lambda i, e, j, lo, hi, fi, nw: (layer, e[i], 0, 0)
    xmap = lambda i, e, j, lo, hi, fi, nw: (j[i], 0)
    grid_spec = pltpu.PrefetchScalarGridSpec(
        num_scalar_prefetch=6,
        grid=(n_items,),
        in_specs=[
            pl.BlockSpec((tm * n_sub, LANES), xmap),
            pl.BlockSpec((None, None, d, f2), wmap),
            pl.BlockSpec((None, None, 1, f2), wmap),
            pl.BlockSpec((None, None, f2 // 2, d), wmap),
            pl.BlockSpec((None, None, 1, d), wmap),
        ],
        out_specs=pl.BlockSpec((tm * n_sub, LANES), xmap),
        scratch_shapes=[pltpu.VMEM((d, f2), BF16), pltpu.VMEM((f2 // 2, d), BF16)],
    )
    return pl.pallas_call(
        functools.partial(_moe_kernel, n_sub=n_sub),
        grid_spec=grid_spec,
        out_shape=jax.ShapeDtypeStruct(xs.shape, F32),
        compiler_params=_cparams(("arbitrary",)),
        name="moe_experts",
    )(*items, xs, w_in, b_in.reshape(b_in.shape[0], n_exp, 1, f2), w_out, b_out.reshape(b_out.shape[0], n_exp, 1, d))


def _combine_kernel(pos_hbm, x_ref, g_ref, mod_ref, lg_ref, lb_ref, ys_hbm, o_ref,
                    idx_smem, ybuf0, ybuf1, idx_sem, row_sem, *, alpha, tq, n_tiles):
    i = pl.program_id(0)
    slot = i % 2
    d = x_ref.shape[1]
    n_sub = d // LANES
    bufs = (ybuf0, ybuf1)

    def idx_copy(tile, sl):
        return pltpu.make_async_copy(pos_hbm.at[tile], idx_smem.at[sl], idx_sem.at[sl])

    def gather_start(sl, rows):
        for r in rows:
            src = pl.multiple_of(idx_smem[sl, 0, r] * n_sub, n_sub)
            pltpu.make_async_copy(ys_hbm.at[pl.ds(src, n_sub), :], bufs[sl].at[pl.ds(r * n_sub, n_sub), :],
                                  row_sem.at[sl]).start(priority=r % 2)

    def rows_wait(sl):
        pltpu.make_async_copy(ys_hbm.at[pl.ds(0, TOP_K * tq * n_sub), :], bufs[sl], row_sem.at[sl]).wait()

    @pl.when(i == 0)
    def _():
        idx_copy(0, 0).start()
        idx_copy(0, 0).wait()
        gather_start(0, range(TOP_K * tq))
        idx_copy(1, 1).start()

    @pl.when(i + 1 < n_tiles)
    def _():
        idx_copy(i + 1, 1 - slot).wait()

    def step(cur):
        nxt = 1 - cur
        rows_wait(cur)

        @pl.when(i + 2 < n_tiles)
        def _():
            idx_copy(i + 2, cur).start()

        gates = g_ref[...]
        per_chunk = TOP_K * tq // n_sub
        parts = []
        for c in range(n_sub):
            gather_start(nxt, range(c * per_chunk, (c + 1) * per_chunk))
            acc = None
            for k in range(TOP_K):
                term = gates[:, k:k + 1] * bufs[cur][pl.ds(k * tq * n_sub + c, tq, stride=n_sub), :]
                acc = term if acc is None else acc + term
            parts.append(acc)
        y = jnp.concatenate(parts, axis=-1)
        o_ref[...] = _layer_norm(alpha * x_ref[...] + mod_ref[5:6, :] * y, lg_ref[...], lb_ref[...])

        @pl.when(i == n_tiles - 1)
        def _():
            rows_wait(nxt)

    @pl.when(slot == 0)
    def _():
        step(0)

    @pl.when(slot == 1)
    def _():
        step(1)


def moe_combine(x1, pos_tiles, gates_tok, modtab, ln_g, ln_b, ys, n_ctx, alpha, lat_only=False):
    b, t, d = x1.shape
    tq = ROW_TILE
    nt = t // tq
    n_tiles = b * nt
    nct = n_ctx // tq
    seg = lambda i: jnp.minimum((i % nt) // nct, 1) if nct > 0 else 1
    assert n_tiles >= 2, "the combine pipeline prefetches one tile ahead"
    kern = functools.partial(_combine_kernel, alpha=alpha, tq=tq, n_tiles=n_tiles)
    if lat_only:
        nlt = nt - nct
        out_map = lambda i: ((i // nt) * nlt + jnp.maximum(i % nt - nct, 0), 0)
        out_rows = b * nlt * tq
    else:
        out_map = lambda i: (i, 0)
        out_rows = b * t
    out = pl.pallas_call(
        kern,
        grid=(n_tiles,),
        in_specs=[
            pl.BlockSpec(memory_space=pl.ANY),
            pl.BlockSpec((tq, d), lambda i: (i, 0)),
            pl.BlockSpec((tq, TOP_K), lambda i: (i, 0)),
            pl.BlockSpec((None, None, 6, d), lambda i: (i // nt, seg(i), 0, 0)),
            pl.BlockSpec((1, d), lambda i: (0, 0)),
            pl.BlockSpec((1, d), lambda i: (0, 0)),
            pl.BlockSpec(memory_space=pl.ANY),
        ],
        out_specs=pl.BlockSpec((tq, d), out_map),
        out_shape=jax.ShapeDtypeStruct((out_rows, d), F32),
        scratch_shapes=[
            pltpu.SMEM((2, 1, TOP_K * tq), I32),
            pltpu.VMEM((TOP_K * tq * (d // LANES), LANES), F32),
            pltpu.VMEM((TOP_K * tq * (d // LANES), LANES), F32),
            pltpu.SemaphoreType.DMA((2,)),
            pltpu.SemaphoreType.DMA((2,)),
        ],
        compiler_params=_cparams(("arbitrary",)),
        name="moe_combine",
    )(pos_tiles, x1.reshape(b * t, d), gates_tok, modtab, ln_g.reshape(1, d), ln_b.reshape(1, d), ys)
    return out.reshape(b, out_rows // b, d)


def _group_items(counts, n_slots, tm):
    n_exp = counts.shape[0]
    le = (jnp.arange(n_exp)[:, None] <= jnp.arange(n_exp)[None, :]).astype(I32)
    g_end = counts @ le
    g_start = g_end - counts
    first_blk = g_start // tm
    n_blk = jnp.where(counts > 0, (g_end - 1) // tm - first_blk + 1, 0)
    i_end = n_blk @ le
    i_start = i_end - n_blk
    n_items = n_slots // tm + n_exp - 1
    it = jnp.arange(n_items, dtype=I32)
    valid = it < i_end[-1]
    e = jnp.minimum(jnp.sum((i_end[None, :] <= it[:, None]).astype(I32), axis=1), n_exp - 1)
    onehot = (e[:, None] == jnp.arange(n_exp)[None, :]).astype(I32)
    pick = lambda tbl: onehot @ tbl
    blk = pick(first_blk) + it - pick(i_start)
    lo = jnp.clip(pick(g_start) - blk * tm, 0, tm)
    hi = jnp.clip(pick(g_end) - blk * tm, 0, tm)
    last = jnp.maximum(i_end[-1] - 1, 0)
    e_last = jnp.sum(jnp.where(it == last, e, 0))
    e = jnp.where(valid, e, e_last)
    blk = jnp.where(valid, blk, n_slots // tm - 1)
    lo = jnp.where(valid, lo, 0)
    hi = jnp.where(valid, hi, 0)
    prev = lambda a: jnp.concatenate([jnp.full((1,), -1, I32), a[:-1]])
    first = (blk != prev(blk)).astype(I32)
    new_e = (e != prev(e)).astype(I32)
    return g_start, tuple(a.astype(I32) for a in (e, blk, lo, hi, first, new_e))


def kernel(x, c, ctx, c_ctx, w_mod, b_mod, ln_g, ln_b, w_mix_out, w_router, b_router, w_exp_in, b_exp_in, w_exp_out, b_exp_out, w_in_ab, a_q_norm, a_k_norm, s5_lam_re, s5_lam_im, s5_log_dt, s5_b_re, s5_b_im, s5_c_re, s5_c_im, s5_d, w_glu, b_glu, w_in_cd, c_q_norm, c_kv_norm, w_uq, w_ukv, d_sink):
    batch, n_lat, d = x.shape
    n_ctx = ctx.shape[1]
    depth = w_mod.shape[0]
    alpha = (2.0 * depth) ** 0.25
    t = n_ctx + n_lat
    n_tok = batch * t

    xs = jnp.concatenate([ctx, x], axis=1)
    n_rows = -(-(batch + 1) // SUBLANES) * SUBLANES
    cvec = jnp.zeros((n_rows, d), F32).at[:batch].set(c).at[batch].set(c_ctx)
    mods = modulation(cvec, w_mod, b_mod).reshape(depth, n_rows, 6, d)
    modtab = jnp.stack([jnp.broadcast_to(mods[:, batch:batch + 1], (depth, batch, 6, d)), mods[:, :batch]],
                       axis=2)

    cos_h, sin_h = _rope_tables(n_ctx, n_lat, HEAD_DIM)
    cos_m, sin_m = _rope_tables(n_ctx, n_lat, C_ROPE)

    for layer in range(depth):
        i = layer // 2
        mt = modtab[layer]
        if layer % 2 == 0:
            q, k, v, u = inproj_ab(xs, mt, w_in_ab[i].astype(BF16), a_q_norm[i], a_k_norm[i], cos_h, sin_h, n_ctx)
            att = attention(q, k, v, n_ctx)
            tables = _s5_tables(s5_lam_re[i], s5_lam_im[i], s5_log_dt[i], s5_b_re[i], s5_b_im[i],
                                s5_c_re[i], s5_c_im[i], s5_d[i])
            second = s5_apply(u, tables, n_ctx)
            glu_w, glu_b = w_glu[i], b_glu[i]
        else:
            w_in_r, uq_r, ukv_r = _prep_cd_weights(w_in_cd[i], w_uq[i], w_ukv[i])
            qm, km, vm, qd, kd, vd = inproj_cd(xs, mt, w_in_r, c_q_norm[i], c_kv_norm[i], uq_r, ukv_r,
                                               cos_h, sin_h, cos_m, sin_m, n_ctx)
            att = attention(qm, km, vm, n_ctx, hps=N_HEADS)
            second = attention(qd, kd, vd, n_ctx, windowed=True, sink=d_sink[i])
            glu_w = glu_b = None
        x1, idx, gates, rank, counts = post_mixer(
            xs, att, second, mt, w_mix_out[layer], ln_g[layer, 0], ln_b[layer, 0],
            w_router[layer], b_router[layer], n_ctx, alpha, glu_w, glu_b)
        g_start, items = _group_items(counts[:, 0], TOP_K * n_tok, MOE_BLOCK)
        pos_tiles = route_positions(g_start, idx, rank)
        sorted_rows = moe_dispatch(pos_tiles, x1, mt, n_ctx)
        ys = moe_experts(sorted_rows, items, w_exp_in, b_exp_in, w_exp_out, b_exp_out, layer)
        xs = moe_combine(x1, pos_tiles, gates.T, mt, ln_g[layer, 1], ln_b[layer, 1], ys, n_ctx, alpha,
                         lat_only=layer == depth - 1)
    return xs
```
